```python
import math
import jax, jax.numpy as jnp
from jax import lax
import numpy as np

D_MODEL = 4096
BATCH = 1
SEQ = 8192
DEPTH = 2

CTX_LEN = 256
GRID_W = 64
N_MIXERS = 2
N_HEADS = 16
HEAD_DIM = D_MODEL // (2 * N_HEADS)
V_HEAD_DIM = 2 * HEAD_DIM
ROPE_AXIS_DIM = HEAD_DIM // 2
ROPE_BASE = 10000.0
CONV_WIDTH = 3
D_FF = 11008
N_EXPERTS = 8
TOP_K = 2
D_FF_EXPERT = 4096
N_MOD = 6
Q_BLOCK = 128
NORM_EPS = 1e-6
N_CONV_LAYERS = (DEPTH + 1) // 2
N_ATTN_LAYERS = DEPTH // 2

kernel_name = "hybrid_shortconv_diffattn_moe_dit"


def rmsnorm(t, gain):
    tf = t.astype(jnp.float32)
    tf = tf * lax.rsqrt(jnp.mean(tf * tf, axis=-1, keepdims=True) + NORM_EPS)
    return tf.astype(t.dtype) * gain


def modulate(t, shift, scale):
    return t * (1.0 + scale) + shift


def centred_depthwise_conv3(t, w):
    L = t.shape[1]
    tp = jnp.pad(t, ((0, 0), (1, 1), (0, 0)))
    return tp[:, :L] * w[0] + tp[:, 1:L + 1] * w[1] + tp[:, 2:] * w[2]


def short_conv_mixer(u, w_in, w_conv, w_out):
    b, cg, v = jnp.split(u @ w_in, 3, axis=-1)
    z = centred_depthwise_conv3(cg * v, w_conv)
    return (b * z) @ w_out


def swiglu(h, w_gu, w_down):
    g, up = jnp.split(h @ w_gu, 2, axis=-1)
    return (jax.nn.silu(g) * up) @ w_down


def moe_swiglu(h, w_router, w_gu, w_down):
    logits = (h @ w_router).astype(jnp.float32)
    top_v, top_i = lax.top_k(logits, TOP_K)
    top_w = jax.nn.softmax(top_v, axis=-1)
    combine = jnp.sum(jax.nn.one_hot(top_i, N_EXPERTS, dtype=jnp.float32) * top_w[..., None], axis=-2)
    combine = combine.astype(h.dtype)
    out = jnp.zeros_like(h)
    for e in range(N_EXPERTS):
        out = out + combine[..., e:e + 1] * swiglu(h, w_gu[e], w_down[e])
    return out


def axial_rope_tables(n_tokens):
    rows = n_tokens // GRID_W
    row = jnp.repeat(jnp.arange(rows, dtype=jnp.int32), GRID_W)
    col = jnp.tile(jnp.arange(GRID_W, dtype=jnp.int32), rows)
    n_freq = ROPE_AXIS_DIM // 2
    inv = ROPE_BASE ** (-jnp.arange(n_freq, dtype=jnp.float32) / n_freq)
    ang = jnp.stack([row, col], axis=-1).astype(jnp.float32)[..., None] * inv
    return jnp.cos(ang), jnp.sin(ang)


def apply_axial_rope(t, cos, sin):
    B, S, H, M, _ = t.shape
    n_freq = ROPE_AXIS_DIM // 2
    tr = t.reshape(B, S, H, M, 2, 2, n_freq)
    t1, t2 = tr[..., 0, :], tr[..., 1, :]
    cs = cos[None, :, None, None].astype(t.dtype)
    sn = sin[None, :, None, None].astype(t.dtype)
    out = jnp.stack([t1 * cs - t2 * sn, t1 * sn + t2 * cs], axis=-2)
    return out.reshape(t.shape)


def diff_attend(q, k, v, lam):
    s = jnp.einsum('bqhmd,bkhmd->bhmqk', q, k).astype(jnp.float32) * (HEAD_DIM ** -0.5)
    p = jax.nn.softmax(s, axis=-1)
    a = (p[:, :, 0] - lam * p[:, :, 1]).astype(v.dtype)
    return jnp.einsum('bhqk,bkhe->bqhe', a, v)


def diff_attention(u, uc, w_qkv, w_o, q_gain, k_gain, lambdas, subln_gain, lambda_init, cos, sin, with_ctx_out):
    def project(t):
        B, L, _ = t.shape
        q, k, v = jnp.split(t @ w_qkv, 3, axis=-1)
        q = rmsnorm(q.reshape(B, L, N_HEADS, 2, HEAD_DIM), q_gain)
        k = rmsnorm(k.reshape(B, L, N_HEADS, 2, HEAD_DIM), k_gain)
        return q, k, v.reshape(B, L, N_HEADS, V_HEAD_DIM)

    q, k, v = project(u)
    q = apply_axial_rope(q, cos, sin)
    k = apply_axial_rope(k, cos, sin)
    qc, kc, vc = project(uc)
    lf = lambdas.astype(jnp.float32)
    lam = jnp.exp(jnp.sum(lf[0] * lf[1])) - jnp.exp(jnp.sum(lf[2] * lf[3])) + lambda_init

    k_all = jnp.concatenate([k, kc], axis=1)
    v_all = jnp.concatenate([v, vc], axis=1)
    B, S = u.shape[0], u.shape[1]
    nblk = S // Q_BLOCK
    qb = jnp.moveaxis(q.reshape(B, nblk, Q_BLOCK, N_HEADS, 2, HEAD_DIM), 1, 0)
    ob = lax.map(lambda qi: diff_attend(qi, k_all, v_all, lam), qb)
    o = jnp.moveaxis(ob, 0, 1).reshape(B, S, N_HEADS, V_HEAD_DIM)

    def finish(oh):
        oh = rmsnorm(oh, subln_gain) * (1.0 - lambda_init)
        return oh.reshape(oh.shape[0], oh.shape[1], -1) @ w_o

    y = finish(o)
    yc = finish(diff_attend(qc, kc, vc, lam)) if with_ctx_out else None
    return y, yc


def setup_inputs(seed: int = 0) -> dict:
    key = jax.random.key(seed)
    ks = jax.random.split(key, 24)
    D = D_MODEL

    def nrm(k, shape, scale):
        return jax.random.normal(k, shape, jnp.float32) * scale

    return {
        "x": nrm(ks[0], (BATCH, SEQ, D), 1.0),
        "c": nrm(ks[1], (BATCH, D), 1.0),
        "ctx": nrm(ks[2], (BATCH, CTX_LEN, D), 1.0),
        "c_ctx": nrm(ks[3], (D,), 1.0),
        "w_mod": nrm(ks[4], (DEPTH, D, N_MOD * D), 0.5 * D ** -0.5),
        "b_mod": nrm(ks[5], (DEPTH, N_MOD * D), 0.02),
        "norm_gain": 1.0 + nrm(ks[6], (DEPTH, 2, D), 0.02),
        "conv_w_in": nrm(ks[7], (N_CONV_LAYERS, D, 3 * D), D ** -0.5),
        "conv_w": nrm(ks[8], (N_CONV_LAYERS, CONV_WIDTH, D), CONV_WIDTH ** -0.5),
        "conv_w_out": nrm(ks[9], (N_CONV_LAYERS, D, D), D ** -0.5),
        "attn_w_qkv": nrm(ks[10], (N_ATTN_LAYERS, D, 3 * D), D ** -0.5),
        "attn_w_o": nrm(ks[11], (N_ATTN_LAYERS, D, D), D ** -0.5),
        "attn_q_gain": 1.0 + nrm(ks[12], (N_ATTN_LAYERS, HEAD_DIM), 0.02),
        "attn_k_gain": 1.0 + nrm(ks[13], (N_ATTN_LAYERS, HEAD_DIM), 0.02),
        "attn_lambdas": nrm(ks[14], (N_ATTN_LAYERS, 4, HEAD_DIM), 0.1),
        "attn_subln_gain": 1.0 + nrm(ks[15], (N_ATTN_LAYERS, V_HEAD_DIM), 0.02),
        "ffn_w_gu": nrm(ks[16], (N_CONV_LAYERS, D, 2 * D_FF), D ** -0.5),
        "ffn_w_down": nrm(ks[17], (N_CONV_LAYERS, D_FF, D), D_FF ** -0.5),
        "moe_router": nrm(ks[18], (N_ATTN_LAYERS, D, N_EXPERTS), D ** -0.5),
        "moe_w_gu": nrm(ks[19], (N_ATTN_LAYERS, N_EXPERTS, D, 2 * D_FF_EXPERT), D ** -0.5),
        "moe_w_down": nrm(ks[20], (N_ATTN_LAYERS, N_EXPERTS, D_FF_EXPERT, D), D_FF_EXPERT ** -0.5),
    }


def reference(x, c, ctx, c_ctx, w_mod, b_mod, norm_gain, conv_w_in, conv_w, conv_w_out,
              attn_w_qkv, attn_w_o, attn_q_gain, attn_k_gain, attn_lambdas, attn_subln_gain,
              ffn_w_gu, ffn_w_down, moe_router, moe_w_gu, moe_w_down):
    S = x.shape[1]
    cos, sin = axial_rope_tables(S)
    h_ctx = ctx
    silu_c = jax.nn.silu(c)
    silu_cc = jax.nn.silu(c_ctx)
    for i in range(DEPTH):
        j = i // N_MIXERS
        keep_ctx = i + 1 < DEPTH
        mod = jnp.split((silu_c @ w_mod[i] + b_mod[i])[:, None, :], N_MOD, axis=-1)
        mod_c = jnp.split(silu_cc @ w_mod[i] + b_mod[i], N_MOD, axis=-1)

        u = modulate(rmsnorm(x, norm_gain[i, 0]), mod[0], mod[1])
        uc = modulate(rmsnorm(h_ctx, norm_gain[i, 0]), mod_c[0], mod_c[1])
        if i % N_MIXERS == 0:
            y = short_conv_mixer(u, conv_w_in[j], conv_w[j], conv_w_out[j])
            yc = short_conv_mixer(uc, conv_w_in[j], conv_w[j], conv_w_out[j]) if keep_ctx else None
        else:
            lambda_init = 0.8 - 0.6 * math.exp(-0.3 * i)
            y, yc = diff_attention(u, uc, attn_w_qkv[j], attn_w_o[j], attn_q_gain[j], attn_k_gain[j],
                                   attn_lambdas[j], attn_subln_gain[j], lambda_init, cos, sin, keep_ctx)
        x = x + mod[2] * y
        if keep_ctx:
            h_ctx = h_ctx + mod_c[2] * yc

        u = modulate(rmsnorm(x, norm_gain[i, 1]), mod[3], mod[4])
        if i % 2 == 0:
            x = x + mod[5] * swiglu(u, ffn_w_gu[j], ffn_w_down[j])
        else:
            x = x + mod[5] * moe_swiglu(u, moe_router[j], moe_w_gu[j], moe_w_down[j])
        if keep_ctx:
            uc = modulate(rmsnorm(h_ctx, norm_gain[i, 1]), mod_c[3], mod_c[4])
            if i % 2 == 0:
                h_ctx = h_ctx + mod_c[5] * swiglu(uc, ffn_w_gu[j], ffn_w_down[j])
            else:
                h_ctx = h_ctx + mod_c[5] * moe_swiglu(uc, moe_router[j], moe_w_gu[j], moe_w_down[j])
    return x
```

```python
import functools
import math

import jax
import jax.numpy as jnp
from jax import lax
from jax.experimental import pallas as pl
from jax.experimental.pallas import tpu as pltpu

BF = jnp.bfloat16
F32 = jnp.float32

GRID_W = 64
ROPE_BASE = 10000.0
NORM_EPS = 1e-6
TOP_K = 2
N_MOD = 6

LANES = 128
BF16_SUBLANES = 16
VMEM_LIMIT_BYTES = 56 * 1024 * 1024


def _pick(n, target, align):
    best = None
    d = align
    while d <= min(n, target):
        if n % d == 0:
            best = d
        d += align
    return best if best is not None else n


def _round_up(n, m):
    return (n + m - 1) // m * m


def _params(*sem):
    return pltpu.CompilerParams(dimension_semantics=sem, vmem_limit_bytes=VMEM_LIMIT_BYTES)


def _sigmoid(x):
    return 1.0 / (1.0 + jnp.exp(-x))


def _mod_kernel(c_ref, w_ref, b_ref, o_ref):
    c = c_ref[...]
    s = (c * _sigmoid(c)).astype(BF)
    o_ref[0] = jnp.dot(s, w_ref[0].astype(BF), preferred_element_type=F32) + b_ref[0]


def _mod_vectors(cc, w_mod, b_mod):
    L, D, N = w_mod.shape
    tn = _pick(N, 512, LANES)
    return pl.pallas_call(
        _mod_kernel,
        grid=(L, N // tn),
        in_specs=[pl.BlockSpec((BF16_SUBLANES, D), lambda l, j: (0, 0)),
                  pl.BlockSpec((1, D, tn), lambda l, j: (l, 0, j)),
                  pl.BlockSpec((1, 1, tn), lambda l, j: (l, 0, j))],
        out_specs=pl.BlockSpec((1, BF16_SUBLANES, tn), lambda l, j: (l, 0, j)),
        out_shape=jax.ShapeDtypeStruct((L, BF16_SUBLANES, N), F32),
        compiler_params=_params("arbitrary", "arbitrary"),
        name="mod_vectors",
    )(cc, w_mod, b_mod.reshape(L, 1, N))


def _modnorm_rows(x, gain, shift, scale):
    ms = jnp.mean(x * x, axis=-1, keepdims=True)
    xn = x * lax.rsqrt(ms + NORM_EPS)
    return (xn * gain) * (1.0 + scale) + shift


def _modnorm_kernel(x_ref, g_ref, sh_ref, sc_ref, o_ref):
    o_ref[...] = _modnorm_rows(x_ref[...], g_ref[...], sh_ref[0], sc_ref[0]).astype(BF)


def _modnorm(x, gain, shift2, scale2, n_lat, n_rows):
    D = x.shape[1]
    tr = _pick(math.gcd(n_lat, n_rows), 256, BF16_SUBLANES)
    n_lat_tiles = n_lat // tr
    grp = lambda i: (jnp.where(i < n_lat_tiles, 0, 1), 0, 0)
    return pl.pallas_call(
        _modnorm_kernel,
        grid=(n_rows // tr,),
        in_specs=[pl.BlockSpec((tr, D), lambda i: (i, 0)),
                  pl.BlockSpec((1, D), lambda i: (0, 0)),
                  pl.BlockSpec((1, 1, D), grp),
                  pl.BlockSpec((1, 1, D), grp)],
        out_specs=pl.BlockSpec((tr, D), lambda i: (i, 0)),
        out_shape=jax.ShapeDtypeStruct((n_rows, D), BF),
        compiler_params=_params("arbitrary"),
        name="modnorm",
    )(x, gain.reshape(1, D), shift2.reshape(2, 1, D), scale2.reshape(2, 1, D))


def _conv_in_kernel(a_ref, wb_ref, wc_ref, wv_ref, b_ref, cv_ref):
    a = a_ref[...]
    b_ref[...] = jnp.dot(a, wb_ref[...], preferred_element_type=F32).astype(BF)
    c = jnp.dot(a, wc_ref[...], preferred_element_type=F32)
    v = jnp.dot(a, wv_ref[...], preferred_element_type=F32)
    cv_ref[...] = (c * v).astype(BF)


def _conv_in(u, w_in):
    R, D = u.shape
    tm = _pick(R, 1100, BF16_SUBLANES)
    tn = _pick(D, 512, LANES)
    nj = D // tn
    wspec = lambda k: pl.BlockSpec((D, tn), lambda j, i: (0, j + k * nj))
    return pl.pallas_call(
        _conv_in_kernel,
        grid=(nj, R // tm),
        in_specs=[pl.BlockSpec((tm, D), lambda j, i: (i, 0)), wspec(0), wspec(1), wspec(2)],
        out_specs=[pl.BlockSpec((tm, tn), lambda j, i: (i, j)),
                   pl.BlockSpec((tm, tn), lambda j, i: (i, j))],
        out_shape=[jax.ShapeDtypeStruct((R, D), BF), jax.ShapeDtypeStruct((R, D), BF)],
        compiler_params=_params("arbitrary", "arbitrary"),
        name="conv_in",
    )(u, w_in, w_in, w_in)


def _conv_mix_kernel(n_lat_tiles, b_ref, cv_ref, pv_ref, nx_ref, w_ref, o_ref):
    i = pl.program_id(0)
    tr = cv_ref.shape[0]
    cv = cv_ref[...].astype(F32)
    top_zero = (i == 0) | (i == n_lat_tiles)
    bot_zero = (i == n_lat_tiles - 1) | (i == pl.num_programs(0) - 1)
    hp = pv_ref[BF16_SUBLANES - 1:BF16_SUBLANES, :].astype(F32) * jnp.where(top_zero, 0.0, 1.0)
    hn = nx_ref[0:1, :].astype(F32) * jnp.where(bot_zero, 0.0, 1.0)
    rows = lax.broadcasted_iota(jnp.int32, cv.shape, 0)
    prev = jnp.where(rows == 0, hp, pltpu.roll(cv, 1, 0))
    nxt = jnp.where(rows == tr - 1, hn, pltpu.roll(cv, tr - 1, 0))
    w = w_ref[...]
    z = prev * w[0:1] + cv * w[1:2] + nxt * w[2:3]
    o_ref[...] = (b_ref[...].astype(F32) * z).astype(BF)


def _conv_mix(b, cv, conv_w, n_lat):
    R, D = cv.shape
    tr = _pick(math.gcd(n_lat, R), 256, BF16_SUBLANES)
    tc = _pick(D, 1024, LANES)
    hb = tr // BF16_SUBLANES
    last = R // BF16_SUBLANES - 1
    return pl.pallas_call(
        functools.partial(_conv_mix_kernel, n_lat // tr),
        grid=(R // tr, D // tc),
        in_specs=[pl.BlockSpec((tr, tc), lambda i, j: (i, j)),
                  pl.BlockSpec((tr, tc), lambda i, j: (i, j)),
                  pl.BlockSpec((BF16_SUBLANES, tc), lambda i, j: (jnp.maximum(i * hb - 1, 0), j)),
                  pl.BlockSpec((BF16_SUBLANES, tc), lambda i, j: (jnp.minimum((i + 1) * hb, last), j)),
                  pl.BlockSpec((3, tc), lambda i, j: (0, j))],
        out_specs=pl.BlockSpec((tr, tc), lambda i, j: (i, j)),
        out_shape=jax.ShapeDtypeStruct((R, D), BF),
        compiler_params=_params("arbitrary", "arbitrary"),
        name="conv_mix",
    )(b, cv, cv, cv, conv_w)


def _row_gate(g_ref, row0, shape, n_lat):
    rows = row0 + lax.broadcasted_iota(jnp.int32, shape, 0)
    return jnp.where(rows < n_lat, g_ref[0:1, :], g_ref[1:2, :])


def _proj_resid_kernel(n_lat, a_ref, w_ref, r_ref, g_ref, o_ref):
    tm = o_ref.shape[0]
    acc = jnp.dot(a_ref[...], w_ref[...], preferred_element_type=F32)
    g = _row_gate(g_ref, pl.program_id(1) * tm, acc.shape, n_lat)
    o_ref[...] = r_ref[...] + g * acc


def _proj_resid(a, w, resid, gate2, n_lat):
    M, K = a.shape
    N = w.shape[1]
    tm = _pick(M, 1100, BF16_SUBLANES)
    tn = _pick(N, 512, LANES)
    return pl.pallas_call(
        functools.partial(_proj_resid_kernel, n_lat),
        grid=(N // tn, M // tm),
        in_specs=[pl.BlockSpec((tm, K), lambda j, i: (i, 0)),
                  pl.BlockSpec((K, tn), lambda j, i: (0, j)),
                  pl.BlockSpec((tm, tn), lambda j, i: (i, j)),
                  pl.BlockSpec((2, tn), lambda j, i: (0, j))],
        out_specs=pl.BlockSpec((tm, tn), lambda j, i: (i, j)),
        out_shape=jax.ShapeDtypeStruct((M, N), F32),
        compiler_params=_params("arbitrary", "arbitrary"),
        name="proj_resid",
    )(a, w, resid, gate2)


def _gu_kernel(a_ref, wg_ref, wu_ref, o_ref):
    a = a_ref[...]
    g = jnp.dot(a, wg_ref[...], preferred_element_type=F32)
    u = jnp.dot(a, wu_ref[...], preferred_element_type=F32)
    o_ref[...] = ((g * _sigmoid(g)) * u).astype(BF)


def _ffn_gu(u, w_gu, f_pad):
    R, D = u.shape
    tm = _pick(R, 1100, BF16_SUBLANES)
    tn = _pick(f_pad, 512, LANES)
    nj = f_pad // tn
    return pl.pallas_call(
        _gu_kernel,
        grid=(R // tm, nj),
        in_specs=[pl.BlockSpec((tm, D), lambda i, j: (i, 0)),
                  pl.BlockSpec((D, tn), lambda i, j: (0, j)),
                  pl.BlockSpec((D, tn), lambda i, j: (0, j + nj))],
        out_specs=pl.BlockSpec((tm, tn), lambda i, j: (i, j)),
        out_shape=jax.ShapeDtypeStruct((R, f_pad), BF),
        compiler_params=_params("arbitrary", "arbitrary"),
        name="ffn_gu",
    )(u, w_gu, w_gu)


def _down_kernel(n_lat, a_ref, w_ref, r_ref, g_ref, o_ref, acc_ref):
    k = pl.program_id(2)

    @pl.when(k == 0)
    def _():
        acc_ref[...] = jnp.zeros_like(acc_ref)

    acc_ref[...] += jnp.dot(a_ref[...], w_ref[...], preferred_element_type=F32)

    @pl.when(k == pl.num_programs(2) - 1)
    def _():
        acc = acc_ref[...]
        g = _row_gate(g_ref, pl.program_id(0) * acc.shape[0], acc.shape, n_lat)
        o_ref[...] = r_ref[...] + g * acc


def _ffn_down(h, w_down, resid, gate2, n_lat):
    R, Fp = h.shape
    D = w_down.shape[1]
    tm = _pick(R, 1100, BF16_SUBLANES)
    tn = _pick(D, 1024, LANES)
    tk = _pick(Fp, 2816, LANES)
    return pl.pallas_call(
        functools.partial(_down_kernel, n_lat),
        grid=(R // tm, D // tn, Fp // tk),
        in_specs=[pl.BlockSpec((tm, tk), lambda i, j, k: (i, k)),
                  pl.BlockSpec((tk, tn), lambda i, j, k: (k, j)),
                  pl.BlockSpec((tm, tn), lambda i, j, k: (i, j)),
                  pl.BlockSpec((2, tn), lambda i, j, k: (0, j))],
        out_specs=pl.BlockSpec((tm, tn), lambda i, j, k: (i, j)),
        out_shape=jax.ShapeDtypeStruct((R, D), F32),
        scratch_shapes=[pltpu.VMEM((tm, tn), F32)],
        compiler_params=_params("arbitrary", "arbitrary", "arbitrary"),
        name="ffn_down",
    )(h, w_down, resid, gate2)


def _qkv_kernel(n_qk_tiles, q_scale, a_ref, w_ref, gain_ref, cos_ref, sa_ref, sb_ref, o_ref):
    j = pl.program_id(1)
    acc = jnp.dot(a_ref[...], w_ref[...], preferred_element_type=F32)

    @pl.when(j >= 2 * n_qk_tiles)
    def _():
        o_ref[...] = acc.astype(BF)

    @pl.when(j < 2 * n_qk_tiles)
    def _():
        gain = gain_ref[0]
        cos, sa, sb = cos_ref[...], sa_ref[...], sb_ref[...]
        post = jnp.where(j < n_qk_tiles, q_scale, 1.0)
        half = LANES // 4
        for g in range(acc.shape[1] // LANES):
            t = acc[:, g * LANES:(g + 1) * LANES]
            t = t * lax.rsqrt(jnp.mean(t * t, axis=-1, keepdims=True) + NORM_EPS)
            t = t * gain
            t = t * cos + pltpu.roll(t, LANES - half, 1) * sa + pltpu.roll(t, half, 1) * sb
            o_ref[:, g * LANES:(g + 1) * LANES] = (t * post).astype(BF)


def _qkv(u, w_qkv, qk_gain2, cos, sa, sb, q_scale):
    R, D = u.shape
    tm = _pick(R, 1100, BF16_SUBLANES)
    tn = _pick(D, 512, LANES)
    nq = D // tn
    tab = pl.BlockSpec((tm, LANES), lambda i, j: (i, 0))
    return pl.pallas_call(
        functools.partial(_qkv_kernel, nq, q_scale),
        grid=(R // tm, 3 * nq),
        in_specs=[pl.BlockSpec((tm, D), lambda i, j: (i, 0)),
                  pl.BlockSpec((D, tn), lambda i, j: (0, j)),
                  pl.BlockSpec((1, 1, LANES), lambda i, j: (jnp.where(j < nq, 0, 1), 0, 0)),
                  tab, tab, tab],
        out_specs=pl.BlockSpec((tm, tn), lambda i, j: (i, j)),
        out_shape=jax.ShapeDtypeStruct((R, 3 * D), BF),
        compiler_params=_params("arbitrary", "arbitrary"),
        name="qkv_proj",
    )(u, w_qkv, qk_gain2, cos, sa, sb)


def _attn_kernel(n_lat, tk, lambda_init, q_ref, k_ref, v_ref, lam_ref, sg_ref, o_ref,
                 m_ref, l_ref, acc_ref):
    hd = q_ref.shape[1] // 2
    n_kv = k_ref.shape[0]
    m_ref[...] = jnp.full_like(m_ref, -jnp.inf)
    l_ref[...] = jnp.zeros_like(l_ref)
    acc_ref[...] = jnp.zeros_like(acc_ref)

    def update(kc, vc):
        for m in range(2):
            q = q_ref[:, m * hd:(m + 1) * hd]
            s = lax.dot_general(q, kc[:, m * hd:(m + 1) * hd], (((1,), (1,)), ((), ())),
                                preferred_element_type=F32)
            m_old = m_ref[m]
            m_new = jnp.maximum(m_old, jnp.max(s, axis=-1, keepdims=True))
            alpha = jnp.exp(m_old - m_new)
            p = jnp.exp(s - m_new)
            l_ref[m] = alpha * l_ref[m] + jnp.sum(p, axis=-1, keepdims=True)
            acc_ref[m] = alpha * acc_ref[m] + jnp.dot(p.astype(BF), vc, preferred_element_type=F32)
            m_ref[m] = m_new

    def body(c, carry):
        start = pl.multiple_of(c * tk, tk)
        update(k_ref[pl.ds(start, tk), :], v_ref[pl.ds(start, tk), :])
        return carry

    lax.fori_loop(0, n_lat // tk, body, 0)
    if n_kv > n_lat:
        update(k_ref[n_lat:n_kv, :], v_ref[n_lat:n_kv, :])

    lf = lam_ref[...]
    lam = (jnp.exp(jnp.sum(lf[0:1] * lf[1:2], axis=-1, keepdims=True))
           - jnp.exp(jnp.sum(lf[2:3] * lf[3:4], axis=-1, keepdims=True)) + lambda_init)
    o = acc_ref[0] / l_ref[0] - lam * (acc_ref[1] / l_ref[1])
    o = o * lax.rsqrt(jnp.mean(o * o, axis=-1, keepdims=True) + NORM_EPS)
    o_ref[...] = ((o * sg_ref[...]) * (1.0 - lambda_init)).astype(BF)


def _diff_attention(qkv, lambdas, subln_gain, n_lat, n_heads, lambda_init):
    R = qkv.shape[0]
    D = qkv.shape[1] // 3
    vd = D // n_heads
    tq = _pick(n_lat, 1024, BF16_SUBLANES)
    tk = _pick(n_lat, 1024, LANES)
    return pl.pallas_call(
        functools.partial(_attn_kernel, n_lat, tk, lambda_init),
        grid=(n_heads, n_lat // tq),
        in_specs=[pl.BlockSpec((tq, vd), lambda h, i: (i, h)),
                  pl.BlockSpec((R, vd), lambda h, i: (0, n_heads + h)),
                  pl.BlockSpec((R, vd), lambda h, i: (0, 2 * n_heads + h)),
                  pl.BlockSpec(lambdas.shape, lambda h, i: (0, 0)),
                  pl.BlockSpec((1, vd), lambda h, i: (0, 0))],
        out_specs=pl.BlockSpec((tq, vd), lambda h, i: (i, h)),
        out_shape=jax.ShapeDtypeStruct((n_lat, D), BF),
        scratch_shapes=[pltpu.VMEM((2, tq, 1), F32), pltpu.VMEM((2, tq, 1), F32),
                        pltpu.VMEM((2, tq, vd), F32)],
        compiler_params=_params("arbitrary", "arbitrary"),
        name="diff_attention",
    )(qkv, qkv, qkv, lambdas, subln_gain.reshape(1, vd))


def _router_kernel(n_experts, x_ref, g_ref, sh_ref, sc_ref, w_ref, comb_ref, sel_ref):
    u = _modnorm_rows(x_ref[...], g_ref[...], sh_ref[...], sc_ref[...]).astype(BF)
    logits = jnp.dot(u, w_ref[...], preferred_element_type=F32)
    lane = lax.broadcasted_iota(jnp.int32, logits.shape, 1)
    neg = jnp.float32(-jnp.inf)
    l1 = jnp.where(lane < n_experts, logits, neg)
    m1 = jnp.max(l1, axis=-1, keepdims=True)
    i1 = jnp.min(jnp.where(l1 == m1, lane, LANES), axis=-1, keepdims=True)
    sel1 = lane == i1
    l2 = jnp.where(sel1, neg, l1)
    m2 = jnp.max(l2, axis=-1, keepdims=True)
    i2 = jnp.min(jnp.where(l2 == m2, lane, LANES), axis=-1, keepdims=True)
    sel2 = lane == i2
    e2 = jnp.exp(m2 - m1)
    den = 1.0 + e2
    comb_ref[...] = jnp.where(sel1, 1.0 / den, 0.0) + jnp.where(sel2, e2 / den, 0.0)
    sel_ref[...] = (sel1 | sel2).astype(jnp.int32)


def _router(x, gain, shift, scale, w_router_pad, n_experts):
    S, D = x.shape
    tm = _pick(S, 512, 8)
    vec = pl.BlockSpec((1, D), lambda i: (0, 0))
    out = pl.BlockSpec((tm, LANES), lambda i: (i, 0))
    return pl.pallas_call(
        functools.partial(_router_kernel, n_experts),
        grid=(S // tm,),
        in_specs=[pl.BlockSpec((tm, D), lambda i: (i, 0)), vec, vec, vec,
                  pl.BlockSpec((D, LANES), lambda i: (0, 0))],
        out_specs=[out, out],
        out_shape=[jax.ShapeDtypeStruct((S, LANES), F32), jax.ShapeDtypeStruct((S, LANES), jnp.int32)],
        compiler_params=_params("arbitrary"),
        name="moe_router",
    )(x, gain.reshape(1, D), shift.reshape(1, D), scale.reshape(1, D), w_router_pad)


def _row_copy(src_hbm, row, dst, r, sem):
    return pltpu.make_async_copy(src_hbm.at[pl.ds(row, 1)], dst.at[pl.ds(r, 1)], sem)


def _gather_norm_kernel(tok_ref, nrows_ref, x_hbm, g_ref, sh_ref, sc_ref, o_ref, buf, sem):
    tg = buf.shape[0]
    base = pl.program_id(0) * tg

    @pl.when(base < nrows_ref[0])
    def _():
        def issue(r, c):
            _row_copy(x_hbm, tok_ref[base + r], buf, r, sem).start()
            return c

        def wait(r, c):
            _row_copy(x_hbm, 0, buf, r, sem).wait()
            return c

        lax.fori_loop(0, tg, issue, 0)
        lax.fori_loop(0, tg, wait, 0)
        o_ref[...] = _modnorm_rows(buf[...], g_ref[...], sh_ref[...], sc_ref[...]).astype(BF)

    @pl.when(base >= nrows_ref[0])
    def _():
        o_ref[...] = jnp.zeros_like(o_ref)


def _gather_norm(x, row_token, n_rows_used, gain, shift, scale):
    D = x.shape[1]
    Rs = row_token.shape[0]
    tg = _pick(Rs, 256, BF16_SUBLANES)
    vec = pl.BlockSpec((1, D), lambda t, tok, n: (0, 0))
    return pl.pallas_call(
        _gather_norm_kernel,
        grid_spec=pltpu.PrefetchScalarGridSpec(
            num_scalar_prefetch=2,
            grid=(Rs // tg,),
            in_specs=[pl.BlockSpec(memory_space=pl.ANY), vec, vec, vec],
            out_specs=pl.BlockSpec((tg, D), lambda t, tok, n: (t, 0)),
            scratch_shapes=[pltpu.VMEM((tg, D), F32), pltpu.SemaphoreType.DMA(())]),
        out_shape=jax.ShapeDtypeStruct((Rs, D), BF),
        compiler_params=_params("arbitrary"),
        name="moe_gather_norm",
    )(row_token, n_rows_used, x, gain.reshape(1, D), shift.reshape(1, D), scale.reshape(1, D))


def _moe_gu_kernel(te_ref, nt_ref, a_ref, wg_ref, wu_ref, o_ref):
    @pl.when(pl.program_id(1) < nt_ref[0])
    def _():
        a = a_ref[...]
        g = jnp.dot(a, wg_ref[0], preferred_element_type=F32)
        u = jnp.dot(a, wu_ref[0], preferred_element_type=F32)
        o_ref[...] = ((g * _sigmoid(g)) * u).astype(BF)

    @pl.when(pl.program_id(1) >= nt_ref[0])
    def _():
        o_ref[...] = jnp.zeros_like(o_ref)


def _moe_gu(a, w_gu, tile_expert, n_tiles, tm):
    Rs, D = a.shape
    F = w_gu.shape[2] // 2
    tn = _pick(F, 512, LANES)
    nj = F // tn
    tile = lambda t, nt: jnp.minimum(t, nt[0] - 1)
    return pl.pallas_call(
        _moe_gu_kernel,
        grid_spec=pltpu.PrefetchScalarGridSpec(
            num_scalar_prefetch=2,
            grid=(nj, Rs // tm),
            in_specs=[pl.BlockSpec((tm, D), lambda j, t, te, nt: (tile(t, nt), 0)),
                      pl.BlockSpec((1, D, tn), lambda j, t, te, nt: (te[tile(t, nt)], 0, j)),
                      pl.BlockSpec((1, D, tn), lambda j, t, te, nt: (te[tile(t, nt)], 0, j + nj))],
            out_specs=pl.BlockSpec((tm, tn), lambda j, t, te, nt: (t, j))),
        out_shape=jax.ShapeDtypeStruct((Rs, F), BF),
        compiler_params=_params("arbitrary", "arbitrary"),
        name="moe_gu",
    )(tile_expert, n_tiles, a, w_gu, w_gu)


def _moe_down_kernel(te_ref, nt_ref, a_ref, w_ref, rw_ref, o_ref):
    @pl.when(pl.program_id(1) < nt_ref[0])
    def _():
        acc = jnp.dot(a_ref[...], w_ref[0], preferred_element_type=F32)
        o_ref[...] = rw_ref[...] * acc

    @pl.when(pl.program_id(1) >= nt_ref[0])
    def _():
        o_ref[...] = jnp.zeros_like(o_ref)


def _moe_down(h, w_down, row_w, tile_expert, n_tiles, tm):
    Rs, F = h.shape
    D = w_down.shape[2]
    tn = _pick(D, 512, LANES)
    tile = lambda t, nt: jnp.minimum(t, nt[0] - 1)
    return pl.pallas_call(
        _moe_down_kernel,
        grid_spec=pltpu.PrefetchScalarGridSpec(
            num_scalar_prefetch=2,
            grid=(D // tn, Rs // tm),
            in_specs=[pl.BlockSpec((tm, F), lambda j, t, te, nt: (tile(t, nt), 0)),
                      pl.BlockSpec((1, F, tn), lambda j, t, te, nt: (te[tile(t, nt)], 0, j)),
                      pl.BlockSpec((tm, 1), lambda j, t, te, nt: (tile(t, nt), 0))],
            out_specs=pl.BlockSpec((tm, tn), lambda j, t, te, nt: (t, j))),
        out_shape=jax.ShapeDtypeStruct((Rs, D), F32),
        compiler_params=_params("arbitrary", "arbitrary"),
        name="moe_down",
    )(tile_expert, n_tiles, h, w_down, row_w)


def _combine_kernel(pa_ref, pb_ref, y_hbm, x_ref, g_ref, o_ref, ya, yb, sem):
    tt = ya.shape[0]
    base = pl.program_id(0) * tt

    def issue(r, c):
        _row_copy(y_hbm, pa_ref[base + r], ya, r, sem.at[0]).start()
        _row_copy(y_hbm, pb_ref[base + r], yb, r, sem.at[1]).start()
        return c

    def wait(r, c):
        _row_copy(y_hbm, 0, ya, r, sem.at[0]).wait()
        _row_copy(y_hbm, 0, yb, r, sem.at[1]).wait()
        return c

    lax.fori_loop(0, tt, issue, 0)
    lax.fori_loop(0, tt, wait, 0)
    o_ref[...] = x_ref[...] + g_ref[...] * (ya[...] + yb[...])


def _moe_combine(x, y, pos_a, pos_b, gate):
    S, D = x.shape
    tt = _pick(S, 256, 8)
    return pl.pallas_call(
        _combine_kernel,
        grid_spec=pltpu.PrefetchScalarGridSpec(
            num_scalar_prefetch=2,
            grid=(S // tt,),
            in_specs=[pl.BlockSpec(memory_space=pl.ANY),
                      pl.BlockSpec((tt, D), lambda t, pa, pb: (t, 0)),
                      pl.BlockSpec((1, D), lambda t, pa, pb: (0, 0))],
            out_specs=pl.BlockSpec((tt, D), lambda t, pa, pb: (t, 0)),
            scratch_shapes=[pltpu.VMEM((tt, D), F32), pltpu.VMEM((tt, D), F32),
                            pltpu.SemaphoreType.DMA((2,))]),
        out_shape=jax.ShapeDtypeStruct((S, D), F32),
        compiler_params=_params("arbitrary"),
        name="moe_combine",
    )(pos_a, pos_b, y, x, gate.reshape(1, D))


def _routing_tables(sel, comb, tm):
    S, E = sel.shape
    Rs = TOP_K * S + E * tm
    seli = sel.astype(jnp.int32)
    cnt = jnp.sum(seli, axis=0)
    padded = (cnt + tm - 1) // tm * tm
    gend = jnp.cumsum(padded)
    gstart = gend - padded
    pos = gstart[None, :] + jnp.cumsum(seli, axis=0) - seli
    dest = jnp.where(sel, pos, Rs).reshape(-1)
    tok = jnp.broadcast_to(jnp.arange(S, dtype=jnp.int32)[:, None], (S, E)).reshape(-1)
    row_token = jnp.zeros((Rs,), jnp.int32).at[dest].set(tok, mode="drop")
    row_w = jnp.zeros((Rs,), F32).at[dest].set(comb.reshape(-1), mode="drop")
    two = jnp.sort(jnp.where(sel, pos, Rs), axis=1)[:, :TOP_K].astype(jnp.int32)
    tile_start = jnp.arange(Rs // tm, dtype=jnp.int32) * tm
    tile_expert = jnp.minimum(jnp.searchsorted(gend, tile_start, side="right"), E - 1).astype(jnp.int32)
    n_rows = gend[-1].astype(jnp.int32).reshape(1)
    return row_token, row_w.reshape(Rs, 1), two[:, 0], two[:, 1], tile_expert, n_rows, (n_rows // tm)


def _rope_tables(n_lat, n_rows, head_dim):
    n_freq = head_dim // 4
    s = jnp.arange(n_lat, dtype=jnp.int32)
    inv = ROPE_BASE ** (-jnp.arange(n_freq, dtype=F32) / n_freq)
    ang = jnp.stack([s // GRID_W, s % GRID_W], axis=-1).astype(F32)[..., None] * inv
    cos, sin = jnp.cos(ang), jnp.sin(ang)
    zero = jnp.zeros_like(sin)
    cos_t = jnp.stack([cos, cos], axis=2).reshape(n_lat, head_dim)
    sa_t = jnp.stack([-sin, zero], axis=2).reshape(n_lat, head_dim)
    sb_t = jnp.stack([zero, sin], axis=2).reshape(n_lat, head_dim)
    pad = ((0, n_rows - n_lat), (0, 0))
    return jnp.pad(cos_t, pad, constant_values=1.0), jnp.pad(sa_t, pad), jnp.pad(sb_t, pad)


def kernel(x, c, ctx, c_ctx, w_mod, b_mod, norm_gain, conv_w_in, conv_w, conv_w_out, attn_w_qkv, attn_w_o,
           attn_q_gain, attn_k_gain, attn_lambdas, attn_subln_gain, ffn_w_gu, ffn_w_down, moe_router,
           moe_w_gu, moe_w_down):
    assert x.shape[0] == 1 and w_mod.shape[0] == 2
    S, D = x.shape[1], x.shape[2]
    C = ctx.shape[1]
    R = S + C
    head_dim = attn_q_gain.shape[-1]
    n_heads = D // (2 * head_dim)
    E = moe_router.shape[-1]
    F = ffn_w_down.shape[1]
    f_pad = _round_up(F, 512)

    cc = jnp.zeros((BF16_SUBLANES, D), F32).at[0].set(c[0]).at[1].set(c_ctx)
    mod = _mod_vectors(cc, w_mod, b_mod)[:, :2, :].reshape(2, 2, N_MOD, D)

    xa = jnp.concatenate([x[0], ctx[0]], axis=0)

    m0 = mod[0]
    u = _modnorm(xa, norm_gain[0, 0], m0[:, 0], m0[:, 1], S, R)
    b, cv = _conv_in(u, conv_w_in[0].astype(BF))
    bz = _conv_mix(b, cv, conv_w[0], S)
    xa = _proj_resid(bz, conv_w_out[0].astype(BF), xa, m0[:, 2], S)
    u = _modnorm(xa, norm_gain[0, 1], m0[:, 3], m0[:, 4], S, R)
    w_gu = jnp.pad(ffn_w_gu[0].reshape(D, 2, F), ((0, 0), (0, 0), (0, f_pad - F))).astype(BF)
    w_dn = jnp.pad(ffn_w_down[0], ((0, f_pad - F), (0, 0))).astype(BF)
    h = _ffn_gu(u, w_gu.reshape(D, 2 * f_pad), f_pad)
    xa = _ffn_down(h, w_dn, xa, m0[:, 5], S)

    m1 = mod[1]
    lambda_init = 0.8 - 0.6 * math.exp(-0.3 * 1)
    u = _modnorm(xa, norm_gain[1, 0], m1[:, 0], m1[:, 1], S, R)
    cos, sa, sb = _rope_tables(S, R, head_dim)
    qk_gain = jnp.stack([attn_q_gain[0], attn_k_gain[0]]).reshape(2, 1, head_dim)
    qkv = _qkv(u, attn_w_qkv[0].astype(BF), qk_gain, cos, sa, sb, head_dim ** -0.5)
    o = _diff_attention(qkv, attn_lambdas[0], attn_subln_gain[0], S, n_heads, lambda_init)
    x1 = _proj_resid(o, attn_w_o[0].astype(BF), xa, m1[:, 2], S)

    tm = _pick(S, 512, BF16_SUBLANES)
    w_r = jnp.pad(moe_router[0], ((0, 0), (0, LANES - E))).astype(BF)
    comb, selm = _router(x1, norm_gain[1, 1], m1[0, 3], m1[0, 4], w_r, E)
    row_token, row_w, pos_a, pos_b, tile_expert, n_rows, n_tiles = _routing_tables(
        selm[:, :E] > 0, comb[:, :E], tm)
    us = _gather_norm(x1, row_token, n_rows, norm_gain[1, 1], m1[0, 3], m1[0, 4])
    hs = _moe_gu(us, moe_w_gu[0].astype(BF), tile_expert, n_tiles, tm)
    ys = _moe_down(hs, moe_w_down[0].astype(BF), row_w, tile_expert, n_tiles, tm)
    out = _moe_combine(x1, ys, pos_a, pos_b, m1[0, 5])
    return out[None]
```

```python
import functools
import math

import jax
import jax.numpy as jnp
from jax import lax
from jax.experimental import pallas as pl
from jax.experimental.pallas import tpu as pltpu

BF = jnp.bfloat16
F32 = jnp.float32

GRID_W = 64
ROPE_BASE = 10000.0
NORM_EPS = 1e-6
TOP_K = 2
N_MOD = 6

LANES = 128
BF16_SUBLANES = 16
VMEM_LIMIT_BYTES = 56 * 1024 * 1024


def _pick(n, target, align):
    best = None
    d = align
    while d <= min(n, target):
        if n % d == 0:
            best = d
        d += align
    return best if best is not None else n


def _round_up(n, m):
    return (n + m - 1) // m * m


def _params(*sem):
    return pltpu.CompilerParams(dimension_semantics=sem, vmem_limit_bytes=VMEM_LIMIT_BYTES)


def _sigmoid(x):
    return 1.0 / (1.0 + jnp.exp(-x))


def _mod_kernel(c_ref, w_ref, b_ref, o_ref):
    c = c_ref[...]
    s = (c * _sigmoid(c)).astype(BF)
    o_ref[0] = jnp.dot(s, w_ref[0].astype(BF), preferred_element_type=F32) + b_ref[0]


def _mod_vectors(cc, w_mod, b_mod):
    L, D, N = w_mod.shape
    tn = _pick(N, 512, LANES)
    return pl.pallas_call(
        _mod_kernel,
        grid=(L, N // tn),
        in_specs=[pl.BlockSpec((BF16_SUBLANES, D), lambda l, j: (0, 0)),
                  pl.BlockSpec((1, D, tn), lambda l, j: (l, 0, j)),
                  pl.BlockSpec((1, 1, tn), lambda l, j: (l, 0, j))],
        out_specs=pl.BlockSpec((1, BF16_SUBLANES, tn), lambda l, j: (l, 0, j)),
        out_shape=jax.ShapeDtypeStruct((L, BF16_SUBLANES, N), F32),
        compiler_params=_params("arbitrary", "arbitrary"),
        name="mod_vectors",
    )(cc, w_mod, b_mod.reshape(L, 1, N))


def _modnorm_rows(x, gain, shift, scale):
    ms = jnp.mean(x * x, axis=-1, keepdims=True)
    xn = x * lax.rsqrt(ms + NORM_EPS)
    return (xn * gain) * (1.0 + scale) + shift


def _modnorm_kernel(x_ref, g_ref, sh_ref, sc_ref, o_ref):
    o_ref[...] = _modnorm_rows(x_ref[...], g_ref[...], sh_ref[0], sc_ref[0]).astype(BF)


def _modnorm(x, gain, shift2, scale2, n_lat, n_rows):
    D = x.shape[1]
    tr = _pick(math.gcd(n_lat, n_rows), 256, BF16_SUBLANES)
    n_lat_tiles = n_lat // tr
    grp = lambda i: (jnp.where(i < n_lat_tiles, 0, 1), 0, 0)
    return pl.pallas_call(
        _modnorm_kernel,
        grid=(n_rows // tr,),
        in_specs=[pl.BlockSpec((tr, D), lambda i: (i, 0)),
                  pl.BlockSpec((1, D), lambda i: (0, 0)),
                  pl.BlockSpec((1, 1, D), grp),
                  pl.BlockSpec((1, 1, D), grp)],
        out_specs=pl.BlockSpec((tr, D), lambda i: (i, 0)),
        out_shape=jax.ShapeDtypeStruct((n_rows, D), BF),
        compiler_params=_params("arbitrary"),
        name="modnorm",
    )(x, gain.reshape(1, D), shift2.reshape(2, 1, D), scale2.reshape(2, 1, D))


def _conv_in_kernel(a_ref, wb_ref, wc_ref, wv_ref, b_ref, cv_ref, w_bf):
    @pl.when(pl.program_id(1) == 0)
    def _():
        w_bf[0] = wb_ref[0].astype(BF)
        w_bf[1] = wc_ref[0].astype(BF)
        w_bf[2] = wv_ref[0].astype(BF)

    a = a_ref[...]
    b_ref[...] = jnp.dot(a, w_bf[0], preferred_element_type=F32).astype(BF)
    c = jnp.dot(a, w_bf[1], preferred_element_type=F32)
    v = jnp.dot(a, w_bf[2], preferred_element_type=F32)
    cv_ref[...] = (c * v).astype(BF)


def _conv_in(u, w_in):
    R, D = u.shape
    tm = _pick(R, 600, BF16_SUBLANES)
    tn = _pick(D, 256, LANES)
    nj = D // tn
    wspec = lambda k: pl.BlockSpec((1, D, tn), lambda j, i: (0, 0, j + k * nj))
    return pl.pallas_call(
        _conv_in_kernel,
        grid=(nj, R // tm),
        in_specs=[pl.BlockSpec((tm, D), lambda j, i: (i, 0)), wspec(0), wspec(1), wspec(2)],
        out_specs=[pl.BlockSpec((tm, tn), lambda j, i: (i, j)),
                   pl.BlockSpec((tm, tn), lambda j, i: (i, j))],
        out_shape=[jax.ShapeDtypeStruct((R, D), BF), jax.ShapeDtypeStruct((R, D), BF)],
        scratch_shapes=[pltpu.VMEM((3, D, tn), BF)],
        compiler_params=_params("arbitrary", "arbitrary"),
        name="conv_in",
    )(u, w_in, w_in, w_in)


def _conv_mix_kernel(n_lat_tiles, b_ref, cv_ref, pv_ref, nx_ref, w_ref, o_ref):
    i = pl.program_id(0)
    tr = cv_ref.shape[0]
    cv = cv_ref[...].astype(F32)
    top_zero = (i == 0) | (i == n_lat_tiles)
    bot_zero = (i == n_lat_tiles - 1) | (i == pl.num_programs(0) - 1)
    hp = pv_ref[BF16_SUBLANES - 1:BF16_SUBLANES, :].astype(F32) * jnp.where(top_zero, 0.0, 1.0)
    hn = nx_ref[0:1, :].astype(F32) * jnp.where(bot_zero, 0.0, 1.0)
    rows = lax.broadcasted_iota(jnp.int32, cv.shape, 0)
    prev = jnp.where(rows == 0, hp, pltpu.roll(cv, 1, 0))
    nxt = jnp.where(rows == tr - 1, hn, pltpu.roll(cv, tr - 1, 0))
    w = w_ref[...]
    z = prev * w[0:1] + cv * w[1:2] + nxt * w[2:3]
    o_ref[...] = (b_ref[...].astype(F32) * z).astype(BF)


def _conv_mix(b, cv, conv_w, n_lat):
    R, D = cv.shape
    tr = _pick(math.gcd(n_lat, R), 256, BF16_SUBLANES)
    tc = _pick(D, 1024, LANES)
    hb = tr // BF16_SUBLANES
    last = R // BF16_SUBLANES - 1
    return pl.pallas_call(
        functools.partial(_conv_mix_kernel, n_lat // tr),
        grid=(R // tr, D // tc),
        in_specs=[pl.BlockSpec((tr, tc), lambda i, j: (i, j)),
                  pl.BlockSpec((tr, tc), lambda i, j: (i, j)),
                  pl.BlockSpec((BF16_SUBLANES, tc), lambda i, j: (jnp.maximum(i * hb - 1, 0), j)),
                  pl.BlockSpec((BF16_SUBLANES, tc), lambda i, j: (jnp.minimum((i + 1) * hb, last), j)),
                  pl.BlockSpec((3, tc), lambda i, j: (0, j))],
        out_specs=pl.BlockSpec((tr, tc), lambda i, j: (i, j)),
        out_shape=jax.ShapeDtypeStruct((R, D), BF),
        compiler_params=_params("arbitrary", "arbitrary"),
        name="conv_mix",
    )(b, cv, cv, cv, conv_w)


def _row_gate(g_ref, row0, shape, n_lat):
    rows = row0 + lax.broadcasted_iota(jnp.int32, shape, 0)
    return jnp.where(rows < n_lat, g_ref[0:1, :], g_ref[1:2, :])


def _proj_resid_kernel(n_lat, a_ref, w_ref, r_ref, g_ref, o_ref, w_bf):
    @pl.when(pl.program_id(1) == 0)
    def _():
        w_bf[...] = w_ref[0].astype(BF)

    tm = o_ref.shape[0]
    acc = jnp.dot(a_ref[...], w_bf[...], preferred_element_type=F32)
    g = _row_gate(g_ref, pl.program_id(1) * tm, acc.shape, n_lat)
    o_ref[...] = r_ref[...] + g * acc


def _proj_resid(a, w, resid, gate2, n_lat):
    M, K = a.shape
    N = w.shape[2]
    tm = _pick(M, 1100, BF16_SUBLANES)
    tn = _pick(N, 512, LANES)
    return pl.pallas_call(
        functools.partial(_proj_resid_kernel, n_lat),
        grid=(N // tn, M // tm),
        in_specs=[pl.BlockSpec((tm, K), lambda j, i: (i, 0)),
                  pl.BlockSpec((1, K, tn), lambda j, i: (0, 0, j)),
                  pl.BlockSpec((tm, tn), lambda j, i: (i, j)),
                  pl.BlockSpec((2, tn), lambda j, i: (0, j))],
        out_specs=pl.BlockSpec((tm, tn), lambda j, i: (i, j)),
        out_shape=jax.ShapeDtypeStruct((M, N), F32),
        scratch_shapes=[pltpu.VMEM((K, tn), BF)],
        compiler_params=_params("arbitrary", "arbitrary"),
        name="proj_resid",
    )(a, w, resid, gate2)


def _gu_kernel(a_ref, wg_ref, wu_ref, o_ref):
    a = a_ref[...]
    g = jnp.dot(a, wg_ref[...], preferred_element_type=F32)
    u = jnp.dot(a, wu_ref[...], preferred_element_type=F32)
    o_ref[...] = ((g * _sigmoid(g)) * u).astype(BF)


def _ffn_gu(u, w_gu, f_pad):
    R, D = u.shape
    tm = _pick(R, 1100, BF16_SUBLANES)
    tn = _pick(f_pad, 512, LANES)
    nj = f_pad // tn
    return pl.pallas_call(
        _gu_kernel,
        grid=(R // tm, nj),
        in_specs=[pl.BlockSpec((tm, D), lambda i, j: (i, 0)),
                  pl.BlockSpec((D, tn), lambda i, j: (0, j)),
                  pl.BlockSpec((D, tn), lambda i, j: (0, j + nj))],
        out_specs=pl.BlockSpec((tm, tn), lambda i, j: (i, j)),
        out_shape=jax.ShapeDtypeStruct((R, f_pad), BF),
        compiler_params=_params("arbitrary", "arbitrary"),
        name="ffn_gu",
    )(u, w_gu, w_gu)


def _down_kernel(n_lat, a_ref, w_ref, r_ref, g_ref, o_ref, acc_ref):
    k = pl.program_id(2)

    @pl.when(k == 0)
    def _():
        acc_ref[...] = jnp.zeros_like(acc_ref)

    acc_ref[...] += jnp.dot(a_ref[...], w_ref[...], preferred_element_type=F32)

    @pl.when(k == pl.num_programs(2) - 1)
    def _():
        acc = acc_ref[...]
        g = _row_gate(g_ref, pl.program_id(0) * acc.shape[0], acc.shape, n_lat)
        o_ref[...] = r_ref[...] + g * acc


def _ffn_down(h, w_down, resid, gate2, n_lat):
    R, Fp = h.shape
    D = w_down.shape[1]
    tm = _pick(R, 1100, BF16_SUBLANES)
    tn = _pick(D, 1024, LANES)
    tk = _pick(Fp, 2816, LANES)
    return pl.pallas_call(
        functools.partial(_down_kernel, n_lat),
        grid=(R // tm, D // tn, Fp // tk),
        in_specs=[pl.BlockSpec((tm, tk), lambda i, j, k: (i, k)),
                  pl.BlockSpec((tk, tn), lambda i, j, k: (k, j)),
                  pl.BlockSpec((tm, tn), lambda i, j, k: (i, j)),
                  pl.BlockSpec((2, tn), lambda i, j, k: (0, j))],
        out_specs=pl.BlockSpec((tm, tn), lambda i, j, k: (i, j)),
        out_shape=jax.ShapeDtypeStruct((R, D), F32),
        scratch_shapes=[pltpu.VMEM((tm, tn), F32)],
        compiler_params=_params("arbitrary", "arbitrary", "arbitrary"),
        name="ffn_down",
    )(h, w_down, resid, gate2)


def _qkv_kernel(n_qk_tiles, q_scale, a_ref, w_ref, gain_ref, cos_ref, sa_ref, sb_ref, o_ref, w_bf):
    j = pl.program_id(0)

    @pl.when(pl.program_id(1) == 0)
    def _():
        w_bf[...] = w_ref[0].astype(BF)

    acc = jnp.dot(a_ref[...], w_bf[...], preferred_element_type=F32)

    @pl.when(j >= 2 * n_qk_tiles)
    def _():
        o_ref[...] = acc.astype(BF)

    @pl.when(j < 2 * n_qk_tiles)
    def _():
        gain = gain_ref[0]
        cos, sa, sb = cos_ref[...], sa_ref[...], sb_ref[...]
        post = jnp.where(j < n_qk_tiles, q_scale, 1.0)
        half = LANES // 4
        for g in range(acc.shape[1] // LANES):
            t = acc[:, g * LANES:(g + 1) * LANES]
            t = t * lax.rsqrt(jnp.mean(t * t, axis=-1, keepdims=True) + NORM_EPS)
            t = t * gain
            t = t * cos + pltpu.roll(t, LANES - half, 1) * sa + pltpu.roll(t, half, 1) * sb
            o_ref[:, g * LANES:(g + 1) * LANES] = (t * post).astype(BF)


def _qkv(u, w_qkv, qk_gain2, cos, sa, sb, q_scale):
    R, D = u.shape
    tm = _pick(R, 1100, BF16_SUBLANES)
    tn = _pick(D, 512, LANES)
    nq = D // tn
    tab = pl.BlockSpec((tm, LANES), lambda j, i: (i, 0))
    return pl.pallas_call(
        functools.partial(_qkv_kernel, nq, q_scale),
        grid=(3 * nq, R // tm),
        in_specs=[pl.BlockSpec((tm, D), lambda j, i: (i, 0)),
                  pl.BlockSpec((1, D, tn), lambda j, i: (0, 0, j)),
                  pl.BlockSpec((1, 1, LANES), lambda j, i: (jnp.where(j < nq, 0, 1), 0, 0)),
                  tab, tab, tab],
        out_specs=pl.BlockSpec((tm, tn), lambda j, i: (i, j)),
        out_shape=jax.ShapeDtypeStruct((R, 3 * D), BF),
        scratch_shapes=[pltpu.VMEM((D, tn), BF)],
        compiler_params=_params("arbitrary", "arbitrary"),
        name="qkv_proj",
    )(u, w_qkv, qk_gain2, cos, sa, sb)


def _attn_kernel(n_lat, tk, lambda_init, q_ref, k_ref, v_ref, lam_ref, sg_ref, o_ref,
                 sa_ref, sb_ref, sc_ref, m_ref, l_ref, acc_ref):
    hd = q_ref.shape[1] // 2
    n_kv = k_ref.shape[0]
    n_chunks = n_lat // tk
    has_ctx = n_kv > n_lat
    m_ref[...] = jnp.full_like(m_ref, -jnp.inf)
    l_ref[...] = jnp.zeros_like(l_ref)
    acc_ref[...] = jnp.zeros_like(acc_ref)

    def scores(kc, dst):
        for m in range(2):
            dst[m] = lax.dot_general(q_ref[:, m * hd:(m + 1) * hd], kc[:, m * hd:(m + 1) * hd],
                                     (((1,), (1,)), ((), ())), preferred_element_type=F32)

    def absorb(src, vc):
        scaled = []
        for m in range(2):
            s = src[m]
            m_old = m_ref[m]
            m_new = jnp.maximum(m_old, jnp.max(s, axis=-1, keepdims=True))
            alpha = jnp.exp2(m_old - m_new)
            p = jnp.exp2(s - m_new)
            l_ref[m] = alpha * l_ref[m] + jnp.sum(p, axis=-1, keepdims=True)
            m_ref[m] = m_new
            scaled.append((alpha, p.astype(BF)))
        for m in range(2):
            alpha, p = scaled[m]
            acc_ref[m] = alpha * acc_ref[m] + jnp.dot(p, vc, preferred_element_type=F32)

    def k_chunk(c):
        return k_ref[pl.ds(pl.multiple_of(c * tk, tk), tk), :]

    def v_chunk(c):
        return v_ref[pl.ds(pl.multiple_of(c * tk, tk), tk), :]

    scores(k_chunk(0), sa_ref)

    def pair(i, carry):
        c = 2 * i
        scores(k_chunk(c + 1), sb_ref)
        absorb(sa_ref, v_chunk(c))
        scores(k_chunk(c + 2), sa_ref)
        absorb(sb_ref, v_chunk(c + 1))
        return carry

    n_pairs = (n_chunks - 1) // 2
    lax.fori_loop(0, n_pairs, pair, 0)
    c = 2 * n_pairs
    if n_chunks - c == 2:
        scores(k_chunk(c + 1), sb_ref)
        absorb(sa_ref, v_chunk(c))
        if has_ctx:
            scores(k_ref[n_lat:n_kv, :], sc_ref)
        absorb(sb_ref, v_chunk(c + 1))
    else:
        if has_ctx:
            scores(k_ref[n_lat:n_kv, :], sc_ref)
        absorb(sa_ref, v_chunk(c))
    if has_ctx:
        absorb(sc_ref, v_ref[n_lat:n_kv, :])

    lf = lam_ref[...]
    lam = (jnp.exp(jnp.sum(lf[0:1] * lf[1:2], axis=-1, keepdims=True))
           - jnp.exp(jnp.sum(lf[2:3] * lf[3:4], axis=-1, keepdims=True)) + lambda_init)
    o = acc_ref[0] / l_ref[0] - lam * (acc_ref[1] / l_ref[1])
    o = o * lax.rsqrt(jnp.mean(o * o, axis=-1, keepdims=True) + NORM_EPS)
    o_ref[...] = ((o * sg_ref[...]) * (1.0 - lambda_init)).astype(BF)


def _diff_attention(qkv, lambdas, subln_gain, n_lat, n_heads, lambda_init):
    R = qkv.shape[0]
    D = qkv.shape[1] // 3
    vd = D // n_heads
    tq = _pick(n_lat, 512, BF16_SUBLANES)
    tk = _pick(n_lat, 1024, LANES)
    return pl.pallas_call(
        functools.partial(_attn_kernel, n_lat, tk, lambda_init),
        grid=(n_heads, n_lat // tq),
        in_specs=[pl.BlockSpec((tq, vd), lambda h, i: (i, h)),
                  pl.BlockSpec((R, vd), lambda h, i: (0, n_heads + h)),
                  pl.BlockSpec((R, vd), lambda h, i: (0, 2 * n_heads + h)),
                  pl.BlockSpec(lambdas.shape, lambda h, i: (0, 0)),
                  pl.BlockSpec((1, vd), lambda h, i: (0, 0))],
        out_specs=pl.BlockSpec((tq, vd), lambda h, i: (i, h)),
        out_shape=jax.ShapeDtypeStruct((n_lat, D), BF),
        scratch_shapes=[pltpu.VMEM((2, tq, tk), F32), pltpu.VMEM((2, tq, tk), F32),
                        pltpu.VMEM((2, tq, (R - n_lat) or LANES), F32),
                        pltpu.VMEM((2, tq, 1), F32), pltpu.VMEM((2, tq, 1), F32),
                        pltpu.VMEM((2, tq, vd), F32)],
        compiler_params=_params("arbitrary", "arbitrary"),
        name="diff_attention",
    )(qkv, qkv, qkv, lambdas, subln_gain.reshape(1, vd))


def _router_kernel(n_experts, x_ref, g_ref, sh_ref, sc_ref, w_ref, comb_ref, sel_ref):
    u = _modnorm_rows(x_ref[...], g_ref[...], sh_ref[...], sc_ref[...]).astype(BF)
    logits = jnp.dot(u, w_ref[...], preferred_element_type=F32)
    lane = lax.broadcasted_iota(jnp.int32, logits.shape, 1)
    neg = jnp.float32(-jnp.inf)
    l1 = jnp.where(lane < n_experts, logits, neg)
    m1 = jnp.max(l1, axis=-1, keepdims=True)
    i1 = jnp.min(jnp.where(l1 == m1, lane, LANES), axis=-1, keepdims=True)
    sel1 = lane == i1
    l2 = jnp.where(sel1, neg, l1)
    m2 = jnp.max(l2, axis=-1, keepdims=True)
    i2 = jnp.min(jnp.where(l2 == m2, lane, LANES), axis=-1, keepdims=True)
    sel2 = lane == i2
    e2 = jnp.exp(m2 - m1)
    den = 1.0 + e2
    comb_ref[...] = jnp.where(sel1, 1.0 / den, 0.0) + jnp.where(sel2, e2 / den, 0.0)
    sel_ref[...] = (sel1 | sel2).astype(jnp.int32)


def _router(x, gain, shift, scale, w_router_pad, n_experts):
    S, D = x.shape
    tm = _pick(S, 512, 8)
    vec = pl.BlockSpec((1, D), lambda i: (0, 0))
    out = pl.BlockSpec((tm, LANES), lambda i: (i, 0))
    return pl.pallas_call(
        functools.partial(_router_kernel, n_experts),
        grid=(S // tm,),
        in_specs=[pl.BlockSpec((tm, D), lambda i: (i, 0)), vec, vec, vec,
                  pl.BlockSpec((D, LANES), lambda i: (0, 0))],
        out_specs=[out, out],
        out_shape=[jax.ShapeDtypeStruct((S, LANES), F32), jax.ShapeDtypeStruct((S, LANES), jnp.int32)],
        compiler_params=_params("arbitrary"),
        name="moe_router",
    )(x, gain.reshape(1, D), shift.reshape(1, D), scale.reshape(1, D), w_router_pad)


def _row_copy(src_hbm, row, dst, r, sem):
    return pltpu.make_async_copy(src_hbm.at[pl.ds(row, 1)], dst.at[pl.ds(r, 1)], sem)


def _gather_norm_kernel(tok_ref, nrows_ref, x_hbm, g_ref, sh_ref, sc_ref, o_ref, buf, sem):
    tg = buf.shape[0]
    base = pl.program_id(0) * tg

    @pl.when(base < nrows_ref[0])
    def _():
        def issue(r, c):
            _row_copy(x_hbm, tok_ref[base + r], buf, r, sem).start()
            return c

        def wait(r, c):
            _row_copy(x_hbm, 0, buf, r, sem).wait()
            return c

        lax.fori_loop(0, tg, issue, 0)
        lax.fori_loop(0, tg, wait, 0)
        o_ref[...] = _modnorm_rows(buf[...], g_ref[...], sh_ref[...], sc_ref[...]).astype(BF)

    @pl.when(base >= nrows_ref[0])
    def _():
        o_ref[...] = jnp.zeros_like(o_ref)


def _gather_norm(x, row_token, n_rows_used, gain, shift, scale):
    D = x.shape[1]
    Rs = row_token.shape[0]
    tg = _pick(Rs, 256, BF16_SUBLANES)
    vec = pl.BlockSpec((1, D), lambda t, tok, n: (0, 0))
    return pl.pallas_call(
        _gather_norm_kernel,
        grid_spec=pltpu.PrefetchScalarGridSpec(
            num_scalar_prefetch=2,
            grid=(Rs // tg,),
            in_specs=[pl.BlockSpec(memory_space=pl.ANY), vec, vec, vec],
            out_specs=pl.BlockSpec((tg, D), lambda t, tok, n: (t, 0)),
            scratch_shapes=[pltpu.VMEM((tg, D), F32), pltpu.SemaphoreType.DMA(())]),
        out_shape=jax.ShapeDtypeStruct((Rs, D), BF),
        compiler_params=_params("arbitrary"),
        name="moe_gather_norm",
    )(row_token, n_rows_used, x, gain.reshape(1, D), shift.reshape(1, D), scale.reshape(1, D))


def _expert_changed(te_ref):
    t = pl.program_id(1)
    return (t == 0) | (te_ref[t] != te_ref[jnp.maximum(t - 1, 0)])


def _moe_gu_kernel(te_ref, nt_ref, a_ref, wg_ref, wu_ref, o_ref, w_bf):
    t = pl.program_id(1)

    @pl.when(t < nt_ref[0])
    def _():
        @pl.when(_expert_changed(te_ref))
        def _():
            w_bf[0] = wg_ref[0, 0].astype(BF)
            w_bf[1] = wu_ref[0, 0].astype(BF)

        a = a_ref[...]
        g = jnp.dot(a, w_bf[0], preferred_element_type=F32)
        u = jnp.dot(a, w_bf[1], preferred_element_type=F32)
        o_ref[...] = ((g * _sigmoid(g)) * u).astype(BF)

    @pl.when(t >= nt_ref[0])
    def _():
        o_ref[...] = jnp.zeros_like(o_ref)


def _moe_gu(a, w_gu, tile_expert, n_tiles, tm):
    Rs, D = a.shape
    F = w_gu.shape[3] // 2
    tn = _pick(F, 512, LANES)
    nj = F // tn
    tile = lambda t, nt: jnp.minimum(t, nt[0] - 1)
    return pl.pallas_call(
        _moe_gu_kernel,
        grid_spec=pltpu.PrefetchScalarGridSpec(
            num_scalar_prefetch=2,
            grid=(nj, Rs // tm),
            in_specs=[pl.BlockSpec((tm, D), lambda j, t, te, nt: (tile(t, nt), 0)),
                      pl.BlockSpec((1, 1, D, tn), lambda j, t, te, nt: (0, te[tile(t, nt)], 0, j)),
                      pl.BlockSpec((1, 1, D, tn), lambda j, t, te, nt: (0, te[tile(t, nt)], 0, j + nj))],
            out_specs=pl.BlockSpec((tm, tn), lambda j, t, te, nt: (t, j)),
            scratch_shapes=[pltpu.VMEM((2, D, tn), BF)]),
        out_shape=jax.ShapeDtypeStruct((Rs, F), BF),
        compiler_params=_params("arbitrary", "arbitrary"),
        name="moe_gu",
    )(tile_expert, n_tiles, a, w_gu, w_gu)


def _moe_down_kernel(te_ref, nt_ref, a_ref, w_ref, o_ref, w_bf):
    t = pl.program_id(1)

    @pl.when(t < nt_ref[0])
    def _():
        @pl.when(_expert_changed(te_ref))
        def _():
            w_bf[...] = w_ref[0, 0].astype(BF)

        o_ref[...] = jnp.dot(a_ref[...], w_bf[...], preferred_element_type=F32)

    @pl.when(t >= nt_ref[0])
    def _():
        o_ref[...] = jnp.zeros_like(o_ref)


def _moe_down(h, w_down, tile_expert, n_tiles, tm):
    Rs, F = h.shape
    D = w_down.shape[3]
    tn = _pick(D, 512, LANES)
    tile = lambda t, nt: jnp.minimum(t, nt[0] - 1)
    return pl.pallas_call(
        _moe_down_kernel,
        grid_spec=pltpu.PrefetchScalarGridSpec(
            num_scalar_prefetch=2,
            grid=(D // tn, Rs // tm),
            in_specs=[pl.BlockSpec((tm, F), lambda j, t, te, nt: (tile(t, nt), 0)),
                      pl.BlockSpec((1, 1, F, tn), lambda j, t, te, nt: (0, te[tile(t, nt)], 0, j))],
            out_specs=pl.BlockSpec((tm, tn), lambda j, t, te, nt: (t, j)),
            scratch_shapes=[pltpu.VMEM((F, tn), BF)]),
        out_shape=jax.ShapeDtypeStruct((Rs, D), F32),
        compiler_params=_params("arbitrary", "arbitrary"),
        name="moe_down",
    )(tile_expert, n_tiles, h, w_down)


def _combine_kernel(pa_ref, pb_ref, y_hbm, x_ref, g_ref, wa_ref, wb_ref, o_ref, ya, yb, sem):
    tt = ya.shape[0]
    base = pl.program_id(0) * tt

    def issue(r, c):
        _row_copy(y_hbm, pa_ref[base + r], ya, r, sem.at[0]).start()
        _row_copy(y_hbm, pb_ref[base + r], yb, r, sem.at[1]).start()
        return c

    def wait(r, c):
        _row_copy(y_hbm, 0, ya, r, sem.at[0]).wait()
        _row_copy(y_hbm, 0, yb, r, sem.at[1]).wait()
        return c

    lax.fori_loop(0, tt, issue, 0)
    lax.fori_loop(0, tt, wait, 0)
    o_ref[...] = x_ref[...] + g_ref[...] * (wa_ref[...] * ya[...] + wb_ref[...] * yb[...])


def _moe_combine(x, y, pos_a, pos_b, w_a, w_b, gate):
    S, D = x.shape
    tt = _pick(S, 256, 8)
    col = pl.BlockSpec((tt, 1), lambda t, pa, pb: (t, 0))
    return pl.pallas_call(
        _combine_kernel,
        grid_spec=pltpu.PrefetchScalarGridSpec(
            num_scalar_prefetch=2,
            grid=(S // tt,),
            in_specs=[pl.BlockSpec(memory_space=pl.ANY),
                      pl.BlockSpec((tt, D), lambda t, pa, pb: (t, 0)),
                      pl.BlockSpec((1, D), lambda t, pa, pb: (0, 0)),
                      col, col],
            out_specs=pl.BlockSpec((tt, D), lambda t, pa, pb: (t, 0)),
            scratch_shapes=[pltpu.VMEM((tt, D), F32), pltpu.VMEM((tt, D), F32),
                            pltpu.SemaphoreType.DMA((2,))]),
        out_shape=jax.ShapeDtypeStruct((S, D), F32),
        compiler_params=_params("arbitrary"),
        name="moe_combine",
    )(pos_a, pos_b, y, x, gate.reshape(1, D), w_a, w_b)


def _routing_tables(sel, comb, tm):
    S, E = sel.shape
    Rs = TOP_K * S + E * tm
    seli = sel.astype(jnp.int32)
    cnt = jnp.sum(seli, axis=0)
    padded = (cnt + tm - 1) // tm * tm
    gend = jnp.cumsum(padded)
    gstart = gend - padded
    pos = gstart[None, :] + jnp.cumsum(seli, axis=0) - seli
    pos_a = jnp.min(jnp.where(sel, pos, Rs), axis=1).astype(jnp.int32)
    pos_b = jnp.max(jnp.where(sel, pos, -1), axis=1).astype(jnp.int32)
    w_a = jnp.sum(jnp.where(sel & (pos == pos_a[:, None]), comb, 0.0), axis=1, keepdims=True)
    w_b = jnp.sum(jnp.where(sel & (pos == pos_b[:, None]), comb, 0.0), axis=1, keepdims=True)
    tok = jnp.arange(S, dtype=jnp.int32)
    row_token = jnp.zeros((Rs,), jnp.int32).at[jnp.concatenate([pos_a, pos_b])].set(
        jnp.concatenate([tok, tok]), mode="drop")
    tile_start = jnp.arange(Rs // tm, dtype=jnp.int32) * tm
    tile_expert = jnp.minimum(
        jnp.sum((tile_start[:, None] >= gend[None, :]).astype(jnp.int32), axis=1), E - 1).astype(jnp.int32)
    n_rows = gend[-1].astype(jnp.int32).reshape(1)
    return row_token, pos_a, pos_b, w_a, w_b, tile_expert, n_rows, (n_rows // tm)


def _rope_tables(n_lat, n_rows, head_dim):
    n_freq = head_dim // 4
    s = jnp.arange(n_lat, dtype=jnp.int32)
    inv = ROPE_BASE ** (-jnp.arange(n_freq, dtype=F32) / n_freq)
    ang = jnp.stack([s // GRID_W, s % GRID_W], axis=-1).astype(F32)[..., None] * inv
    cos, sin = jnp.cos(ang), jnp.sin(ang)
    zero = jnp.zeros_like(sin)
    cos_t = jnp.stack([cos, cos], axis=2).reshape(n_lat, head_dim)
    sa_t = jnp.stack([-sin, zero], axis=2).reshape(n_lat, head_dim)
    sb_t = jnp.stack([zero, sin], axis=2).reshape(n_lat, head_dim)
    pad = ((0, n_rows - n_lat), (0, 0))
    return jnp.pad(cos_t, pad, constant_values=1.0), jnp.pad(sa_t, pad), jnp.pad(sb_t, pad)


def kernel(x, c, ctx, c_ctx, w_mod, b_mod, norm_gain, conv_w_in, conv_w, conv_w_out, attn_w_qkv, attn_w_o,
           attn_q_gain, attn_k_gain, attn_lambdas, attn_subln_gain, ffn_w_gu, ffn_w_down, moe_router,
           moe_w_gu, moe_w_down):
    assert x.shape[0] == 1 and w_mod.shape[0] == 2
    S, D = x.shape[1], x.shape[2]
    C = ctx.shape[1]
    R = S + C
    head_dim = attn_q_gain.shape[-1]
    n_heads = D // (2 * head_dim)
    E = moe_router.shape[-1]
    F = ffn_w_down.shape[1]
    f_pad = _round_up(F, 512)

    cc = jnp.zeros((BF16_SUBLANES, D), F32).at[0].set(c[0]).at[1].set(c_ctx)
    mod = _mod_vectors(cc, w_mod, b_mod)[:, :2, :].reshape(2, 2, N_MOD, D)

    xa = jnp.concatenate([x[0], ctx[0]], axis=0)

    m0 = mod[0]
    u = _modnorm(xa, norm_gain[0, 0], m0[:, 0], m0[:, 1], S, R)
    b, cv = _conv_in(u, conv_w_in)
    bz = _conv_mix(b, cv, conv_w[0], S)
    xa = _proj_resid(bz, conv_w_out, xa, m0[:, 2], S)
    u = _modnorm(xa, norm_gain[0, 1], m0[:, 3], m0[:, 4], S, R)
    w_gu = jnp.pad(ffn_w_gu.reshape(D, 2, F).astype(BF), ((0, 0), (0, 0), (0, f_pad - F)))
    w_dn = jnp.pad(ffn_w_down.reshape(F, D).astype(BF), ((0, f_pad - F), (0, 0)))
    h = _ffn_gu(u, w_gu.reshape(D, 2 * f_pad), f_pad)
    xa = _ffn_down(h, w_dn, xa, m0[:, 5], S)

    m1 = mod[1]
    lambda_init = 0.8 - 0.6 * math.exp(-0.3 * 1)
    u = _modnorm(xa, norm_gain[1, 0], m1[:, 0], m1[:, 1], S, R)
    cos, sa, sb = _rope_tables(S, R, head_dim)
    qk_gain = jnp.stack([attn_q_gain[0], attn_k_gain[0]]).reshape(2, 1, head_dim)
    qkv = _qkv(u, attn_w_qkv, qk_gain, cos, sa, sb, head_dim ** -0.5 * math.log2(math.e))
    o = _diff_attention(qkv, attn_lambdas[0], attn_subln_gain[0], S, n_heads, lambda_init)
    x1 = _proj_resid(o, attn_w_o, xa, m1[:, 2], S)

    tm = _pick(S, 512, BF16_SUBLANES)
    w_r = jnp.pad(moe_router[0], ((0, 0), (0, LANES - E))).astype(BF)
    comb, selm = _router(x1, norm_gain[1, 1], m1[0, 3], m1[0, 4], w_r, E)
    row_token, pos_a, pos_b, w_a, w_b, tile_expert, n_rows, n_tiles = _routing_tables(
        selm[:, :E] > 0, comb[:, :E], tm)
    us = _gather_norm(x1, row_token, n_rows, norm_gain[1, 1], m1[0, 3], m1[0, 4])
    hs = _moe_gu(us, moe_w_gu, tile_expert, n_tiles, tm)
    ys = _moe_down(hs, moe_w_down, tile_expert, n_tiles, tm)
    out = _moe_combine(x1, ys, pos_a, pos_b, w_a, w_b, m1[0, 5])
    return out[None]
```

```python
import functools
import math

import jax
import jax.numpy as jnp
from jax import lax
from jax.experimental import pallas as pl
from jax.experimental.pallas import tpu as pltpu

BF = jnp.bfloat16
F32 = jnp.float32

GRID_W = 64
ROPE_BASE = 10000.0
NORM_EPS = 1e-6
TOP_K = 2
N_MOD = 6

LANES = 128
BF16_SUBLANES = 16
VMEM_LIMIT_BYTES = 56 * 1024 * 1024


def _pick(n, target, align):
    best = None
    d = align
    while d <= min(n, target):
        if n % d == 0:
            best = d
        d += align
    return best if best is not None else n


def _round_up(n, m):
    return (n + m - 1) // m * m


def _params(*sem):
    return pltpu.CompilerParams(dimension_semantics=sem, vmem_limit_bytes=VMEM_LIMIT_BYTES)


def _sigmoid(x):
    return 1.0 / (1.0 + jnp.exp(-x))


def _mod_kernel(c_ref, w_ref, b_ref, o_ref, s_ref):
    @pl.when((pl.program_id(0) == 0) & (pl.program_id(1) == 0))
    def _():
        c = c_ref[...]
        s_ref[...] = jnp.broadcast_to(c * _sigmoid(c), s_ref.shape)

    o_ref[...] = jnp.zeros_like(o_ref)
    for g in range(w_ref.shape[2] // LANES):
        w = w_ref[0, :, g * LANES:(g + 1) * LANES]
        for r in range(2):
            o_ref[0, r:r + 1, g * LANES:(g + 1) * LANES] = (
                jnp.sum(w * s_ref[r], axis=0, keepdims=True) + b_ref[0, :, g * LANES:(g + 1) * LANES])


def _mod_vectors(cc, w_mod, b_mod):
    L, D, N = w_mod.shape
    tn = _pick(N, 512, LANES)
    return pl.pallas_call(
        _mod_kernel,
        grid=(L, N // tn),
        in_specs=[pl.BlockSpec((2, D, 1), lambda l, j: (0, 0, 0)),
                  pl.BlockSpec((1, D, tn), lambda l, j: (l, 0, j)),
                  pl.BlockSpec((1, 1, tn), lambda l, j: (l, 0, j))],
        out_specs=pl.BlockSpec((1, 8, tn), lambda l, j: (l, 0, j)),
        out_shape=jax.ShapeDtypeStruct((L, 8, N), F32),
        scratch_shapes=[pltpu.VMEM((2, D, LANES), F32)],
        compiler_params=_params("arbitrary", "arbitrary"),
        name="mod_vectors",
    )(cc, w_mod, b_mod.reshape(L, 1, N))


def _modnorm_rows(x, gain, shift, scale):
    ms = jnp.mean(x * x, axis=-1, keepdims=True)
    xn = x * lax.rsqrt(ms + NORM_EPS)
    return (xn * gain) * (1.0 + scale) + shift


def _modnorm_kernel(x_ref, g_ref, sh_ref, sc_ref, o_ref):
    o_ref[...] = _modnorm_rows(x_ref[...], g_ref[...], sh_ref[0], sc_ref[0]).astype(BF)


def _modnorm(x, gain, shift2, scale2, n_lat, n_rows):
    D = x.shape[1]
    tr = _pick(math.gcd(n_lat, n_rows), 256, BF16_SUBLANES)
    n_lat_tiles = n_lat // tr
    grp = lambda i: (jnp.where(i < n_lat_tiles, 0, 1), 0, 0)
    return pl.pallas_call(
        _modnorm_kernel,
        grid=(n_rows // tr,),
        in_specs=[pl.BlockSpec((tr, D), lambda i: (i, 0)),
                  pl.BlockSpec((1, D), lambda i: (0, 0)),
                  pl.BlockSpec((1, 1, D), grp),
                  pl.BlockSpec((1, 1, D), grp)],
        out_specs=pl.BlockSpec((tr, D), lambda i: (i, 0)),
        out_shape=jax.ShapeDtypeStruct((n_rows, D), BF),
        compiler_params=_params("arbitrary"),
        name="modnorm",
    )(x, gain.reshape(1, D), shift2.reshape(2, 1, D), scale2.reshape(2, 1, D))


def _conv_in_kernel(a_ref, wb_ref, wc_ref, wv_ref, b_ref, cv_ref, w_bf):
    @pl.when(pl.program_id(1) == 0)
    def _():
        w_bf[0] = wb_ref[0].astype(BF)
        w_bf[1] = wc_ref[0].astype(BF)
        w_bf[2] = wv_ref[0].astype(BF)

    a = a_ref[...]
    b_ref[...] = jnp.dot(a, w_bf[0], preferred_element_type=F32).astype(BF)
    c = jnp.dot(a, w_bf[1], preferred_element_type=F32)
    v = jnp.dot(a, w_bf[2], preferred_element_type=F32)
    cv_ref[...] = (c * v).astype(BF)


def _conv_in(u, w_in):
    R, D = u.shape
    tm = _pick(R, 600, BF16_SUBLANES)
    tn = _pick(D, 256, LANES)
    nj = D // tn
    wspec = lambda k: pl.BlockSpec((1, D, tn), lambda j, i: (0, 0, j + k * nj))
    return pl.pallas_call(
        _conv_in_kernel,
        grid=(nj, R // tm),
        in_specs=[pl.BlockSpec((tm, D), lambda j, i: (i, 0)), wspec(0), wspec(1), wspec(2)],
        out_specs=[pl.BlockSpec((tm, tn), lambda j, i: (i, j)),
                   pl.BlockSpec((tm, tn), lambda j, i: (i, j))],
        out_shape=[jax.ShapeDtypeStruct((R, D), BF), jax.ShapeDtypeStruct((R, D), BF)],
        scratch_shapes=[pltpu.VMEM((3, D, tn), BF)],
        compiler_params=_params("arbitrary", "arbitrary"),
        name="conv_in",
    )(u, w_in, w_in, w_in)


def _conv_mix_kernel(n_lat_tiles, b_ref, cv_ref, pv_ref, nx_ref, w_ref, o_ref):
    i = pl.program_id(0)
    tr = cv_ref.shape[0]
    cv = cv_ref[...].astype(F32)
    top_zero = (i == 0) | (i == n_lat_tiles)
    bot_zero = (i == n_lat_tiles - 1) | (i == pl.num_programs(0) - 1)
    hp = pv_ref[BF16_SUBLANES - 1:BF16_SUBLANES, :].astype(F32) * jnp.where(top_zero, 0.0, 1.0)
    hn = nx_ref[0:1, :].astype(F32) * jnp.where(bot_zero, 0.0, 1.0)
    rows = lax.broadcasted_iota(jnp.int32, cv.shape, 0)
    prev = jnp.where(rows == 0, hp, pltpu.roll(cv, 1, 0))
    nxt = jnp.where(rows == tr - 1, hn, pltpu.roll(cv, tr - 1, 0))
    w = w_ref[...]
    z = prev * w[0:1] + cv * w[1:2] + nxt * w[2:3]
    o_ref[...] = (b_ref[...].astype(F32) * z).astype(BF)


def _conv_mix(b, cv, conv_w, n_lat):
    R, D = cv.shape
    tr = _pick(math.gcd(n_lat, R), 256, BF16_SUBLANES)
    tc = _pick(D, 1024, LANES)
    hb = tr // BF16_SUBLANES
    last = R // BF16_SUBLANES - 1
    return pl.pallas_call(
        functools.partial(_conv_mix_kernel, n_lat // tr),
        grid=(R // tr, D // tc),
        in_specs=[pl.BlockSpec((tr, tc), lambda i, j: (i, j)),
                  pl.BlockSpec((tr, tc), lambda i, j: (i, j)),
                  pl.BlockSpec((BF16_SUBLANES, tc), lambda i, j: (jnp.maximum(i * hb - 1, 0), j)),
                  pl.BlockSpec((BF16_SUBLANES, tc), lambda i, j: (jnp.minimum((i + 1) * hb, last), j)),
                  pl.BlockSpec((3, tc), lambda i, j: (0, j))],
        out_specs=pl.BlockSpec((tr, tc), lambda i, j: (i, j)),
        out_shape=jax.ShapeDtypeStruct((R, D), BF),
        compiler_params=_params("arbitrary", "arbitrary"),
        name="conv_mix",
    )(b, cv, cv, cv, conv_w)


def _row_gate(g_ref, row0, shape, n_lat):
    rows = row0 + lax.broadcasted_iota(jnp.int32, shape, 0)
    return jnp.where(rows < n_lat, g_ref[0:1, :], g_ref[1:2, :])


def _proj_resid_kernel(n_lat, a_ref, w_ref, r_ref, g_ref, o_ref, w_bf):
    @pl.when(pl.program_id(1) == 0)
    def _():
        w_bf[...] = w_ref[0].astype(BF)

    tm = o_ref.shape[0]
    acc = jnp.dot(a_ref[...], w_bf[...], preferred_element_type=F32)
    g = _row_gate(g_ref, pl.program_id(1) * tm, acc.shape, n_lat)
    o_ref[...] = r_ref[...] + g * acc


def _proj_resid(a, w, resid, gate2, n_lat):
    M, K = a.shape
    N = w.shape[2]
    tm = _pick(M, 1100, BF16_SUBLANES)
    tn = _pick(N, 512, LANES)
    return pl.pallas_call(
        functools.partial(_proj_resid_kernel, n_lat),
        grid=(N // tn, M // tm),
        in_specs=[pl.BlockSpec((tm, K), lambda j, i: (i, 0)),
                  pl.BlockSpec((1, K, tn), lambda j, i: (0, 0, j)),
                  pl.BlockSpec((tm, tn), lambda j, i: (i, j)),
                  pl.BlockSpec((2, tn), lambda j, i: (0, j))],
        out_specs=pl.BlockSpec((tm, tn), lambda j, i: (i, j)),
        out_shape=jax.ShapeDtypeStruct((M, N), F32),
        scratch_shapes=[pltpu.VMEM((K, tn), BF)],
        compiler_params=_params("arbitrary", "arbitrary"),
        name="proj_resid",
    )(a, w, resid, gate2)


def _gu_kernel(n_real, a_ref, wg_ref, wu_ref, o_ref):
    j = pl.program_id(1)

    @pl.when(j < n_real)
    def _():
        a = a_ref[...]
        g = jnp.dot(a, wg_ref[0], preferred_element_type=F32)
        u = jnp.dot(a, wu_ref[0], preferred_element_type=F32)
        o_ref[...] = ((g * _sigmoid(g)) * u).astype(BF)

    @pl.when(j >= n_real)
    def _():
        o_ref[...] = jnp.zeros_like(o_ref)


def _ffn_gu(u, w_gu, f_pad):
    R, D = u.shape
    F = w_gu.shape[2] // 2
    tm = _pick(R, 1100, BF16_SUBLANES)
    tn = _pick(math.gcd(F, f_pad), 256, LANES)
    nf = F // tn
    col = lambda j: jnp.minimum(j, nf - 1)
    return pl.pallas_call(
        functools.partial(_gu_kernel, nf),
        grid=(R // tm, f_pad // tn),
        in_specs=[pl.BlockSpec((tm, D), lambda i, j: (i, 0)),
                  pl.BlockSpec((1, D, tn), lambda i, j: (0, 0, col(j))),
                  pl.BlockSpec((1, D, tn), lambda i, j: (0, 0, col(j) + nf))],
        out_specs=pl.BlockSpec((tm, tn), lambda i, j: (i, j)),
        out_shape=jax.ShapeDtypeStruct((R, f_pad), BF),
        compiler_params=_params("arbitrary", "arbitrary"),
        name="ffn_gu",
    )(u, w_gu, w_gu)


def _down_kernel(n_lat, a_ref, w_ref, r_ref, g_ref, o_ref, acc_ref):
    k = pl.program_id(2)

    @pl.when(k == 0)
    def _():
        acc_ref[...] = jnp.zeros_like(acc_ref)

    acc_ref[...] += jnp.dot(a_ref[...], w_ref[...], preferred_element_type=F32)

    @pl.when(k == pl.num_programs(2) - 1)
    def _():
        acc = acc_ref[...]
        g = _row_gate(g_ref, pl.program_id(0) * acc.shape[0], acc.shape, n_lat)
        o_ref[...] = r_ref[...] + g * acc


def _ffn_down(h, w_down, resid, gate2, n_lat):
    R, Fp = h.shape
    D = w_down.shape[1]
    tm = _pick(R, 1100, BF16_SUBLANES)
    tn = _pick(D, 1024, LANES)
    tk = _pick(Fp, 2816, LANES)
    return pl.pallas_call(
        functools.partial(_down_kernel, n_lat),
        grid=(R // tm, D // tn, Fp // tk),
        in_specs=[pl.BlockSpec((tm, tk), lambda i, j, k: (i, k)),
                  pl.BlockSpec((tk, tn), lambda i, j, k: (k, j)),
                  pl.BlockSpec((tm, tn), lambda i, j, k: (i, j)),
                  pl.BlockSpec((2, tn), lambda i, j, k: (0, j))],
        out_specs=pl.BlockSpec((tm, tn), lambda i, j, k: (i, j)),
        out_shape=jax.ShapeDtypeStruct((R, D), F32),
        scratch_shapes=[pltpu.VMEM((tm, tn), F32)],
        compiler_params=_params("arbitrary", "arbitrary", "arbitrary"),
        name="ffn_down",
    )(h, w_down, resid, gate2)


def _qkv_kernel(n_qk_tiles, q_scale, a_ref, w_ref, gain_ref, cos_ref, sa_ref, sb_ref, o_ref, w_bf):
    j = pl.program_id(0)

    @pl.when(pl.program_id(1) == 0)
    def _():
        w_bf[...] = w_ref[0].astype(BF)

    acc = jnp.dot(a_ref[...], w_bf[...], preferred_element_type=F32)

    @pl.when(j >= 2 * n_qk_tiles)
    def _():
        o_ref[...] = acc.astype(BF)

    @pl.when(j < 2 * n_qk_tiles)
    def _():
        gain = gain_ref[0]
        cos, sa, sb = cos_ref[...], sa_ref[...], sb_ref[...]
        post = jnp.where(j < n_qk_tiles, q_scale, 1.0)
        half = LANES // 4
        for g in range(acc.shape[1] // LANES):
            t = acc[:, g * LANES:(g + 1) * LANES]
            t = t * lax.rsqrt(jnp.mean(t * t, axis=-1, keepdims=True) + NORM_EPS)
            t = t * gain
            t = t * cos + pltpu.roll(t, LANES - half, 1) * sa + pltpu.roll(t, half, 1) * sb
            o_ref[:, g * LANES:(g + 1) * LANES] = (t * post).astype(BF)


def _qkv(u, w_qkv, qk_gain2, cos, sa, sb, q_scale):
    R, D = u.shape
    tm = _pick(R, 1100, BF16_SUBLANES)
    tn = _pick(D, 512, LANES)
    nq = D // tn
    tab = pl.BlockSpec((tm, LANES), lambda j, i: (i, 0))
    return pl.pallas_call(
        functools.partial(_qkv_kernel, nq, q_scale),
        grid=(3 * nq, R // tm),
        in_specs=[pl.BlockSpec((tm, D), lambda j, i: (i, 0)),
                  pl.BlockSpec((1, D, tn), lambda j, i: (0, 0, j)),
                  pl.BlockSpec((1, 1, LANES), lambda j, i: (jnp.where(j < nq, 0, 1), 0, 0)),
                  tab, tab, tab],
        out_specs=pl.BlockSpec((tm, tn), lambda j, i: (i, j)),
        out_shape=jax.ShapeDtypeStruct((R, 3 * D), BF),
        scratch_shapes=[pltpu.VMEM((D, tn), BF)],
        compiler_params=_params("arbitrary", "arbitrary"),
        name="qkv_proj",
    )(u, w_qkv, qk_gain2, cos, sa, sb)


def _attn_kernel(n_lat, tk, lambda_init, q_ref, k_ref, v_ref, lam_ref, sg_ref, o_ref,
                 sa_ref, sb_ref, sc_ref, m_ref, l_ref, acc_ref):
    hd = q_ref.shape[1] // 2
    n_kv = k_ref.shape[0]
    n_chunks = n_lat // tk
    has_ctx = n_kv > n_lat
    m_ref[...] = jnp.full_like(m_ref, -jnp.inf)
    l_ref[...] = jnp.zeros_like(l_ref)
    acc_ref[...] = jnp.zeros_like(acc_ref)

    def scores(kc, dst):
        for m in range(2):
            dst[m] = lax.dot_general(q_ref[:, m * hd:(m + 1) * hd], kc[:, m * hd:(m + 1) * hd],
                                     (((1,), (1,)), ((), ())), preferred_element_type=F32)

    def absorb(src, vc):
        scaled = []
        for m in range(2):
            s = src[m]
            m_old = m_ref[m]
            m_new = jnp.maximum(m_old, jnp.max(s, axis=-1, keepdims=True))
            alpha = jnp.exp2(m_old - m_new)
            p = jnp.exp2(s - m_new)
            l_ref[m] = alpha * l_ref[m] + jnp.sum(p, axis=-1, keepdims=True)
            m_ref[m] = m_new
            scaled.append((alpha, p.astype(BF)))
        for m in range(2):
            alpha, p = scaled[m]
            acc_ref[m] = alpha * acc_ref[m] + jnp.dot(p, vc, preferred_element_type=F32)

    def k_chunk(c):
        return k_ref[pl.ds(pl.multiple_of(c * tk, tk), tk), :]

    def v_chunk(c):
        return v_ref[pl.ds(pl.multiple_of(c * tk, tk), tk), :]

    scores(k_chunk(0), sa_ref)

    def pair(i, carry):
        c = 2 * i
        scores(k_chunk(c + 1), sb_ref)
        absorb(sa_ref, v_chunk(c))
        scores(k_chunk(c + 2), sa_ref)
        absorb(sb_ref, v_chunk(c + 1))
        return carry

    n_pairs = (n_chunks - 1) // 2
    lax.fori_loop(0, n_pairs, pair, 0)
    c = 2 * n_pairs
    if n_chunks - c == 2:
        scores(k_chunk(c + 1), sb_ref)
        absorb(sa_ref, v_chunk(c))
        if has_ctx:
            scores(k_ref[n_lat:n_kv, :], sc_ref)
        absorb(sb_ref, v_chunk(c + 1))
    else:
        if has_ctx:
            scores(k_ref[n_lat:n_kv, :], sc_ref)
        absorb(sa_ref, v_chunk(c))
    if has_ctx:
        absorb(sc_ref, v_ref[n_lat:n_kv, :])

    lf = lam_ref[...]
    lam = (jnp.exp(jnp.sum(lf[0:1] * lf[1:2], axis=-1, keepdims=True))
           - jnp.exp(jnp.sum(lf[2:3] * lf[3:4], axis=-1, keepdims=True)) + lambda_init)
    o = acc_ref[0] / l_ref[0] - lam * (acc_ref[1] / l_ref[1])
    o = o * lax.rsqrt(jnp.mean(o * o, axis=-1, keepdims=True) + NORM_EPS)
    o_ref[...] = ((o * sg_ref[...]) * (1.0 - lambda_init)).astype(BF)


def _diff_attention(qkv, lambdas, subln_gain, n_lat, n_heads, lambda_init):
    R = qkv.shape[0]
    D = qkv.shape[1] // 3
    vd = D // n_heads
    tq = _pick(n_lat, 512, BF16_SUBLANES)
    tk = _pick(n_lat, 1024, LANES)
    return pl.pallas_call(
        functools.partial(_attn_kernel, n_lat, tk, lambda_init),
        grid=(n_heads, n_lat // tq),
        in_specs=[pl.BlockSpec((tq, vd), lambda h, i: (i, h)),
                  pl.BlockSpec((R, vd), lambda h, i: (0, n_heads + h)),
                  pl.BlockSpec((R, vd), lambda h, i: (0, 2 * n_heads + h)),
                  pl.BlockSpec(lambdas.shape, lambda h, i: (0, 0)),
                  pl.BlockSpec((1, vd), lambda h, i: (0, 0))],
        out_specs=pl.BlockSpec((tq, vd), lambda h, i: (i, h)),
        out_shape=jax.ShapeDtypeStruct((n_lat, D), BF),
        scratch_shapes=[pltpu.VMEM((2, tq, tk), F32), pltpu.VMEM((2, tq, tk), F32),
                        pltpu.VMEM((2, tq, (R - n_lat) or LANES), F32),
                        pltpu.VMEM((2, tq, 1), F32), pltpu.VMEM((2, tq, 1), F32),
                        pltpu.VMEM((2, tq, vd), F32)],
        compiler_params=_params("arbitrary", "arbitrary"),
        name="diff_attention",
    )(qkv, qkv, qkv, lambdas, subln_gain.reshape(1, vd))


def _router_kernel(n_experts, x_ref, g_ref, sh_ref, sc_ref, w_ref, comb_ref, sel_ref):
    u = _modnorm_rows(x_ref[...], g_ref[...], sh_ref[...], sc_ref[...]).astype(BF)
    logits = jnp.dot(u, w_ref[...], preferred_element_type=F32)
    lane = lax.broadcasted_iota(jnp.int32, logits.shape, 1)
    neg = jnp.float32(-jnp.inf)
    l1 = jnp.where(lane < n_experts, logits, neg)
    m1 = jnp.max(l1, axis=-1, keepdims=True)
    i1 = jnp.min(jnp.where(l1 == m1, lane, LANES), axis=-1, keepdims=True)
    sel1 = lane == i1
    l2 = jnp.where(sel1, neg, l1)
    m2 = jnp.max(l2, axis=-1, keepdims=True)
    i2 = jnp.min(jnp.where(l2 == m2, lane, LANES), axis=-1, keepdims=True)
    sel2 = lane == i2
    e2 = jnp.exp(m2 - m1)
    den = 1.0 + e2
    comb_ref[...] = jnp.where(sel1, 1.0 / den, 0.0) + jnp.where(sel2, e2 / den, 0.0)
    sel_ref[...] = (sel1 | sel2).astype(jnp.int32)


def _router(x, gain, shift, scale, w_router_pad, n_experts):
    S, D = x.shape
    tm = _pick(S, 512, 8)
    vec = pl.BlockSpec((1, D), lambda i: (0, 0))
    out = pl.BlockSpec((tm, LANES), lambda i: (i, 0))
    return pl.pallas_call(
        functools.partial(_router_kernel, n_experts),
        grid=(S // tm,),
        in_specs=[pl.BlockSpec((tm, D), lambda i: (i, 0)), vec, vec, vec,
                  pl.BlockSpec((D, LANES), lambda i: (0, 0))],
        out_specs=[out, out],
        out_shape=[jax.ShapeDtypeStruct((S, LANES), F32), jax.ShapeDtypeStruct((S, LANES), jnp.int32)],
        compiler_params=_params("arbitrary"),
        name="moe_router",
    )(x, gain.reshape(1, D), shift.reshape(1, D), scale.reshape(1, D), w_router_pad)


def _row_copy(src_hbm, row, dst, r, sem):
    return pltpu.make_async_copy(src_hbm.at[pl.ds(row, 1)], dst.at[pl.ds(r, 1)], sem)


ROW_DMA_UNROLL = 8


def _gather_norm_kernel(tok_ref, nrows_ref, x_hbm, g_ref, sh_ref, sc_ref, o_ref, buf, sem):
    tg = buf.shape[1]
    t = pl.program_id(0)

    def used(tile):
        return tile * tg < nrows_ref[0]

    def issue(tile):
        slot = tile % 2

        def one(r, c):
            _row_copy(x_hbm, tok_ref[tile * tg + r], buf.at[slot], r, sem.at[slot]).start()
            return c

        lax.fori_loop(0, tg, one, 0, unroll=ROW_DMA_UNROLL)

    def wait(tile):
        slot = tile % 2

        def one(r, c):
            _row_copy(x_hbm, 0, buf.at[slot], r, sem.at[slot]).wait()
            return c

        lax.fori_loop(0, tg, one, 0, unroll=ROW_DMA_UNROLL)

    @pl.when((t == 0) & used(0))
    def _():
        issue(0)

    @pl.when((t + 1 < pl.num_programs(0)) & used(t + 1))
    def _():
        issue(t + 1)

    @pl.when(used(t))
    def _():
        wait(t)
        o_ref[...] = _modnorm_rows(buf[t % 2], g_ref[...], sh_ref[...], sc_ref[...]).astype(BF)

    @pl.when(jnp.logical_not(used(t)))
    def _():
        o_ref[...] = jnp.zeros_like(o_ref)


def _gather_norm(x, row_token, n_rows_used, gain, shift, scale):
    D = x.shape[1]
    Rs = row_token.shape[0]
    tg = _pick(Rs, 256, BF16_SUBLANES)
    vec = pl.BlockSpec((1, D), lambda t, tok, n: (0, 0))
    return pl.pallas_call(
        _gather_norm_kernel,
        grid_spec=pltpu.PrefetchScalarGridSpec(
            num_scalar_prefetch=2,
            grid=(Rs // tg,),
            in_specs=[pl.BlockSpec(memory_space=pl.ANY), vec, vec, vec],
            out_specs=pl.BlockSpec((tg, D), lambda t, tok, n: (t, 0)),
            scratch_shapes=[pltpu.VMEM((2, tg, D), F32), pltpu.SemaphoreType.DMA((2,))]),
        out_shape=jax.ShapeDtypeStruct((Rs, D), BF),
        compiler_params=_params("arbitrary"),
        name="moe_gather_norm",
    )(row_token, n_rows_used, x, gain.reshape(1, D), shift.reshape(1, D), scale.reshape(1, D))


def _expert_changed(te_ref):
    t = pl.program_id(1)
    return (t == 0) | (te_ref[t] != te_ref[jnp.maximum(t - 1, 0)])


def _moe_gu_kernel(te_ref, nt_ref, a_ref, wg_ref, wu_ref, o_ref, w_bf):
    t = pl.program_id(1)

    @pl.when(t < nt_ref[0])
    def _():
        @pl.when(_expert_changed(te_ref))
        def _():
            w_bf[0] = wg_ref[0, 0].astype(BF)
            w_bf[1] = wu_ref[0, 0].astype(BF)

        a = a_ref[...]
        g = jnp.dot(a, w_bf[0], preferred_element_type=F32)
        u = jnp.dot(a, w_bf[1], preferred_element_type=F32)
        o_ref[...] = ((g * _sigmoid(g)) * u).astype(BF)

    @pl.when(t >= nt_ref[0])
    def _():
        o_ref[...] = jnp.zeros_like(o_ref)


def _moe_gu(a, w_gu, tile_expert, n_tiles, tm):
    Rs, D = a.shape
    F = w_gu.shape[3] // 2
    tn = _pick(F, 512, LANES)
    nj = F // tn
    tile = lambda t, nt: jnp.minimum(t, nt[0] - 1)
    return pl.pallas_call(
        _moe_gu_kernel,
        grid_spec=pltpu.PrefetchScalarGridSpec(
            num_scalar_prefetch=2,
            grid=(nj, Rs // tm),
            in_specs=[pl.BlockSpec((tm, D), lambda j, t, te, nt: (tile(t, nt), 0)),
                      pl.BlockSpec((1, 1, D, tn), lambda j, t, te, nt: (0, te[tile(t, nt)], 0, j)),
                      pl.BlockSpec((1, 1, D, tn), lambda j, t, te, nt: (0, te[tile(t, nt)], 0, j + nj))],
            out_specs=pl.BlockSpec((tm, tn), lambda j, t, te, nt: (t, j)),
            scratch_shapes=[pltpu.VMEM((2, D, tn), BF)]),
        out_shape=jax.ShapeDtypeStruct((Rs, F), BF),
        compiler_params=_params("arbitrary", "arbitrary"),
        name="moe_gu",
    )(tile_expert, n_tiles, a, w_gu, w_gu)


def _moe_down_kernel(te_ref, nt_ref, a_ref, w_ref, o_ref, w_bf):
    t = pl.program_id(1)

    @pl.when(t < nt_ref[0])
    def _():
        @pl.when(_expert_changed(te_ref))
        def _():
            w_bf[...] = w_ref[0, 0].astype(BF)

        o_ref[...] = jnp.dot(a_ref[...], w_bf[...], preferred_element_type=F32)

    @pl.when(t >= nt_ref[0])
    def _():
        o_ref[...] = jnp.zeros_like(o_ref)


def _moe_down(h, w_down, tile_expert, n_tiles, tm):
    Rs, F = h.shape
    D = w_down.shape[3]
    tn = _pick(D, 512, LANES)
    tile = lambda t, nt: jnp.minimum(t, nt[0] - 1)
    return pl.pallas_call(
        _moe_down_kernel,
        grid_spec=pltpu.PrefetchScalarGridSpec(
            num_scalar_prefetch=2,
            grid=(D // tn, Rs // tm),
            in_specs=[pl.BlockSpec((tm, F), lambda j, t, te, nt: (tile(t, nt), 0)),
                      pl.BlockSpec((1, 1, F, tn), lambda j, t, te, nt: (0, te[tile(t, nt)], 0, j))],
            out_specs=pl.BlockSpec((tm, tn), lambda j, t, te, nt: (t, j)),
            scratch_shapes=[pltpu.VMEM((F, tn), BF)]),
        out_shape=jax.ShapeDtypeStruct((Rs, D), F32),
        compiler_params=_params("arbitrary", "arbitrary"),
        name="moe_down",
    )(tile_expert, n_tiles, h, w_down)


def _combine_kernel(pa_ref, pb_ref, y_hbm, x_ref, g_ref, wa_ref, wb_ref, o_ref, ya, yb, sem):
    tt = ya.shape[1]
    t = pl.program_id(0)

    def issue(tile):
        slot = tile % 2

        def one(r, c):
            _row_copy(y_hbm, pa_ref[tile * tt + r], ya.at[slot], r, sem.at[0, slot]).start()
            _row_copy(y_hbm, pb_ref[tile * tt + r], yb.at[slot], r, sem.at[1, slot]).start()
            return c

        lax.fori_loop(0, tt, one, 0, unroll=ROW_DMA_UNROLL)

    def wait(tile):
        slot = tile % 2

        def one(r, c):
            _row_copy(y_hbm, 0, ya.at[slot], r, sem.at[0, slot]).wait()
            _row_copy(y_hbm, 0, yb.at[slot], r, sem.at[1, slot]).wait()
            return c

        lax.fori_loop(0, tt, one, 0, unroll=ROW_DMA_UNROLL)

    @pl.when(t == 0)
    def _():
        issue(0)

    @pl.when(t + 1 < pl.num_programs(0))
    def _():
        issue(t + 1)

    wait(t)
    slot = t % 2
    o_ref[...] = x_ref[...] + g_ref[...] * (wa_ref[...] * ya[slot] + wb_ref[...] * yb[slot])


def _moe_combine(x, y, pos_a, pos_b, w_a, w_b, gate):
    S, D = x.shape
    tt = _pick(S, 256, 8)
    col = pl.BlockSpec((tt, 1), lambda t, pa, pb: (t, 0))
    return pl.pallas_call(
        _combine_kernel,
        grid_spec=pltpu.PrefetchScalarGridSpec(
            num_scalar_prefetch=2,
            grid=(S // tt,),
            in_specs=[pl.BlockSpec(memory_space=pl.ANY),
                      pl.BlockSpec((tt, D), lambda t, pa, pb: (t, 0)),
                      pl.BlockSpec((1, D), lambda t, pa, pb: (0, 0)),
                      col, col],
            out_specs=pl.BlockSpec((tt, D), lambda t, pa, pb: (t, 0)),
            scratch_shapes=[pltpu.VMEM((2, tt, D), F32), pltpu.VMEM((2, tt, D), F32),
                            pltpu.SemaphoreType.DMA((2, 2))]),
        out_shape=jax.ShapeDtypeStruct((S, D), F32),
        compiler_params=_params("arbitrary"),
        name="moe_combine",
    )(pos_a, pos_b, y, x, gate.reshape(1, D), w_a, w_b)


def _routing_tables(sel, comb, tm):
    S, E = sel.shape
    Rs = TOP_K * S + E * tm
    seli = sel.astype(jnp.int32)
    cnt = jnp.sum(seli, axis=0)
    padded = (cnt + tm - 1) // tm * tm
    gend = jnp.cumsum(padded)
    gstart = gend - padded
    pos = gstart[None, :] + jnp.cumsum(seli, axis=0) - seli
    pos_a = jnp.min(jnp.where(sel, pos, Rs), axis=1).astype(jnp.int32)
    pos_b = jnp.max(jnp.where(sel, pos, -1), axis=1).astype(jnp.int32)
    w_a = jnp.sum(jnp.where(sel & (pos == pos_a[:, None]), comb, 0.0), axis=1, keepdims=True)
    w_b = jnp.sum(jnp.where(sel & (pos == pos_b[:, None]), comb, 0.0), axis=1, keepdims=True)
    tok = jnp.arange(S, dtype=jnp.int32)
    row_token = jnp.zeros((Rs,), jnp.int32).at[jnp.concatenate([pos_a, pos_b])].set(
        jnp.concatenate([tok, tok]), mode="drop")
    tile_start = jnp.arange(Rs // tm, dtype=jnp.int32) * tm
    tile_expert = jnp.minimum(
        jnp.sum((tile_start[:, None] >= gend[None, :]).astype(jnp.int32), axis=1), E - 1).astype(jnp.int32)
    n_rows = gend[-1].astype(jnp.int32).reshape(1)
    return row_token, pos_a, pos_b, w_a, w_b, tile_expert, n_rows, (n_rows // tm)


def _rope_tables(n_lat, n_rows, head_dim):
    n_freq = head_dim // 4
    s = jnp.arange(n_lat, dtype=jnp.int32)
    inv = ROPE_BASE ** (-jnp.arange(n_freq, dtype=F32) / n_freq)
    ang = jnp.stack([s // GRID_W, s % GRID_W], axis=-1).astype(F32)[..., None] * inv
    cos, sin = jnp.cos(ang), jnp.sin(ang)
    zero = jnp.zeros_like(sin)
    cos_t = jnp.stack([cos, cos], axis=2).reshape(n_lat, head_dim)
    sa_t = jnp.stack([-sin, zero], axis=2).reshape(n_lat, head_dim)
    sb_t = jnp.stack([zero, sin], axis=2).reshape(n_lat, head_dim)
    pad = ((0, n_rows - n_lat), (0, 0))
    return jnp.pad(cos_t, pad, constant_values=1.0), jnp.pad(sa_t, pad), jnp.pad(sb_t, pad)


def kernel(x, c, ctx, c_ctx, w_mod, b_mod, norm_gain, conv_w_in, conv_w, conv_w_out, attn_w_qkv, attn_w_o,
           attn_q_gain, attn_k_gain, attn_lambdas, attn_subln_gain, ffn_w_gu, ffn_w_down, moe_router,
           moe_w_gu, moe_w_down):
    assert x.shape[0] == 1 and w_mod.shape[0] == 2
    S, D = x.shape[1], x.shape[2]
    C = ctx.shape[1]
    R = S + C
    head_dim = attn_q_gain.shape[-1]
    n_heads = D // (2 * head_dim)
    E = moe_router.shape[-1]
    F = ffn_w_down.shape[1]
    f_pad = _round_up(F, 512)

    cc = jnp.stack([c[0], c_ctx]).reshape(2, D, 1)
    mod = _mod_vectors(cc, w_mod, b_mod)[:, :2, :].reshape(2, 2, N_MOD, D)

    xa = jnp.concatenate([x[0], ctx[0]], axis=0)

    m0 = mod[0]
    u = _modnorm(xa, norm_gain[0, 0], m0[:, 0], m0[:, 1], S, R)
    b, cv = _conv_in(u, conv_w_in)
    bz = _conv_mix(b, cv, conv_w[0], S)
    xa = _proj_resid(bz, conv_w_out, xa, m0[:, 2], S)
    u = _modnorm(xa, norm_gain[0, 1], m0[:, 3], m0[:, 4], S, R)
    w_dn = jnp.pad(ffn_w_down[0], ((0, f_pad - F), (0, 0))).astype(BF)
    h = _ffn_gu(u, ffn_w_gu.astype(BF), f_pad)
    xa = _ffn_down(h, w_dn, xa, m0[:, 5], S)

    m1 = mod[1]
    lambda_init = 0.8 - 0.6 * math.exp(-0.3 * 1)
    u = _modnorm(xa, norm_gain[1, 0], m1[:, 0], m1[:, 1], S, R)
    cos, sa, sb = _rope_tables(S, R, head_dim)
    qk_gain = jnp.stack([attn_q_gain[0], attn_k_gain[0]]).reshape(2, 1, head_dim)
    qkv = _qkv(u, attn_w_qkv, qk_gain, cos, sa, sb, head_dim ** -0.5 * math.log2(math.e))
    o = _diff_attention(qkv, attn_lambdas[0], attn_subln_gain[0], S, n_heads, lambda_init)
    x1 = _proj_resid(o, attn_w_o, xa, m1[:, 2], S)

    tm = _pick(S, 512, BF16_SUBLANES)
    w_r = jnp.pad(moe_router[0], ((0, 0), (0, LANES - E))).astype(BF)
    comb, selm = _router(x1, norm_gain[1, 1], m1[0, 3], m1[0, 4], w_r, E)
    row_token, pos_a, pos_b, w_a, w_b, tile_expert, n_rows, n_tiles = _routing_tables(
        selm[:, :E] > 0, comb[:, :E], tm)
    us = _gather_norm(x1, row_token, n_rows, norm_gain[1, 1], m1[0, 3], m1[0, 4])
    hs = _moe_gu(us, moe_w_gu, tile_expert, n_tiles, tm)
    ys = _moe_down(hs, moe_w_down, tile_expert, n_tiles, tm)
    out = _moe_combine(x1, ys, pos_a, pos_b, w_a, w_b, m1[0, 5])
    return out[None]
```

```python
import functools
import math

import jax
import jax.numpy as jnp
from jax import lax
from jax.experimental import pallas as pl
from jax.experimental.pallas import tpu as pltpu

BF = jnp.bfloat16
F32 = jnp.float32

GRID_W = 64
ROPE_BASE = 10000.0
NORM_EPS = 1e-6
TOP_K = 2
N_MOD = 6

LANES = 128
BF16_SUBLANES = 16
VMEM_LIMIT_BYTES = 56 * 1024 * 1024


def _pick(n, target, align):
    best = None
    d = align
    while d <= min(n, target):
        if n % d == 0:
            best = d
        d += align
    return best if best is not None else n


def _round_up(n, m):
    return (n + m - 1) // m * m


def _params(*sem):
    return pltpu.CompilerParams(dimension_semantics=sem, vmem_limit_bytes=VMEM_LIMIT_BYTES)


def _sigmoid(x):
    return 1.0 / (1.0 + jnp.exp(-x))


def _mod_kernel(c_ref, w_ref, b_ref, o_ref, s_ref):
    @pl.when((pl.program_id(0) == 0) & (pl.program_id(1) == 0))
    def _():
        c = c_ref[...]
        s_ref[...] = jnp.broadcast_to(c * _sigmoid(c), s_ref.shape)

    o_ref[...] = jnp.zeros_like(o_ref)
    for g in range(w_ref.shape[2] // LANES):
        w = w_ref[0, :, g * LANES:(g + 1) * LANES]
        for r in range(2):
            o_ref[0, r:r + 1, g * LANES:(g + 1) * LANES] = (
                jnp.sum(w * s_ref[r], axis=0, keepdims=True) + b_ref[0, :, g * LANES:(g + 1) * LANES])


def _mod_vectors(cc, w_mod, b_mod):
    L, D, N = w_mod.shape
    tn = _pick(N, 512, LANES)
    return pl.pallas_call(
        _mod_kernel,
        grid=(L, N // tn),
        in_specs=[pl.BlockSpec((2, D, 1), lambda l, j: (0, 0, 0)),
                  pl.BlockSpec((1, D, tn), lambda l, j: (l, 0, j)),
                  pl.BlockSpec((1, 1, tn), lambda l, j: (l, 0, j))],
        out_specs=pl.BlockSpec((1, 8, tn), lambda l, j: (l, 0, j)),
        out_shape=jax.ShapeDtypeStruct((L, 8, N), F32),
        scratch_shapes=[pltpu.VMEM((2, D, LANES), F32)],
        compiler_params=_params("arbitrary", "arbitrary"),
        name="mod_vectors",
    )(cc, w_mod, b_mod.reshape(L, 1, N))


def _modnorm_rows(x, gain, shift, scale):
    ms = jnp.mean(x * x, axis=-1, keepdims=True)
    xn = x * lax.rsqrt(ms + NORM_EPS)
    return (xn * gain) * (1.0 + scale) + shift


def _modnorm_kernel(x_ref, g_ref, sh_ref, sc_ref, o_ref):
    o_ref[...] = _modnorm_rows(x_ref[...], g_ref[...], sh_ref[0], sc_ref[0]).astype(BF)


def _modnorm(x, gain, shift2, scale2, n_lat, n_rows):
    D = x.shape[1]
    tr = _pick(math.gcd(n_lat, n_rows), 256, BF16_SUBLANES)
    n_lat_tiles = n_lat // tr
    grp = lambda i: (jnp.where(i < n_lat_tiles, 0, 1), 0, 0)
    return pl.pallas_call(
        _modnorm_kernel,
        grid=(n_rows // tr,),
        in_specs=[pl.BlockSpec((tr, D), lambda i: (i, 0)),
                  pl.BlockSpec((1, D), lambda i: (0, 0)),
                  pl.BlockSpec((1, 1, D), grp),
                  pl.BlockSpec((1, 1, D), grp)],
        out_specs=pl.BlockSpec((tr, D), lambda i: (i, 0)),
        out_shape=jax.ShapeDtypeStruct((n_rows, D), BF),
        compiler_params=_params("arbitrary"),
        name="modnorm",
    )(x, gain.reshape(1, D), shift2.reshape(2, 1, D), scale2.reshape(2, 1, D))


def _conv_in_kernel(a_ref, wb_ref, wc_ref, wv_ref, b_ref, cv_ref, w_bf):
    @pl.when(pl.program_id(1) == 0)
    def _():
        w_bf[0] = wb_ref[0].astype(BF)
        w_bf[1] = wc_ref[0].astype(BF)
        w_bf[2] = wv_ref[0].astype(BF)

    a = a_ref[...]
    b_ref[...] = jnp.dot(a, w_bf[0], preferred_element_type=F32).astype(BF)
    c = jnp.dot(a, w_bf[1], preferred_element_type=F32)
    v = jnp.dot(a, w_bf[2], preferred_element_type=F32)
    cv_ref[...] = (c * v).astype(BF)


def _conv_in(u, w_in):
    R, D = u.shape
    tm = _pick(R, 600, BF16_SUBLANES)
    tn = _pick(D, 256, LANES)
    nj = D // tn
    wspec = lambda k: pl.BlockSpec((1, D, tn), lambda j, i: (0, 0, j + k * nj))
    return pl.pallas_call(
        _conv_in_kernel,
        grid=(nj, R // tm),
        in_specs=[pl.BlockSpec((tm, D), lambda j, i: (i, 0)), wspec(0), wspec(1), wspec(2)],
        out_specs=[pl.BlockSpec((tm, tn), lambda j, i: (i, j)),
                   pl.BlockSpec((tm, tn), lambda j, i: (i, j))],
        out_shape=[jax.ShapeDtypeStruct((R, D), BF), jax.ShapeDtypeStruct((R, D), BF)],
        scratch_shapes=[pltpu.VMEM((3, D, tn), BF)],
        compiler_params=_params("arbitrary", "arbitrary"),
        name="conv_in",
    )(u, w_in, w_in, w_in)


def _conv_mix_kernel(n_lat_tiles, b_ref, cv_ref, pv_ref, nx_ref, w_ref, o_ref):
    i = pl.program_id(0)
    tr = cv_ref.shape[0]
    cv = cv_ref[...].astype(F32)
    top_zero = (i == 0) | (i == n_lat_tiles)
    bot_zero = (i == n_lat_tiles - 1) | (i == pl.num_programs(0) - 1)
    hp = pv_ref[BF16_SUBLANES - 1:BF16_SUBLANES, :].astype(F32) * jnp.where(top_zero, 0.0, 1.0)
    hn = nx_ref[0:1, :].astype(F32) * jnp.where(bot_zero, 0.0, 1.0)
    rows = lax.broadcasted_iota(jnp.int32, cv.shape, 0)
    prev = jnp.where(rows == 0, hp, pltpu.roll(cv, 1, 0))
    nxt = jnp.where(rows == tr - 1, hn, pltpu.roll(cv, tr - 1, 0))
    w = w_ref[...]
    z = prev * w[0:1] + cv * w[1:2] + nxt * w[2:3]
    o_ref[...] = (b_ref[...].astype(F32) * z).astype(BF)


def _conv_mix(b, cv, conv_w, n_lat):
    R, D = cv.shape
    tr = _pick(math.gcd(n_lat, R), 256, BF16_SUBLANES)
    tc = _pick(D, 1024, LANES)
    hb = tr // BF16_SUBLANES
    last = R // BF16_SUBLANES - 1
    return pl.pallas_call(
        functools.partial(_conv_mix_kernel, n_lat // tr),
        grid=(R // tr, D // tc),
        in_specs=[pl.BlockSpec((tr, tc), lambda i, j: (i, j)),
                  pl.BlockSpec((tr, tc), lambda i, j: (i, j)),
                  pl.BlockSpec((BF16_SUBLANES, tc), lambda i, j: (jnp.maximum(i * hb - 1, 0), j)),
                  pl.BlockSpec((BF16_SUBLANES, tc), lambda i, j: (jnp.minimum((i + 1) * hb, last), j)),
                  pl.BlockSpec((3, tc), lambda i, j: (0, j))],
        out_specs=pl.BlockSpec((tr, tc), lambda i, j: (i, j)),
        out_shape=jax.ShapeDtypeStruct((R, D), BF),
        compiler_params=_params("arbitrary", "arbitrary"),
        name="conv_mix",
    )(b, cv, cv, cv, conv_w)


def _row_gate(g_ref, row0, shape, n_lat):
    rows = row0 + lax.broadcasted_iota(jnp.int32, shape, 0)
    return jnp.where(rows < n_lat, g_ref[0:1, :], g_ref[1:2, :])


def _proj_resid_kernel(n_lat, a_ref, w_ref, r_ref, g_ref, o_ref, w_bf):
    @pl.when(pl.program_id(1) == 0)
    def _():
        w_bf[...] = w_ref[0].astype(BF)

    tm = o_ref.shape[0]
    acc = jnp.dot(a_ref[...], w_bf[...], preferred_element_type=F32)
    g = _row_gate(g_ref, pl.program_id(1) * tm, acc.shape, n_lat)
    o_ref[...] = r_ref[...] + g * acc


def _proj_resid(a, w, resid, gate2, n_lat):
    M, K = a.shape
    N = w.shape[2]
    tm = _pick(M, 1100, BF16_SUBLANES)
    tn = _pick(N, 512, LANES)
    return pl.pallas_call(
        functools.partial(_proj_resid_kernel, n_lat),
        grid=(N // tn, M // tm),
        in_specs=[pl.BlockSpec((tm, K), lambda j, i: (i, 0)),
                  pl.BlockSpec((1, K, tn), lambda j, i: (0, 0, j)),
                  pl.BlockSpec((tm, tn), lambda j, i: (i, j)),
                  pl.BlockSpec((2, tn), lambda j, i: (0, j))],
        out_specs=pl.BlockSpec((tm, tn), lambda j, i: (i, j)),
        out_shape=jax.ShapeDtypeStruct((M, N), F32),
        scratch_shapes=[pltpu.VMEM((K, tn), BF)],
        compiler_params=_params("arbitrary", "arbitrary"),
        name="proj_resid",
    )(a, w, resid, gate2)


def _gu_kernel(n_real, a_ref, wg_ref, wu_ref, o_ref):
    j = pl.program_id(1)

    @pl.when(j < n_real)
    def _():
        a = a_ref[...]
        g = jnp.dot(a, wg_ref[0], preferred_element_type=F32)
        u = jnp.dot(a, wu_ref[0], preferred_element_type=F32)
        o_ref[...] = ((g * _sigmoid(g)) * u).astype(BF)

    @pl.when(j >= n_real)
    def _():
        o_ref[...] = jnp.zeros_like(o_ref)


def _ffn_gu(u, w_gu, f_pad):
    R, D = u.shape
    F = w_gu.shape[2] // 2
    tm = _pick(R, 2200, BF16_SUBLANES)
    tn = _pick(math.gcd(F, f_pad), 256, LANES)
    nf = F // tn
    col = lambda j: jnp.minimum(j, nf - 1)
    return pl.pallas_call(
        functools.partial(_gu_kernel, nf),
        grid=(R // tm, f_pad // tn),
        in_specs=[pl.BlockSpec((tm, D), lambda i, j: (i, 0)),
                  pl.BlockSpec((1, D, tn), lambda i, j: (0, 0, col(j))),
                  pl.BlockSpec((1, D, tn), lambda i, j: (0, 0, col(j) + nf))],
        out_specs=pl.BlockSpec((tm, tn), lambda i, j: (i, j)),
        out_shape=jax.ShapeDtypeStruct((R, f_pad), BF),
        compiler_params=_params("arbitrary", "arbitrary"),
        name="ffn_gu",
    )(u, w_gu, w_gu)


def _down_kernel(n_lat, a_ref, w_ref, r_ref, g_ref, o_ref, acc_ref):
    k = pl.program_id(2)

    @pl.when(k == 0)
    def _():
        acc_ref[...] = jnp.zeros_like(acc_ref)

    acc_ref[...] += jnp.dot(a_ref[...], w_ref[...], preferred_element_type=F32)

    @pl.when(k == pl.num_programs(2) - 1)
    def _():
        acc = acc_ref[...]
        g = _row_gate(g_ref, pl.program_id(0) * acc.shape[0], acc.shape, n_lat)
        o_ref[...] = r_ref[...] + g * acc


def _ffn_down(h, w_down, resid, gate2, n_lat):
    R, Fp = h.shape
    D = w_down.shape[1]
    tm = _pick(R, 1100, BF16_SUBLANES)
    tn = _pick(D, 1024, LANES)
    tk = _pick(Fp, 2816, LANES)
    return pl.pallas_call(
        functools.partial(_down_kernel, n_lat),
        grid=(R // tm, D // tn, Fp // tk),
        in_specs=[pl.BlockSpec((tm, tk), lambda i, j, k: (i, k)),
                  pl.BlockSpec((tk, tn), lambda i, j, k: (k, j)),
                  pl.BlockSpec((tm, tn), lambda i, j, k: (i, j)),
                  pl.BlockSpec((2, tn), lambda i, j, k: (0, j))],
        out_specs=pl.BlockSpec((tm, tn), lambda i, j, k: (i, j)),
        out_shape=jax.ShapeDtypeStruct((R, D), F32),
        scratch_shapes=[pltpu.VMEM((tm, tn), F32)],
        compiler_params=_params("arbitrary", "arbitrary", "arbitrary"),
        name="ffn_down",
    )(h, w_down, resid, gate2)


def _qkv_kernel(n_qk_tiles, q_scale, a_ref, w_ref, gain_ref, cos_ref, sa_ref, sb_ref, o_ref, w_bf):
    j = pl.program_id(0)

    @pl.when(pl.program_id(1) == 0)
    def _():
        w_bf[...] = w_ref[0].astype(BF)

    acc = jnp.dot(a_ref[...], w_bf[...], preferred_element_type=F32)

    @pl.when(j >= 2 * n_qk_tiles)
    def _():
        o_ref[...] = acc.astype(BF)

    @pl.when(j < 2 * n_qk_tiles)
    def _():
        gain = gain_ref[0]
        cos, sa, sb = cos_ref[...], sa_ref[...], sb_ref[...]
        post = jnp.where(j < n_qk_tiles, q_scale, 1.0)
        half = LANES // 4
        for g in range(acc.shape[1] // LANES):
            t = acc[:, g * LANES:(g + 1) * LANES]
            t = t * lax.rsqrt(jnp.mean(t * t, axis=-1, keepdims=True) + NORM_EPS)
            t = t * gain
            t = t * cos + pltpu.roll(t, LANES - half, 1) * sa + pltpu.roll(t, half, 1) * sb
            o_ref[:, g * LANES:(g + 1) * LANES] = (t * post).astype(BF)


def _qkv(u, w_qkv, qk_gain2, cos, sa, sb, q_scale):
    R, D = u.shape
    tm = _pick(R, 1100, BF16_SUBLANES)
    tn = _pick(D, 512, LANES)
    nq = D // tn
    tab = pl.BlockSpec((tm, LANES), lambda j, i: (i, 0))
    return pl.pallas_call(
        functools.partial(_qkv_kernel, nq, q_scale),
        grid=(3 * nq, R // tm),
        in_specs=[pl.BlockSpec((tm, D), lambda j, i: (i, 0)),
                  pl.BlockSpec((1, D, tn), lambda j, i: (0, 0, j)),
                  pl.BlockSpec((1, 1, LANES), lambda j, i: (jnp.where(j < nq, 0, 1), 0, 0)),
                  tab, tab, tab],
        out_specs=pl.BlockSpec((tm, tn), lambda j, i: (i, j)),
        out_shape=jax.ShapeDtypeStruct((R, 3 * D), BF),
        scratch_shapes=[pltpu.VMEM((D, tn), BF)],
        compiler_params=_params("arbitrary", "arbitrary"),
        name="qkv_proj",
    )(u, w_qkv, qk_gain2, cos, sa, sb)


def _attn_kernel(n_lat, tk, lambda_init, q_ref, k_ref, v_ref, lam_ref, sg_ref, o_ref,
                 sa_ref, sb_ref, sc_ref, m_ref, l_ref, acc_ref):
    hd = q_ref.shape[1] // 2
    n_kv = k_ref.shape[0]
    n_chunks = n_lat // tk
    has_ctx = n_kv > n_lat
    m_ref[...] = jnp.full_like(m_ref, -jnp.inf)
    l_ref[...] = jnp.zeros_like(l_ref)
    acc_ref[...] = jnp.zeros_like(acc_ref)

    def scores(kc, dst):
        for m in range(2):
            dst[m] = lax.dot_general(q_ref[:, m * hd:(m + 1) * hd], kc[:, m * hd:(m + 1) * hd],
                                     (((1,), (1,)), ((), ())), preferred_element_type=F32)

    def absorb(src, vc):
        scaled = []
        for m in range(2):
            s = src[m]
            m_old = m_ref[m]
            m_new = jnp.maximum(m_old, jnp.max(s, axis=-1, keepdims=True))
            alpha = jnp.exp2(m_old - m_new)
            p = jnp.exp2(s - m_new)
            l_ref[m] = alpha * l_ref[m] + jnp.sum(p, axis=-1, keepdims=True)
            m_ref[m] = m_new
            scaled.append((alpha, p.astype(BF)))
        for m in range(2):
            alpha, p = scaled[m]
            acc_ref[m] = alpha * acc_ref[m] + jnp.dot(p, vc, preferred_element_type=F32)

    def k_chunk(c):
        return k_ref[pl.ds(pl.multiple_of(c * tk, tk), tk), :]

    def v_chunk(c):
        return v_ref[pl.ds(pl.multiple_of(c * tk, tk), tk), :]

    scores(k_chunk(0), sa_ref)

    def pair(i, carry):
        c = 2 * i
        scores(k_chunk(c + 1), sb_ref)
        absorb(sa_ref, v_chunk(c))
        scores(k_chunk(c + 2), sa_ref)
        absorb(sb_ref, v_chunk(c + 1))
        return carry

    n_pairs = (n_chunks - 1) // 2
    lax.fori_loop(0, n_pairs, pair, 0)
    c = 2 * n_pairs
    if n_chunks - c == 2:
        scores(k_chunk(c + 1), sb_ref)
        absorb(sa_ref, v_chunk(c))
        if has_ctx:
            scores(k_ref[n_lat:n_kv, :], sc_ref)
        absorb(sb_ref, v_chunk(c + 1))
    else:
        if has_ctx:
            scores(k_ref[n_lat:n_kv, :], sc_ref)
        absorb(sa_ref, v_chunk(c))
    if has_ctx:
        absorb(sc_ref, v_ref[n_lat:n_kv, :])

    lf = lam_ref[...]
    lam = (jnp.exp(jnp.sum(lf[0:1] * lf[1:2], axis=-1, keepdims=True))
           - jnp.exp(jnp.sum(lf[2:3] * lf[3:4], axis=-1, keepdims=True)) + lambda_init)
    o = acc_ref[0] / l_ref[0] - lam * (acc_ref[1] / l_ref[1])
    o = o * lax.rsqrt(jnp.mean(o * o, axis=-1, keepdims=True) + NORM_EPS)
    o_ref[...] = ((o * sg_ref[...]) * (1.0 - lambda_init)).astype(BF)


def _diff_attention(qkv, lambdas, subln_gain, n_lat, n_heads, lambda_init):
    R = qkv.shape[0]
    D = qkv.shape[1] // 3
    vd = D // n_heads
    tq = _pick(n_lat, 512, BF16_SUBLANES)
    tk = _pick(n_lat, 1024, LANES)
    return pl.pallas_call(
        functools.partial(_attn_kernel, n_lat, tk, lambda_init),
        grid=(n_heads, n_lat // tq),
        in_specs=[pl.BlockSpec((tq, vd), lambda h, i: (i, h)),
                  pl.BlockSpec((R, vd), lambda h, i: (0, n_heads + h)),
                  pl.BlockSpec((R, vd), lambda h, i: (0, 2 * n_heads + h)),
                  pl.BlockSpec(lambdas.shape, lambda h, i: (0, 0)),
                  pl.BlockSpec((1, vd), lambda h, i: (0, 0))],
        out_specs=pl.BlockSpec((tq, vd), lambda h, i: (i, h)),
        out_shape=jax.ShapeDtypeStruct((n_lat, D), BF),
        scratch_shapes=[pltpu.VMEM((2, tq, tk), F32), pltpu.VMEM((2, tq, tk), F32),
                        pltpu.VMEM((2, tq, (R - n_lat) or LANES), F32),
                        pltpu.VMEM((2, tq, 1), F32), pltpu.VMEM((2, tq, 1), F32),
                        pltpu.VMEM((2, tq, vd), F32)],
        compiler_params=_params("arbitrary", "arbitrary"),
        name="diff_attention",
    )(qkv, qkv, qkv, lambdas, subln_gain.reshape(1, vd))


def _router_kernel(n_experts, x_ref, g_ref, sh_ref, sc_ref, w_ref, comb_ref, sel_ref):
    u = _modnorm_rows(x_ref[...], g_ref[...], sh_ref[...], sc_ref[...]).astype(BF)
    logits = jnp.dot(u, w_ref[...], preferred_element_type=F32)
    lane = lax.broadcasted_iota(jnp.int32, logits.shape, 1)
    neg = jnp.float32(-jnp.inf)
    l1 = jnp.where(lane < n_experts, logits, neg)
    m1 = jnp.max(l1, axis=-1, keepdims=True)
    i1 = jnp.min(jnp.where(l1 == m1, lane, LANES), axis=-1, keepdims=True)
    sel1 = lane == i1
    l2 = jnp.where(sel1, neg, l1)
    m2 = jnp.max(l2, axis=-1, keepdims=True)
    i2 = jnp.min(jnp.where(l2 == m2, lane, LANES), axis=-1, keepdims=True)
    sel2 = lane == i2
    e2 = jnp.exp(m2 - m1)
    den = 1.0 + e2
    comb_ref[...] = jnp.where(sel1, 1.0 / den, 0.0) + jnp.where(sel2, e2 / den, 0.0)
    sel_ref[...] = (sel1 | sel2).astype(jnp.int32)


def _router(x, gain, shift, scale, w_router_pad, n_experts):
    S, D = x.shape
    tm = _pick(S, 512, 8)
    vec = pl.BlockSpec((1, D), lambda i: (0, 0))
    out = pl.BlockSpec((tm, LANES), lambda i: (i, 0))
    return pl.pallas_call(
        functools.partial(_router_kernel, n_experts),
        grid=(S // tm,),
        in_specs=[pl.BlockSpec((tm, D), lambda i: (i, 0)), vec, vec, vec,
                  pl.BlockSpec((D, LANES), lambda i: (0, 0))],
        out_specs=[out, out],
        out_shape=[jax.ShapeDtypeStruct((S, LANES), F32), jax.ShapeDtypeStruct((S, LANES), jnp.int32)],
        compiler_params=_params("arbitrary"),
        name="moe_router",
    )(x, gain.reshape(1, D), shift.reshape(1, D), scale.reshape(1, D), w_router_pad)


def _row_copy(src_hbm, row, dst, r, sem):
    return pltpu.make_async_copy(src_hbm.at[pl.ds(row, 1)], dst.at[pl.ds(r, 1)], sem)


ROW_DMA_UNROLL = 8
ROW_CHUNK = BF16_SUBLANES


def _gather_norm_kernel(tok_ref, nrows_ref, x_hbm, g_ref, sh_ref, sc_ref, o_ref, buf, sem):
    tg = buf.shape[1]
    t = pl.program_id(0)

    def used(tile):
        return tile * tg < nrows_ref[0]

    def issue(tile):
        slot = tile % 2

        def one(r, c):
            _row_copy(x_hbm, tok_ref[tile * tg + r], buf.at[slot], r, sem.at[slot]).start()
            return c

        lax.fori_loop(0, tg, one, 0, unroll=ROW_DMA_UNROLL)

    def wait(tile):
        slot = tile % 2

        def one(r, c):
            _row_copy(x_hbm, 0, buf.at[slot], r, sem.at[slot]).wait()
            return c

        lax.fori_loop(0, tg, one, 0, unroll=ROW_DMA_UNROLL)

    @pl.when((t == 0) & used(0))
    def _():
        issue(0)

    @pl.when((t + 1 < pl.num_programs(0)) & used(t + 1))
    def _():
        issue(t + 1)

    @pl.when(used(t))
    def _():
        wait(t)
        slot = t % 2

        def chunk(c, carry):
            rows = pl.ds(pl.multiple_of(c * ROW_CHUNK, ROW_CHUNK), ROW_CHUNK)
            o_ref[rows, :] = _modnorm_rows(buf[slot, rows, :], g_ref[...], sh_ref[...], sc_ref[...]).astype(BF)
            return carry

        lax.fori_loop(0, tg // ROW_CHUNK, chunk, 0)

    @pl.when(jnp.logical_not(used(t)))
    def _():
        o_ref[...] = jnp.zeros_like(o_ref)


def _gather_norm(x, row_token, n_rows_used, gain, shift, scale):
    D = x.shape[1]
    Rs = row_token.shape[0]
    tg = _pick(Rs, 256, BF16_SUBLANES)
    vec = pl.BlockSpec((1, D), lambda t, tok, n: (0, 0))
    return pl.pallas_call(
        _gather_norm_kernel,
        grid_spec=pltpu.PrefetchScalarGridSpec(
            num_scalar_prefetch=2,
            grid=(Rs // tg,),
            in_specs=[pl.BlockSpec(memory_space=pl.ANY), vec, vec, vec],
            out_specs=pl.BlockSpec((tg, D), lambda t, tok, n: (t, 0)),
            scratch_shapes=[pltpu.VMEM((2, tg, D), F32), pltpu.SemaphoreType.DMA((2,))]),
        out_shape=jax.ShapeDtypeStruct((Rs, D), BF),
        compiler_params=_params("arbitrary"),
        name="moe_gather_norm",
    )(row_token, n_rows_used, x, gain.reshape(1, D), shift.reshape(1, D), scale.reshape(1, D))


SCHED_EXPERT, SCHED_RUN_START, SCHED_FIRST_RUN, SCHED_NEXT_EXPERT, SCHED_LAST_RUN = range(5)


def _staged_expert_weights(sched_ref, cnt_ref, w_hbm, col_blocks, wstage, w_bf, sem):
    j, t = pl.program_id(0), pl.program_id(1)
    nj, n_t = pl.num_programs(0), pl.num_programs(1)
    tn = wstage.shape[-1]

    def copies(e, jj):
        return [pltpu.make_async_copy(w_hbm.at[0, e, :, pl.ds(pl.multiple_of(cb * tn, tn), tn)],
                                      wstage.at[k], sem.at[k])
                for k, cb in enumerate(col_blocks(jj))]

    def sched(row):
        return sched_ref[row * n_t + t]

    @pl.when(sched(SCHED_RUN_START) == 1)
    def _():
        e = sched(SCHED_EXPERT)

        @pl.when((j == 0) & (sched(SCHED_FIRST_RUN) == 1))
        def _():
            for cp in copies(e, j):
                cp.start()

        for k, cp in enumerate(copies(e, j)):
            cp.wait()
            w_bf[k] = wstage[k].astype(BF)

        next_j = j + sched(SCHED_LAST_RUN)

        @pl.when(next_j < nj)
        def _():
            for cp in copies(sched(SCHED_NEXT_EXPERT), next_j):
                cp.start()


def _moe_gu_kernel(sched_ref, cnt_ref, a_ref, w_hbm, o_ref, wstage, w_bf, sem):
    t = pl.program_id(1)
    nj = pl.num_programs(0)

    @pl.when(t < cnt_ref[0])
    def _():
        _staged_expert_weights(sched_ref, cnt_ref, w_hbm, lambda jj: (jj, jj + nj), wstage, w_bf, sem)
        a = a_ref[...]
        g = jnp.dot(a, w_bf[0], preferred_element_type=F32)
        u = jnp.dot(a, w_bf[1], preferred_element_type=F32)
        o_ref[...] = ((g * _sigmoid(g)) * u).astype(BF)

    @pl.when(t >= cnt_ref[0])
    def _():
        o_ref[...] = jnp.zeros_like(o_ref)


def _moe_gu(a, w_gu, sched, counts, tm):
    Rs, D = a.shape
    F = w_gu.shape[3] // 2
    tn = _pick(F, 512, LANES)
    tile = lambda t, cnt: jnp.minimum(t, cnt[0] - 1)
    return pl.pallas_call(
        _moe_gu_kernel,
        grid_spec=pltpu.PrefetchScalarGridSpec(
            num_scalar_prefetch=2,
            grid=(F // tn, Rs // tm),
            in_specs=[pl.BlockSpec((tm, D), lambda j, t, sc, cnt: (tile(t, cnt), 0)),
                      pl.BlockSpec(memory_space=pl.ANY)],
            out_specs=pl.BlockSpec((tm, tn), lambda j, t, sc, cnt: (t, j)),
            scratch_shapes=[pltpu.VMEM((2, D, tn), F32), pltpu.VMEM((2, D, tn), BF),
                            pltpu.SemaphoreType.DMA((2,))]),
        out_shape=jax.ShapeDtypeStruct((Rs, F), BF),
        compiler_params=_params("arbitrary", "arbitrary"),
        name="moe_gu",
    )(sched, counts, a, w_gu)


def _moe_down_kernel(sched_ref, cnt_ref, a_ref, w_hbm, o_ref, wstage, w_bf, sem):
    t = pl.program_id(1)

    @pl.when(t < cnt_ref[0])
    def _():
        _staged_expert_weights(sched_ref, cnt_ref, w_hbm, lambda jj: (jj,), wstage, w_bf, sem)
        o_ref[...] = jnp.dot(a_ref[...], w_bf[0], preferred_element_type=F32)

    @pl.when(t >= cnt_ref[0])
    def _():
        o_ref[...] = jnp.zeros_like(o_ref)


def _moe_down(h, w_down, sched, counts, tm):
    Rs, F = h.shape
    D = w_down.shape[3]
    tn = _pick(D, 512, LANES)
    tile = lambda t, cnt: jnp.minimum(t, cnt[0] - 1)
    return pl.pallas_call(
        _moe_down_kernel,
        grid_spec=pltpu.PrefetchScalarGridSpec(
            num_scalar_prefetch=2,
            grid=(D // tn, Rs // tm),
            in_specs=[pl.BlockSpec((tm, F), lambda j, t, sc, cnt: (tile(t, cnt), 0)),
                      pl.BlockSpec(memory_space=pl.ANY)],
            out_specs=pl.BlockSpec((tm, tn), lambda j, t, sc, cnt: (t, j)),
            scratch_shapes=[pltpu.VMEM((1, F, tn), F32), pltpu.VMEM((1, F, tn), BF),
                            pltpu.SemaphoreType.DMA((1,))]),
        out_shape=jax.ShapeDtypeStruct((Rs, D), F32),
        compiler_params=_params("arbitrary", "arbitrary"),
        name="moe_down",
    )(sched, counts, h, w_down)


def _combine_kernel(pa_ref, pb_ref, y_hbm, x_ref, g_ref, wa_ref, wb_ref, o_ref, ya, yb, sem):
    tt = ya.shape[1]
    t = pl.program_id(0)

    def issue(tile):
        slot = tile % 2

        def one(r, c):
            _row_copy(y_hbm, pa_ref[tile * tt + r], ya.at[slot], r, sem.at[0, slot]).start()
            _row_copy(y_hbm, pb_ref[tile * tt + r], yb.at[slot], r, sem.at[1, slot]).start()
            return c

        lax.fori_loop(0, tt, one, 0, unroll=ROW_DMA_UNROLL)

    def wait(tile):
        slot = tile % 2

        def one(r, c):
            _row_copy(y_hbm, 0, ya.at[slot], r, sem.at[0, slot]).wait()
            _row_copy(y_hbm, 0, yb.at[slot], r, sem.at[1, slot]).wait()
            return c

        lax.fori_loop(0, tt, one, 0, unroll=ROW_DMA_UNROLL)

    @pl.when(t == 0)
    def _():
        issue(0)

    @pl.when(t + 1 < pl.num_programs(0))
    def _():
        issue(t + 1)

    wait(t)
    slot = t % 2

    def chunk(c, carry):
        rows = pl.ds(pl.multiple_of(c * ROW_CHUNK, ROW_CHUNK), ROW_CHUNK)
        mix = wa_ref[rows, :] * ya[slot, rows, :] + wb_ref[rows, :] * yb[slot, rows, :]
        o_ref[rows, :] = x_ref[rows, :] + g_ref[...] * mix
        return carry

    lax.fori_loop(0, tt // ROW_CHUNK, chunk, 0)


def _moe_combine(x, y, pos_a, pos_b, w_a, w_b, gate):
    S, D = x.shape
    tt = _pick(S, 256, 8)
    col = pl.BlockSpec((tt, 1), lambda t, pa, pb: (t, 0))
    return pl.pallas_call(
        _combine_kernel,
        grid_spec=pltpu.PrefetchScalarGridSpec(
            num_scalar_prefetch=2,
            grid=(S // tt,),
            in_specs=[pl.BlockSpec(memory_space=pl.ANY),
                      pl.BlockSpec((tt, D), lambda t, pa, pb: (t, 0)),
                      pl.BlockSpec((1, D), lambda t, pa, pb: (0, 0)),
                      col, col],
            out_specs=pl.BlockSpec((tt, D), lambda t, pa, pb: (t, 0)),
            scratch_shapes=[pltpu.VMEM((2, tt, D), F32), pltpu.VMEM((2, tt, D), F32),
                            pltpu.SemaphoreType.DMA((2, 2))]),
        out_shape=jax.ShapeDtypeStruct((S, D), F32),
        compiler_params=_params("arbitrary"),
        name="moe_combine",
    )(pos_a, pos_b, y, x, gate.reshape(1, D), w_a, w_b)


def _routing_tables(sel, comb, tm):
    S, E = sel.shape
    Rs = TOP_K * S + E * tm
    seli = sel.astype(jnp.int32)
    cnt = jnp.sum(seli, axis=0)
    padded = (cnt + tm - 1) // tm * tm
    gend = jnp.cumsum(padded)
    gstart = gend - padded
    pos = gstart[None, :] + jnp.cumsum(seli, axis=0) - seli
    pos_a = jnp.min(jnp.where(sel, pos, Rs), axis=1).astype(jnp.int32)
    pos_b = jnp.max(jnp.where(sel, pos, -1), axis=1).astype(jnp.int32)
    w_a = jnp.sum(jnp.where(sel & (pos == pos_a[:, None]), comb, 0.0), axis=1, keepdims=True)
    w_b = jnp.sum(jnp.where(sel & (pos == pos_b[:, None]), comb, 0.0), axis=1, keepdims=True)
    tok = jnp.arange(S, dtype=jnp.int32)
    row_token = jnp.zeros((Rs,), jnp.int32).at[jnp.concatenate([pos_a, pos_b])].set(
        jnp.concatenate([tok, tok]), mode="drop")
    n_t = Rs // tm
    tix = jnp.arange(n_t, dtype=jnp.int32)
    tile_expert = jnp.minimum(
        jnp.sum((tix[:, None] * tm >= gend[None, :]).astype(jnp.int32), axis=1), E - 1).astype(jnp.int32)
    n_rows = gend[-1].astype(jnp.int32)
    n_tiles = n_rows // tm
    prev_expert = jnp.concatenate([jnp.full((1,), -1, jnp.int32), tile_expert[:-1]])
    first = (tix < n_tiles) & (tile_expert != prev_expert)
    first_run = first & (tix == 0)
    later_first = first[None, :] & (tix[None, :] > tix[:, None])
    next_first = jnp.min(jnp.where(later_first, tix[None, :], n_t), axis=1)
    last_run = next_first >= n_t
    next_expert = jnp.where(last_run, tile_expert[0], jnp.sum(
        jnp.where(tix[None, :] == next_first[:, None], tile_expert[None, :], 0), axis=1))
    sched = jnp.concatenate([tile_expert, first.astype(jnp.int32), first_run.astype(jnp.int32), next_expert,
                             last_run.astype(jnp.int32)]).astype(jnp.int32)
    return row_token, pos_a, pos_b, w_a, w_b, sched, n_tiles.reshape(1), n_rows.reshape(1)


def _rope_tables(n_lat, n_rows, head_dim):
    n_freq = head_dim // 4
    s = jnp.arange(n_lat, dtype=jnp.int32)
    inv = ROPE_BASE ** (-jnp.arange(n_freq, dtype=F32) / n_freq)
    ang = jnp.stack([s // GRID_W, s % GRID_W], axis=-1).astype(F32)[..., None] * inv
    cos, sin = jnp.cos(ang), jnp.sin(ang)
    zero = jnp.zeros_like(sin)
    cos_t = jnp.stack([cos, cos], axis=2).reshape(n_lat, head_dim)
    sa_t = jnp.stack([-sin, zero], axis=2).reshape(n_lat, head_dim)
    sb_t = jnp.stack([zero, sin], axis=2).reshape(n_lat, head_dim)
    pad = ((0, n_rows - n_lat), (0, 0))
    return jnp.pad(cos_t, pad, constant_values=1.0), jnp.pad(sa_t, pad), jnp.pad(sb_t, pad)


def kernel(x, c, ctx, c_ctx, w_mod, b_mod, norm_gain, conv_w_in, conv_w, conv_w_out, attn_w_qkv, attn_w_o,
           attn_q_gain, attn_k_gain, attn_lambdas, attn_subln_gain, ffn_w_gu, ffn_w_down, moe_router,
           moe_w_gu, moe_w_down):
    assert x.shape[0] == 1 and w_mod.shape[0] == 2
    S, D = x.shape[1], x.shape[2]
    C = ctx.shape[1]
    R = S + C
    head_dim = attn_q_gain.shape[-1]
    n_heads = D // (2 * head_dim)
    E = moe_router.shape[-1]
    F = ffn_w_down.shape[1]
    f_pad = _round_up(F, 512)

    cc = jnp.stack([c[0], c_ctx]).reshape(2, D, 1)
    mod = _mod_vectors(cc, w_mod, b_mod)[:, :2, :].reshape(2, 2, N_MOD, D)

    xa = jnp.concatenate([x[0], ctx[0]], axis=0)

    m0 = mod[0]
    u = _modnorm(xa, norm_gain[0, 0], m0[:, 0], m0[:, 1], S, R)
    b, cv = _conv_in(u, conv_w_in)
    bz = _conv_mix(b, cv, conv_w[0], S)
    xa = _proj_resid(bz, conv_w_out, xa, m0[:, 2], S)
    u = _modnorm(xa, norm_gain[0, 1], m0[:, 3], m0[:, 4], S, R)
    w_dn = jnp.pad(ffn_w_down[0], ((0, f_pad - F), (0, 0))).astype(BF)
    h = _ffn_gu(u, ffn_w_gu.astype(BF), f_pad)
    xa = _ffn_down(h, w_dn, xa, m0[:, 5], S)

    m1 = mod[1]
    lambda_init = 0.8 - 0.6 * math.exp(-0.3 * 1)
    u = _modnorm(xa, norm_gain[1, 0], m1[:, 0], m1[:, 1], S, R)
    cos, sa, sb = _rope_tables(S, R, head_dim)
    qk_gain = jnp.stack([attn_q_gain[0], attn_k_gain[0]]).reshape(2, 1, head_dim)
    qkv = _qkv(u, attn_w_qkv, qk_gain, cos, sa, sb, head_dim ** -0.5 * math.log2(math.e))
    o = _diff_attention(qkv, attn_lambdas[0], attn_subln_gain[0], S, n_heads, lambda_init)
    x1 = _proj_resid(o, attn_w_o, xa, m1[:, 2], S)

    tm = _pick(S, 512, BF16_SUBLANES)
    w_r = jnp.pad(moe_router[0], ((0, 0), (0, LANES - E))).astype(BF)
    comb, selm = _router(x1, norm_gain[1, 1], m1[0, 3], m1[0, 4], w_r, E)
    row_token, pos_a, pos_b, w_a, w_b, sched, counts, n_rows = _routing_tables(
        selm[:, :E] > 0, comb[:, :E], tm)
    us = _gather_norm(x1, row_token, n_rows, norm_gain[1, 1], m1[0, 3], m1[0, 4])
    hs = _moe_gu(us, moe_w_gu, sched, counts, tm)
    ys = _moe_down(hs, moe_w_down, sched, counts, tm)
    out = _moe_combine(x1, ys, pos_a, pos_b, w_a, w_b, m1[0, 5])
    return out[None]
```

```python
import functools
import math

import jax
import jax.numpy as jnp
from jax import lax
from jax.experimental import pallas as pl
from jax.experimental.pallas import tpu as pltpu

BF = jnp.bfloat16
F32 = jnp.float32

GRID_W = 64
ROPE_BASE = 10000.0
NORM_EPS = 1e-6
TOP_K = 2
N_MOD = 6

LANES = 128
BF16_SUBLANES = 16
VMEM_LIMIT_BYTES = 56 * 1024 * 1024


def _pick(n, target, align):
    best = None
    d = align
    while d <= min(n, target):
        if n % d == 0:
            best = d
        d += align
    return best if best is not None else n


def _round_up(n, m):
    return (n + m - 1) // m * m


def _params(*sem):
    return pltpu.CompilerParams(dimension_semantics=sem, vmem_limit_bytes=VMEM_LIMIT_BYTES)


def _sigmoid(x):
    return 1.0 / (1.0 + jnp.exp(-x))


def _mod_kernel(c_ref, w_ref, b_ref, o_ref, s_ref):
    @pl.when((pl.program_id(0) == 0) & (pl.program_id(1) == 0))
    def _():
        c = c_ref[...]
        s_ref[...] = jnp.broadcast_to(c * _sigmoid(c), s_ref.shape)

    o_ref[...] = jnp.zeros_like(o_ref)
    for g in range(w_ref.shape[2] // LANES):
        w = w_ref[0, :, g * LANES:(g + 1) * LANES]
        for r in range(2):
            o_ref[0, r:r + 1, g * LANES:(g + 1) * LANES] = (
                jnp.sum(w * s_ref[r], axis=0, keepdims=True) + b_ref[0, :, g * LANES:(g + 1) * LANES])


def _mod_vectors(cc, w_mod, b_mod):
    L, D, N = w_mod.shape
    tn = _pick(N, 512, LANES)
    return pl.pallas_call(
        _mod_kernel,
        grid=(L, N // tn),
        in_specs=[pl.BlockSpec((2, D, 1), lambda l, j: (0, 0, 0)),
                  pl.BlockSpec((1, D, tn), lambda l, j: (l, 0, j)),
                  pl.BlockSpec((1, 1, tn), lambda l, j: (l, 0, j))],
        out_specs=pl.BlockSpec((1, 8, tn), lambda l, j: (l, 0, j)),
        out_shape=jax.ShapeDtypeStruct((L, 8, N), F32),
        scratch_shapes=[pltpu.VMEM((2, D, LANES), F32)],
        compiler_params=_params("arbitrary", "arbitrary"),
        name="mod_vectors",
    )(cc, w_mod, b_mod.reshape(L, 1, N))


def _modnorm_rows(x, gain, shift, scale):
    ms = jnp.mean(x * x, axis=-1, keepdims=True)
    xn = x * lax.rsqrt(ms + NORM_EPS)
    return (xn * gain) * (1.0 + scale) + shift


def _modnorm_kernel(x_ref, g_ref, sh_ref, sc_ref, o_ref):
    o_ref[...] = _modnorm_rows(x_ref[...], g_ref[...], sh_ref[0], sc_ref[0]).astype(BF)


def _modnorm(x, gain, shift2, scale2, n_lat, n_rows):
    D = x.shape[1]
    tr = _pick(math.gcd(n_lat, n_rows), 256, BF16_SUBLANES)
    n_lat_tiles = n_lat // tr
    grp = lambda i: (jnp.where(i < n_lat_tiles, 0, 1), 0, 0)
    return pl.pallas_call(
        _modnorm_kernel,
        grid=(n_rows // tr,),
        in_specs=[pl.BlockSpec((tr, D), lambda i: (i, 0)),
                  pl.BlockSpec((1, D), lambda i: (0, 0)),
                  pl.BlockSpec((1, 1, D), grp),
                  pl.BlockSpec((1, 1, D), grp)],
        out_specs=pl.BlockSpec((tr, D), lambda i: (i, 0)),
        out_shape=jax.ShapeDtypeStruct((n_rows, D), BF),
        compiler_params=_params("arbitrary"),
        name="modnorm",
    )(x, gain.reshape(1, D), shift2.reshape(2, 1, D), scale2.reshape(2, 1, D))


def _staged_column_weights(w_hbm, col_blocks, wstage, w_bf, sem):
    j, i = pl.program_id(0), pl.program_id(1)
    tn = wstage.shape[-1]

    def copies(jj):
        return [pltpu.make_async_copy(w_hbm.at[0, :, pl.ds(pl.multiple_of(cb * tn, tn), tn)],
                                      wstage.at[k], sem.at[k])
                for k, cb in enumerate(col_blocks(jj))]

    @pl.when(i == 0)
    def _():
        @pl.when(j == 0)
        def _():
            for cp in copies(j):
                cp.start()

        for k, cp in enumerate(copies(j)):
            cp.wait()
            w_bf[k] = wstage[k].astype(BF)

        @pl.when(j + 1 < pl.num_programs(0))
        def _():
            for cp in copies(j + 1):
                cp.start()


def _conv_in_kernel(a_ref, w_hbm, b_ref, cv_ref, wstage, w_bf, sem):
    nj = pl.num_programs(0)
    _staged_column_weights(w_hbm, lambda jj: (jj, jj + nj, jj + 2 * nj), wstage, w_bf, sem)
    a = a_ref[...]
    b_ref[...] = jnp.dot(a, w_bf[0], preferred_element_type=F32).astype(BF)
    c = jnp.dot(a, w_bf[1], preferred_element_type=F32)
    v = jnp.dot(a, w_bf[2], preferred_element_type=F32)
    cv_ref[...] = (c * v).astype(BF)


def _conv_in(u, w_in):
    R, D = u.shape
    tm = _pick(R, 1100, BF16_SUBLANES)
    tn = _pick(D, 256, LANES)
    return pl.pallas_call(
        _conv_in_kernel,
        grid=(D // tn, R // tm),
        in_specs=[pl.BlockSpec((tm, D), lambda j, i: (i, 0)), pl.BlockSpec(memory_space=pl.ANY)],
        out_specs=[pl.BlockSpec((tm, tn), lambda j, i: (i, j)),
                   pl.BlockSpec((tm, tn), lambda j, i: (i, j))],
        out_shape=[jax.ShapeDtypeStruct((R, D), BF), jax.ShapeDtypeStruct((R, D), BF)],
        scratch_shapes=[pltpu.VMEM((3, D, tn), F32), pltpu.VMEM((3, D, tn), BF),
                        pltpu.SemaphoreType.DMA((3,))],
        compiler_params=_params("arbitrary", "arbitrary"),
        name="conv_in",
    )(u, w_in)


def _conv_mix_kernel(n_lat_tiles, b_ref, cv_ref, pv_ref, nx_ref, w_ref, o_ref):
    i = pl.program_id(0)
    tr = cv_ref.shape[0]
    cv = cv_ref[...].astype(F32)
    top_zero = (i == 0) | (i == n_lat_tiles)
    bot_zero = (i == n_lat_tiles - 1) | (i == pl.num_programs(0) - 1)
    hp = pv_ref[BF16_SUBLANES - 1:BF16_SUBLANES, :].astype(F32) * jnp.where(top_zero, 0.0, 1.0)
    hn = nx_ref[0:1, :].astype(F32) * jnp.where(bot_zero, 0.0, 1.0)
    rows = lax.broadcasted_iota(jnp.int32, cv.shape, 0)
    prev = jnp.where(rows == 0, hp, pltpu.roll(cv, 1, 0))
    nxt = jnp.where(rows == tr - 1, hn, pltpu.roll(cv, tr - 1, 0))
    w = w_ref[...]
    z = prev * w[0:1] + cv * w[1:2] + nxt * w[2:3]
    o_ref[...] = (b_ref[...].astype(F32) * z).astype(BF)


def _conv_mix(b, cv, conv_w, n_lat):
    R, D = cv.shape
    tr = _pick(math.gcd(n_lat, R), 256, BF16_SUBLANES)
    tc = _pick(D, 1024, LANES)
    hb = tr // BF16_SUBLANES
    last = R // BF16_SUBLANES - 1
    return pl.pallas_call(
        functools.partial(_conv_mix_kernel, n_lat // tr),
        grid=(R // tr, D // tc),
        in_specs=[pl.BlockSpec((tr, tc), lambda i, j: (i, j)),
                  pl.BlockSpec((tr, tc), lambda i, j: (i, j)),
                  pl.BlockSpec((BF16_SUBLANES, tc), lambda i, j: (jnp.maximum(i * hb - 1, 0), j)),
                  pl.BlockSpec((BF16_SUBLANES, tc), lambda i, j: (jnp.minimum((i + 1) * hb, last), j)),
                  pl.BlockSpec((3, tc), lambda i, j: (0, j))],
        out_specs=pl.BlockSpec((tr, tc), lambda i, j: (i, j)),
        out_shape=jax.ShapeDtypeStruct((R, D), BF),
        compiler_params=_params("arbitrary", "arbitrary"),
        name="conv_mix",
    )(b, cv, cv, cv, conv_w)


def _row_gate(g_ref, row0, shape, n_lat):
    rows = row0 + lax.broadcasted_iota(jnp.int32, shape, 0)
    return jnp.where(rows < n_lat, g_ref[0:1, :], g_ref[1:2, :])


def _proj_resid_kernel(n_lat, a_ref, w_ref, r_ref, g_ref, o_ref, w_bf):
    @pl.when(pl.program_id(1) == 0)
    def _():
        w_bf[...] = w_ref[0].astype(BF)

    tm = o_ref.shape[0]
    acc = jnp.dot(a_ref[...], w_bf[...], preferred_element_type=F32)
    g = _row_gate(g_ref, pl.program_id(1) * tm, acc.shape, n_lat)
    o_ref[...] = r_ref[...] + g * acc


def _proj_resid(a, w, resid, gate2, n_lat):
    M, K = a.shape
    N = w.shape[2]
    tm = _pick(M, 1100, BF16_SUBLANES)
    tn = _pick(N, 512, LANES)
    return pl.pallas_call(
        functools.partial(_proj_resid_kernel, n_lat),
        grid=(N // tn, M // tm),
        in_specs=[pl.BlockSpec((tm, K), lambda j, i: (i, 0)),
                  pl.BlockSpec((1, K, tn), lambda j, i: (0, 0, j)),
                  pl.BlockSpec((tm, tn), lambda j, i: (i, j)),
                  pl.BlockSpec((2, tn), lambda j, i: (0, j))],
        out_specs=pl.BlockSpec((tm, tn), lambda j, i: (i, j)),
        out_shape=jax.ShapeDtypeStruct((M, N), F32),
        scratch_shapes=[pltpu.VMEM((K, tn), BF)],
        compiler_params=_params("arbitrary", "arbitrary"),
        name="proj_resid",
    )(a, w, resid, gate2)


def _gu_kernel(n_real, a_ref, wg_ref, wu_ref, o_ref):
    j = pl.program_id(1)

    @pl.when(j < n_real)
    def _():
        a = a_ref[...]
        g = jnp.dot(a, wg_ref[0].astype(BF), preferred_element_type=F32)
        u = jnp.dot(a, wu_ref[0].astype(BF), preferred_element_type=F32)
        o_ref[...] = ((g * _sigmoid(g)) * u).astype(BF)

    @pl.when(j >= n_real)
    def _():
        o_ref[...] = jnp.zeros_like(o_ref)


def _ffn_gu(u, w_gu, f_pad):
    R, D = u.shape
    F = w_gu.shape[2] // 2
    tm = _pick(R, 1100, BF16_SUBLANES)
    tn = _pick(math.gcd(F, f_pad), 256, LANES)
    nf = F // tn
    col = lambda j: jnp.minimum(j, nf - 1)
    return pl.pallas_call(
        functools.partial(_gu_kernel, nf),
        grid=(R // tm, f_pad // tn),
        in_specs=[pl.BlockSpec((tm, D), lambda i, j: (i, 0)),
                  pl.BlockSpec((1, D, tn), lambda i, j: (0, 0, col(j))),
                  pl.BlockSpec((1, D, tn), lambda i, j: (0, 0, col(j) + nf))],
        out_specs=pl.BlockSpec((tm, tn), lambda i, j: (i, j)),
        out_shape=jax.ShapeDtypeStruct((R, f_pad), BF),
        compiler_params=_params("arbitrary", "arbitrary"),
        name="ffn_gu",
    )(u, w_gu, w_gu)


def _down_kernel(n_lat, a_ref, w_ref, r_ref, g_ref, o_ref, acc_ref):
    k = pl.program_id(2)

    @pl.when(k == 0)
    def _():
        acc_ref[...] = jnp.zeros_like(acc_ref)

    acc_ref[...] += jnp.dot(a_ref[...], w_ref[...], preferred_element_type=F32)

    @pl.when(k == pl.num_programs(2) - 1)
    def _():
        acc = acc_ref[...]
        g = _row_gate(g_ref, pl.program_id(0) * acc.shape[0], acc.shape, n_lat)
        o_ref[...] = r_ref[...] + g * acc


def _ffn_down(h, w_down, resid, gate2, n_lat):
    R, Fp = h.shape
    D = w_down.shape[1]
    tm = _pick(R, 1100, BF16_SUBLANES)
    tn = _pick(D, 1024, LANES)
    tk = _pick(Fp, 2816, LANES)
    return pl.pallas_call(
        functools.partial(_down_kernel, n_lat),
        grid=(R // tm, D // tn, Fp // tk),
        in_specs=[pl.BlockSpec((tm, tk), lambda i, j, k: (i, k)),
                  pl.BlockSpec((tk, tn), lambda i, j, k: (k, j)),
                  pl.BlockSpec((tm, tn), lambda i, j, k: (i, j)),
                  pl.BlockSpec((2, tn), lambda i, j, k: (0, j))],
        out_specs=pl.BlockSpec((tm, tn), lambda i, j, k: (i, j)),
        out_shape=jax.ShapeDtypeStruct((R, D), F32),
        scratch_shapes=[pltpu.VMEM((tm, tn), F32)],
        compiler_params=_params("arbitrary", "arbitrary", "arbitrary"),
        name="ffn_down",
    )(h, w_down, resid, gate2)


def _qkv_kernel(n_qk_tiles, q_scale, a_ref, w_ref, gain_ref, cos_ref, sa_ref, sb_ref, o_ref, w_bf):
    j = pl.program_id(0)

    @pl.when(pl.program_id(1) == 0)
    def _():
        w_bf[...] = w_ref[0].astype(BF)

    acc = jnp.dot(a_ref[...], w_bf[...], preferred_element_type=F32)

    @pl.when(j >= 2 * n_qk_tiles)
    def _():
        o_ref[...] = acc.astype(BF)

    @pl.when(j < 2 * n_qk_tiles)
    def _():
        gain = gain_ref[0]
        cos, sa, sb = cos_ref[...], sa_ref[...], sb_ref[...]
        post = jnp.where(j < n_qk_tiles, q_scale, 1.0)
        half = LANES // 4
        for g in range(acc.shape[1] // LANES):
            t = acc[:, g * LANES:(g + 1) * LANES]
            t = t * lax.rsqrt(jnp.mean(t * t, axis=-1, keepdims=True) + NORM_EPS)
            t = t * gain
            t = t * cos + pltpu.roll(t, LANES - half, 1) * sa + pltpu.roll(t, half, 1) * sb
            o_ref[:, g * LANES:(g + 1) * LANES] = (t * post).astype(BF)


def _qkv(u, w_qkv, qk_gain2, cos, sa, sb, q_scale):
    R, D = u.shape
    tm = _pick(R, 1100, BF16_SUBLANES)
    tn = _pick(D, 512, LANES)
    nq = D // tn
    tab = pl.BlockSpec((tm, LANES), lambda j, i: (i, 0))
    return pl.pallas_call(
        functools.partial(_qkv_kernel, nq, q_scale),
        grid=(3 * nq, R // tm),
        in_specs=[pl.BlockSpec((tm, D), lambda j, i: (i, 0)),
                  pl.BlockSpec((1, D, tn), lambda j, i: (0, 0, j)),
                  pl.BlockSpec((1, 1, LANES), lambda j, i: (jnp.where(j < nq, 0, 1), 0, 0)),
                  tab, tab, tab],
        out_specs=pl.BlockSpec((tm, tn), lambda j, i: (i, j)),
        out_shape=jax.ShapeDtypeStruct((R, 3 * D), BF),
        scratch_shapes=[pltpu.VMEM((D, tn), BF)],
        compiler_params=_params("arbitrary", "arbitrary"),
        name="qkv_proj",
    )(u, w_qkv, qk_gain2, cos, sa, sb)


def _attn_kernel(n_lat, tk, lambda_init, q_ref, k_ref, v_ref, lam_ref, sg_ref, o_ref,
                 sa_ref, sb_ref, sc_ref, m_ref, l_ref, acc_ref):
    hd = q_ref.shape[1] // 2
    n_kv = k_ref.shape[0]
    n_chunks = n_lat // tk
    has_ctx = n_kv > n_lat
    m_ref[...] = jnp.full_like(m_ref, -jnp.inf)
    l_ref[...] = jnp.zeros_like(l_ref)
    acc_ref[...] = jnp.zeros_like(acc_ref)

    def scores(kc, dst):
        for m in range(2):
            dst[m] = lax.dot_general(q_ref[:, m * hd:(m + 1) * hd], kc[:, m * hd:(m + 1) * hd],
                                     (((1,), (1,)), ((), ())), preferred_element_type=F32)

    def absorb(src, vc):
        scaled = []
        for m in range(2):
            s = src[m]
            m_old = m_ref[m]
            m_new = jnp.maximum(m_old, jnp.max(s, axis=-1, keepdims=True))
            alpha = jnp.exp2(m_old - m_new)
            p = jnp.exp2(s - m_new)
            l_ref[m] = alpha * l_ref[m] + jnp.sum(p, axis=-1, keepdims=True)
            m_ref[m] = m_new
            scaled.append((alpha, p.astype(BF)))
        for m in range(2):
            alpha, p = scaled[m]
            acc_ref[m] = alpha * acc_ref[m] + jnp.dot(p, vc, preferred_element_type=F32)

    def k_chunk(c):
        return k_ref[pl.ds(pl.multiple_of(c * tk, tk), tk), :]

    def v_chunk(c):
        return v_ref[pl.ds(pl.multiple_of(c * tk, tk), tk), :]

    scores(k_chunk(0), sa_ref)

    def pair(i, carry):
        c = 2 * i
        scores(k_chunk(c + 1), sb_ref)
        absorb(sa_ref, v_chunk(c))
        scores(k_chunk(c + 2), sa_ref)
        absorb(sb_ref, v_chunk(c + 1))
        return carry

    n_pairs = (n_chunks - 1) // 2
    lax.fori_loop(0, n_pairs, pair, 0)
    c = 2 * n_pairs
    if n_chunks - c == 2:
        scores(k_chunk(c + 1), sb_ref)
        absorb(sa_ref, v_chunk(c))
        if has_ctx:
            scores(k_ref[n_lat:n_kv, :], sc_ref)
        absorb(sb_ref, v_chunk(c + 1))
    else:
        if has_ctx:
            scores(k_ref[n_lat:n_kv, :], sc_ref)
        absorb(sa_ref, v_chunk(c))
    if has_ctx:
        absorb(sc_ref, v_ref[n_lat:n_kv, :])

    lf = lam_ref[...]
    lam = (jnp.exp(jnp.sum(lf[0:1] * lf[1:2], axis=-1, keepdims=True))
           - jnp.exp(jnp.sum(lf[2:3] * lf[3:4], axis=-1, keepdims=True)) + lambda_init)
    o = acc_ref[0] / l_ref[0] - lam * (acc_ref[1] / l_ref[1])
    o = o * lax.rsqrt(jnp.mean(o * o, axis=-1, keepdims=True) + NORM_EPS)
    o_ref[...] = ((o * sg_ref[...]) * (1.0 - lambda_init)).astype(BF)


def _diff_attention(qkv, lambdas, subln_gain, n_lat, n_heads, lambda_init):
    R = qkv.shape[0]
    D = qkv.shape[1] // 3
    vd = D // n_heads
    tq = _pick(n_lat, 512, BF16_SUBLANES)
    tk = _pick(n_lat, 1024, LANES)
    return pl.pallas_call(
        functools.partial(_attn_kernel, n_lat, tk, lambda_init),
        grid=(n_heads, n_lat // tq),
        in_specs=[pl.BlockSpec((tq, vd), lambda h, i: (i, h)),
                  pl.BlockSpec((R, vd), lambda h, i: (0, n_heads + h)),
                  pl.BlockSpec((R, vd), lambda h, i: (0, 2 * n_heads + h)),
                  pl.BlockSpec(lambdas.shape, lambda h, i: (0, 0)),
                  pl.BlockSpec((1, vd), lambda h, i: (0, 0))],
        out_specs=pl.BlockSpec((tq, vd), lambda h, i: (i, h)),
        out_shape=jax.ShapeDtypeStruct((n_lat, D), BF),
        scratch_shapes=[pltpu.VMEM((2, tq, tk), F32), pltpu.VMEM((2, tq, tk), F32),
                        pltpu.VMEM((2, tq, (R - n_lat) or LANES), F32),
                        pltpu.VMEM((2, tq, 1), F32), pltpu.VMEM((2, tq, 1), F32),
                        pltpu.VMEM((2, tq, vd), F32)],
        compiler_params=_params("arbitrary", "arbitrary"),
        name="diff_attention",
    )(qkv, qkv, qkv, lambdas, subln_gain.reshape(1, vd))


def _router_kernel(n_experts, x_ref, g_ref, sh_ref, sc_ref, w_ref, comb_ref, sel_ref):
    u = _modnorm_rows(x_ref[...], g_ref[...], sh_ref[...], sc_ref[...]).astype(BF)
    logits = jnp.dot(u, w_ref[...], preferred_element_type=F32)
    lane = lax.broadcasted_iota(jnp.int32, logits.shape, 1)
    neg = jnp.float32(-jnp.inf)
    l1 = jnp.where(lane < n_experts, logits, neg)
    m1 = jnp.max(l1, axis=-1, keepdims=True)
    i1 = jnp.min(jnp.where(l1 == m1, lane, LANES), axis=-1, keepdims=True)
    sel1 = lane == i1
    l2 = jnp.where(sel1, neg, l1)
    m2 = jnp.max(l2, axis=-1, keepdims=True)
    i2 = jnp.min(jnp.where(l2 == m2, lane, LANES), axis=-1, keepdims=True)
    sel2 = lane == i2
    e2 = jnp.exp(m2 - m1)
    den = 1.0 + e2
    comb_ref[...] = jnp.where(sel1, 1.0 / den, 0.0) + jnp.where(sel2, e2 / den, 0.0)
    sel_ref[...] = (sel1 | sel2).astype(jnp.int32)


def _router(x, gain, shift, scale, w_router_pad, n_experts):
    S, D = x.shape
    tm = _pick(S, 512, 8)
    vec = pl.BlockSpec((1, D), lambda i: (0, 0))
    out = pl.BlockSpec((tm, LANES), lambda i: (i, 0))
    return pl.pallas_call(
        functools.partial(_router_kernel, n_experts),
        grid=(S // tm,),
        in_specs=[pl.BlockSpec((tm, D), lambda i: (i, 0)), vec, vec, vec,
                  pl.BlockSpec((D, LANES), lambda i: (0, 0))],
        out_specs=[out, out],
        out_shape=[jax.ShapeDtypeStruct((S, LANES), F32), jax.ShapeDtypeStruct((S, LANES), jnp.int32)],
        compiler_params=_params("arbitrary"),
        name="moe_router",
    )(x, gain.reshape(1, D), shift.reshape(1, D), scale.reshape(1, D), w_router_pad)


def _row_copy(src_hbm, row, dst, r, sem):
    return pltpu.make_async_copy(src_hbm.at[pl.ds(row, 1)], dst.at[pl.ds(r, 1)], sem)


ROW_DMA_UNROLL = 8


def _gather_norm_kernel(tok_ref, nrows_ref, x_hbm, g_ref, sh_ref, sc_ref, o_ref, buf, sem):
    tg = buf.shape[1]
    t = pl.program_id(0)

    def used(tile):
        return tile * tg < nrows_ref[0]

    def issue(tile):
        slot = tile % 2

        def one(r, c):
            _row_copy(x_hbm, tok_ref[tile * tg + r], buf.at[slot], r, sem.at[slot]).start()
            return c

        lax.fori_loop(0, tg, one, 0, unroll=ROW_DMA_UNROLL)

    def wait(tile):
        slot = tile % 2

        def one(r, c):
            _row_copy(x_hbm, 0, buf.at[slot], r, sem.at[slot]).wait()
            return c

        lax.fori_loop(0, tg, one, 0, unroll=ROW_DMA_UNROLL)

    @pl.when((t == 0) & used(0))
    def _():
        issue(0)

    @pl.when((t + 1 < pl.num_programs(0)) & used(t + 1))
    def _():
        issue(t + 1)

    @pl.when(used(t))
    def _():
        wait(t)
        o_ref[...] = _modnorm_rows(buf[t % 2], g_ref[...], sh_ref[...], sc_ref[...]).astype(BF)

    @pl.when(jnp.logical_not(used(t)))
    def _():
        o_ref[...] = jnp.zeros_like(o_ref)


def _gather_norm(x, row_token, n_rows_used, gain, shift, scale):
    D = x.shape[1]
    Rs = row_token.shape[0]
    tg = _pick(Rs, 256, BF16_SUBLANES)
    vec = pl.BlockSpec((1, D), lambda t, tok, n: (0, 0))
    return pl.pallas_call(
        _gather_norm_kernel,
        grid_spec=pltpu.PrefetchScalarGridSpec(
            num_scalar_prefetch=2,
            grid=(Rs // tg,),
            in_specs=[pl.BlockSpec(memory_space=pl.ANY), vec, vec, vec],
            out_specs=pl.BlockSpec((tg, D), lambda t, tok, n: (t, 0)),
            scratch_shapes=[pltpu.VMEM((2, tg, D), F32), pltpu.SemaphoreType.DMA((2,))]),
        out_shape=jax.ShapeDtypeStruct((Rs, D), BF),
        compiler_params=_params("arbitrary"),
        name="moe_gather_norm",
    )(row_token, n_rows_used, x, gain.reshape(1, D), shift.reshape(1, D), scale.reshape(1, D))


SCHED_EXPERT, SCHED_RUN_START, SCHED_FIRST_RUN, SCHED_NEXT_EXPERT, SCHED_LAST_RUN = range(5)


def _staged_expert_weights(sched_ref, cnt_ref, w_hbm, col_blocks, wstage, w_bf, sem):
    j, t = pl.program_id(0), pl.program_id(1)
    nj, n_t = pl.num_programs(0), pl.num_programs(1)
    tn = wstage.shape[-1]

    def copies(e, jj):
        return [pltpu.make_async_copy(w_hbm.at[0, e, :, pl.ds(pl.multiple_of(cb * tn, tn), tn)],
                                      wstage.at[k], sem.at[k])
                for k, cb in enumerate(col_blocks(jj))]

    def sched(row):
        return sched_ref[row * n_t + t]

    @pl.when(sched(SCHED_RUN_START) == 1)
    def _():
        e = sched(SCHED_EXPERT)

        @pl.when((j == 0) & (sched(SCHED_FIRST_RUN) == 1))
        def _():
            for cp in copies(e, j):
                cp.start()

        for k, cp in enumerate(copies(e, j)):
            cp.wait()
            w_bf[k] = wstage[k].astype(BF)

        next_j = j + sched(SCHED_LAST_RUN)

        @pl.when(next_j < nj)
        def _():
            for cp in copies(sched(SCHED_NEXT_EXPERT), next_j):
                cp.start()


def _moe_gu_kernel(sched_ref, cnt_ref, a_ref, w_hbm, o_ref, wstage, w_bf, sem):
    t = pl.program_id(1)
    nj = pl.num_programs(0)

    @pl.when(t < cnt_ref[0])
    def _():
        _staged_expert_weights(sched_ref, cnt_ref, w_hbm, lambda jj: (jj, jj + nj), wstage, w_bf, sem)
        a = a_ref[...]
        g = jnp.dot(a, w_bf[0], preferred_element_type=F32)
        u = jnp.dot(a, w_bf[1], preferred_element_type=F32)
        o_ref[...] = ((g * _sigmoid(g)) * u).astype(BF)

    @pl.when(t >= cnt_ref[0])
    def _():
        o_ref[...] = jnp.zeros_like(o_ref)


def _moe_gu(a, w_gu, sched, counts, tm):
    Rs, D = a.shape
    F = w_gu.shape[3] // 2
    tn = _pick(F, 512, LANES)
    tile = lambda t, cnt: jnp.minimum(t, cnt[0] - 1)
    return pl.pallas_call(
        _moe_gu_kernel,
        grid_spec=pltpu.PrefetchScalarGridSpec(
            num_scalar_prefetch=2,
            grid=(F // tn, Rs // tm),
            in_specs=[pl.BlockSpec((tm, D), lambda j, t, sc, cnt: (tile(t, cnt), 0)),
                      pl.BlockSpec(memory_space=pl.ANY)],
            out_specs=pl.BlockSpec((tm, tn), lambda j, t, sc, cnt: (t, j)),
            scratch_shapes=[pltpu.VMEM((2, D, tn), F32), pltpu.VMEM((2, D, tn), BF),
                            pltpu.SemaphoreType.DMA((2,))]),
        out_shape=jax.ShapeDtypeStruct((Rs, F), BF),
        compiler_params=_params("arbitrary", "arbitrary"),
        name="moe_gu",
    )(sched, counts, a, w_gu)


def _moe_down_kernel(sched_ref, cnt_ref, a_ref, w_hbm, o_ref, wstage, w_bf, sem):
    t = pl.program_id(1)

    @pl.when(t < cnt_ref[0])
    def _():
        _staged_expert_weights(sched_ref, cnt_ref, w_hbm, lambda jj: (jj,), wstage, w_bf, sem)
        o_ref[...] = jnp.dot(a_ref[...], w_bf[0], preferred_element_type=F32)

    @pl.when(t >= cnt_ref[0])
    def _():
        o_ref[...] = jnp.zeros_like(o_ref)


def _moe_down(h, w_down, sched, counts, tm):
    Rs, F = h.shape
    D = w_down.shape[3]
    tn = _pick(D, 512, LANES)
    tile = lambda t, cnt: jnp.minimum(t, cnt[0] - 1)
    return pl.pallas_call(
        _moe_down_kernel,
        grid_spec=pltpu.PrefetchScalarGridSpec(
            num_scalar_prefetch=2,
            grid=(D // tn, Rs // tm),
            in_specs=[pl.BlockSpec((tm, F), lambda j, t, sc, cnt: (tile(t, cnt), 0)),
                      pl.BlockSpec(memory_space=pl.ANY)],
            out_specs=pl.BlockSpec((tm, tn), lambda j, t, sc, cnt: (t, j)),
            scratch_shapes=[pltpu.VMEM((1, F, tn), F32), pltpu.VMEM((1, F, tn), BF),
                            pltpu.SemaphoreType.DMA((1,))]),
        out_shape=jax.ShapeDtypeStruct((Rs, D), F32),
        compiler_params=_params("arbitrary", "arbitrary"),
        name="moe_down",
    )(sched, counts, h, w_down)


def _combine_kernel(pa_ref, pb_ref, y_hbm, x_ref, g_ref, wa_ref, wb_ref, o_ref, ya, yb, sem):
    tt = ya.shape[1]
    t = pl.program_id(0)

    def issue(tile):
        slot = tile % 2

        def one(r, c):
            _row_copy(y_hbm, pa_ref[tile * tt + r], ya.at[slot], r, sem.at[0, slot]).start()
            _row_copy(y_hbm, pb_ref[tile * tt + r], yb.at[slot], r, sem.at[1, slot]).start()
            return c

        lax.fori_loop(0, tt, one, 0, unroll=ROW_DMA_UNROLL)

    def wait(tile):
        slot = tile % 2

        def one(r, c):
            _row_copy(y_hbm, 0, ya.at[slot], r, sem.at[0, slot]).wait()
            _row_copy(y_hbm, 0, yb.at[slot], r, sem.at[1, slot]).wait()
            return c

        lax.fori_loop(0, tt, one, 0, unroll=ROW_DMA_UNROLL)

    @pl.when(t == 0)
    def _():
        issue(0)

    @pl.when(t + 1 < pl.num_programs(0))
    def _():
        issue(t + 1)

    wait(t)
    slot = t % 2
    o_ref[...] = x_ref[...] + g_ref[...] * (wa_ref[...] * ya[slot] + wb_ref[...] * yb[slot])


def _moe_combine(x, y, pos_a, pos_b, w_a, w_b, gate):
    S, D = x.shape
    tt = _pick(S, 256, 8)
    col = pl.BlockSpec((tt, 1), lambda t, pa, pb: (t, 0))
    return pl.pallas_call(
        _combine_kernel,
        grid_spec=pltpu.PrefetchScalarGridSpec(
            num_scalar_prefetch=2,
            grid=(S // tt,),
            in_specs=[pl.BlockSpec(memory_space=pl.ANY),
                      pl.BlockSpec((tt, D), lambda t, pa, pb: (t, 0)),
                      pl.BlockSpec((1, D), lambda t, pa, pb: (0, 0)),
                      col, col],
            out_specs=pl.BlockSpec((tt, D), lambda t, pa, pb: (t, 0)),
            scratch_shapes=[pltpu.VMEM((2, tt, D), F32), pltpu.VMEM((2, tt, D), F32),
                            pltpu.SemaphoreType.DMA((2, 2))]),
        out_shape=jax.ShapeDtypeStruct((S, D), F32),
        compiler_params=_params("arbitrary"),
        name="moe_combine",
    )(pos_a, pos_b, y, x, gate.reshape(1, D), w_a, w_b)


def _routing_tables(sel, comb, tm):
    S, E = sel.shape
    Rs = TOP_K * S + E * tm
    seli = sel.astype(jnp.int32)
    cnt = jnp.sum(seli, axis=0)
    padded = (cnt + tm - 1) // tm * tm
    gend = jnp.cumsum(padded)
    gstart = gend - padded
    pos = gstart[None, :] + jnp.cumsum(seli, axis=0) - seli
    pos_a = jnp.min(jnp.where(sel, pos, Rs), axis=1).astype(jnp.int32)
    pos_b = jnp.max(jnp.where(sel, pos, -1), axis=1).astype(jnp.int32)
    w_a = jnp.sum(jnp.where(sel & (pos == pos_a[:, None]), comb, 0.0), axis=1, keepdims=True)
    w_b = jnp.sum(jnp.where(sel & (pos == pos_b[:, None]), comb, 0.0), axis=1, keepdims=True)
    tok = jnp.arange(S, dtype=jnp.int32)
    row_token = jnp.zeros((Rs,), jnp.int32).at[jnp.concatenate([pos_a, pos_b])].set(
        jnp.concatenate([tok, tok]), mode="drop")
    n_t = Rs // tm
    tix = jnp.arange(n_t, dtype=jnp.int32)
    tile_expert = jnp.minimum(
        jnp.sum((tix[:, None] * tm >= gend[None, :]).astype(jnp.int32), axis=1), E - 1).astype(jnp.int32)
    n_rows = gend[-1].astype(jnp.int32)
    n_tiles = n_rows // tm
    prev_expert = jnp.concatenate([jnp.full((1,), -1, jnp.int32), tile_expert[:-1]])
    first = (tix < n_tiles) & (tile_expert != prev_expert)
    first_run = first & (tix == 0)
    later_first = first[None, :] & (tix[None, :] > tix[:, None])
    next_first = jnp.min(jnp.where(later_first, tix[None, :], n_t), axis=1)
    last_run = next_first >= n_t
    next_expert = jnp.where(last_run, tile_expert[0], jnp.sum(
        jnp.where(tix[None, :] == next_first[:, None], tile_expert[None, :], 0), axis=1))
    sched = jnp.concatenate([tile_expert, first.astype(jnp.int32), first_run.astype(jnp.int32), next_expert,
                             last_run.astype(jnp.int32)]).astype(jnp.int32)
    return row_token, pos_a, pos_b, w_a, w_b, sched, n_tiles.reshape(1), n_rows.reshape(1)


def _rope_tables(n_lat, n_rows, head_dim):
    n_freq = head_dim // 4
    s = jnp.arange(n_lat, dtype=jnp.int32)
    inv = ROPE_BASE ** (-jnp.arange(n_freq, dtype=F32) / n_freq)
    ang = jnp.stack([s // GRID_W, s % GRID_W], axis=-1).astype(F32)[..., None] * inv
    cos, sin = jnp.cos(ang), jnp.sin(ang)
    zero = jnp.zeros_like(sin)
    cos_t = jnp.stack([cos, cos], axis=2).reshape(n_lat, head_dim)
    sa_t = jnp.stack([-sin, zero], axis=2).reshape(n_lat, head_dim)
    sb_t = jnp.stack([zero, sin], axis=2).reshape(n_lat, head_dim)
    pad = ((0, n_rows - n_lat), (0, 0))
    return jnp.pad(cos_t, pad, constant_values=1.0), jnp.pad(sa_t, pad), jnp.pad(sb_t, pad)


def kernel(x, c, ctx, c_ctx, w_mod, b_mod, norm_gain, conv_w_in, conv_w, conv_w_out, attn_w_qkv, attn_w_o,
           attn_q_gain, attn_k_gain, attn_lambdas, attn_subln_gain, ffn_w_gu, ffn_w_down, moe_router,
           moe_w_gu, moe_w_down):
    assert x.shape[0] == 1 and w_mod.shape[0] == 2
    S, D = x.shape[1], x.shape[2]
    C = ctx.shape[1]
    R = S + C
    head_dim = attn_q_gain.shape[-1]
    n_heads = D // (2 * head_dim)
    E = moe_router.shape[-1]
    F = ffn_w_down.shape[1]
    f_pad = _round_up(F, 512)

    cc = jnp.stack([c[0], c_ctx]).reshape(2, D, 1)
    mod = _mod_vectors(cc, w_mod, b_mod)[:, :2, :].reshape(2, 2, N_MOD, D)

    xa = jnp.concatenate([x[0], ctx[0]], axis=0)

    m0 = mod[0]
    u = _modnorm(xa, norm_gain[0, 0], m0[:, 0], m0[:, 1], S, R)
    b, cv = _conv_in(u, conv_w_in)
    bz = _conv_mix(b, cv, conv_w[0], S)
    xa = _proj_resid(bz, conv_w_out, xa, m0[:, 2], S)
    u = _modnorm(xa, norm_gain[0, 1], m0[:, 3], m0[:, 4], S, R)
    w_dn = jnp.pad(ffn_w_down[0], ((0, f_pad - F), (0, 0))).astype(BF)
    h = _ffn_gu(u, ffn_w_gu, f_pad)
    xa = _ffn_down(h, w_dn, xa, m0[:, 5], S)

    m1 = mod[1]
    lambda_init = 0.8 - 0.6 * math.exp(-0.3 * 1)
    u = _modnorm(xa, norm_gain[1, 0], m1[:, 0], m1[:, 1], S, R)
    cos, sa, sb = _rope_tables(S, R, head_dim)
    qk_gain = jnp.stack([attn_q_gain[0], attn_k_gain[0]]).reshape(2, 1, head_dim)
    qkv = _qkv(u, attn_w_qkv, qk_gain, cos, sa, sb, head_dim ** -0.5 * math.log2(math.e))
    o = _diff_attention(qkv, attn_lambdas[0], attn_subln_gain[0], S, n_heads, lambda_init)
    x1 = _proj_resid(o, attn_w_o, xa, m1[:, 2], S)

    tm = _pick(S, 512, BF16_SUBLANES)
    w_r = jnp.pad(moe_router[0], ((0, 0), (0, LANES - E))).astype(BF)
    comb, selm = _router(x1, norm_gain[1, 1], m1[0, 3], m1[0, 4], w_r, E)
    row_token, pos_a, pos_b, w_a, w_b, sched, counts, n_rows = _routing_tables(
        selm[:, :E] > 0, comb[:, :E], tm)
    us = _gather_norm(x1, row_token, n_rows, norm_gain[1, 1], m1[0, 3], m1[0, 4])
    hs = _moe_gu(us, moe_w_gu, sched, counts, tm)
    ys = _moe_down(hs, moe_w_down, sched, counts, tm)
    out = _moe_combine(x1, ys, pos_a, pos_b, w_a, w_b, m1[0, 5])
    return out[None]
```

```python
import functools
import math

import jax
import jax.numpy as jnp
from jax import lax
from jax.experimental import pallas as pl
from jax.experimental.pallas import tpu as pltpu

BF = jnp.bfloat16
F32 = jnp.float32

GRID_W = 64
ROPE_BASE = 10000.0
NORM_EPS = 1e-6
TOP_K = 2
N_MOD = 6

LANES = 128
BF16_SUBLANES = 16
VMEM_LIMIT_BYTES = 56 * 1024 * 1024


def _pick(n, target, align):
    best = None
    d = align
    while d <= min(n, target):
        if n % d == 0:
            best = d
        d += align
    return best if best is not None else n


def _round_up(n, m):
    return (n + m - 1) // m * m


def _params(*sem):
    return pltpu.CompilerParams(dimension_semantics=sem, vmem_limit_bytes=VMEM_LIMIT_BYTES)


def _sigmoid(x):
    return 1.0 / (1.0 + jnp.exp(-x))


def _mod_kernel(c_ref, w_ref, b_ref, o_ref, s_ref):
    @pl.when((pl.program_id(0) == 0) & (pl.program_id(1) == 0))
    def _():
        c = c_ref[...]
        s_ref[...] = jnp.broadcast_to(c * _sigmoid(c), s_ref.shape)

    o_ref[...] = jnp.zeros_like(o_ref)
    for g in range(w_ref.shape[2] // LANES):
        w = w_ref[0, :, g * LANES:(g + 1) * LANES]
        for r in range(2):
            o_ref[0, r:r + 1, g * LANES:(g + 1) * LANES] = (
                jnp.sum(w * s_ref[r], axis=0, keepdims=True) + b_ref[0, :, g * LANES:(g + 1) * LANES])


def _mod_vectors(cc, w_mod, b_mod):
    L, D, N = w_mod.shape
    tn = _pick(N, 512, LANES)
    return pl.pallas_call(
        _mod_kernel,
        grid=(L, N // tn),
        in_specs=[pl.BlockSpec((2, D, 1), lambda l, j: (0, 0, 0)),
                  pl.BlockSpec((1, D, tn), lambda l, j: (l, 0, j)),
                  pl.BlockSpec((1, 1, tn), lambda l, j: (l, 0, j))],
        out_specs=pl.BlockSpec((1, 8, tn), lambda l, j: (l, 0, j)),
        out_shape=jax.ShapeDtypeStruct((L, 8, N), F32),
        scratch_shapes=[pltpu.VMEM((2, D, LANES), F32)],
        compiler_params=_params("arbitrary", "arbitrary"),
        name="mod_vectors",
    )(cc, w_mod, b_mod.reshape(L, 1, N))


def _modnorm_rows(x, gain, shift, scale):
    ms = jnp.mean(x * x, axis=-1, keepdims=True)
    xn = x * lax.rsqrt(ms + NORM_EPS)
    return (xn * gain) * (1.0 + scale) + shift


def _modnorm_kernel(x_ref, g_ref, sh_ref, sc_ref, o_ref):
    o_ref[...] = _modnorm_rows(x_ref[...], g_ref[...], sh_ref[0], sc_ref[0]).astype(BF)


def _modnorm(x, gain, shift2, scale2, n_lat, n_rows):
    D = x.shape[1]
    tr = _pick(math.gcd(n_lat, n_rows), 256, BF16_SUBLANES)
    n_lat_tiles = n_lat // tr
    grp = lambda i: (jnp.where(i < n_lat_tiles, 0, 1), 0, 0)
    return pl.pallas_call(
        _modnorm_kernel,
        grid=(n_rows // tr,),
        in_specs=[pl.BlockSpec((tr, D), lambda i: (i, 0)),
                  pl.BlockSpec((1, D), lambda i: (0, 0)),
                  pl.BlockSpec((1, 1, D), grp),
                  pl.BlockSpec((1, 1, D), grp)],
        out_specs=pl.BlockSpec((tr, D), lambda i: (i, 0)),
        out_shape=jax.ShapeDtypeStruct((n_rows, D), BF),
        compiler_params=_params("arbitrary"),
        name="modnorm",
    )(x, gain.reshape(1, D), shift2.reshape(2, 1, D), scale2.reshape(2, 1, D))


def _staged_column_weights(w_hbm, col_blocks, wstage, w_bf, sem):
    j, i = pl.program_id(0), pl.program_id(1)
    tn = wstage.shape[-1]

    def copies(jj):
        return [pltpu.make_async_copy(w_hbm.at[0, :, pl.ds(pl.multiple_of(cb * tn, tn), tn)],
                                      wstage.at[k], sem.at[k])
                for k, cb in enumerate(col_blocks(jj))]

    @pl.when(i == 0)
    def _():
        @pl.when(j == 0)
        def _():
            for cp in copies(j):
                cp.start()

        for k, cp in enumerate(copies(j)):
            cp.wait()
            w_bf[k] = wstage[k].astype(BF)

        @pl.when(j + 1 < pl.num_programs(0))
        def _():
            for cp in copies(j + 1):
                cp.start()


def _conv_in_kernel(a_ref, w_hbm, b_ref, cv_ref, wstage, w_bf, sem):
    nj = pl.num_programs(0)
    _staged_column_weights(w_hbm, lambda jj: (jj, jj + nj, jj + 2 * nj), wstage, w_bf, sem)
    a = a_ref[...]
    b_ref[...] = jnp.dot(a, w_bf[0], preferred_element_type=F32).astype(BF)
    c = jnp.dot(a, w_bf[1], preferred_element_type=F32)
    v = jnp.dot(a, w_bf[2], preferred_element_type=F32)
    cv_ref[...] = (c * v).astype(BF)


def _conv_in(u, w_in):
    R, D = u.shape
    tm = _pick(R, 1100, BF16_SUBLANES)
    tn = _pick(D, 256, LANES)
    return pl.pallas_call(
        _conv_in_kernel,
        grid=(D // tn, R // tm),
        in_specs=[pl.BlockSpec((tm, D), lambda j, i: (i, 0)), pl.BlockSpec(memory_space=pl.ANY)],
        out_specs=[pl.BlockSpec((tm, tn), lambda j, i: (i, j)),
                   pl.BlockSpec((tm, tn), lambda j, i: (i, j))],
        out_shape=[jax.ShapeDtypeStruct((R, D), BF), jax.ShapeDtypeStruct((R, D), BF)],
        scratch_shapes=[pltpu.VMEM((3, D, tn), F32), pltpu.VMEM((3, D, tn), BF),
                        pltpu.SemaphoreType.DMA((3,))],
        compiler_params=_params("arbitrary", "arbitrary"),
        name="conv_in",
    )(u, w_in)


def _conv_mix_kernel(n_lat_tiles, b_ref, cv_ref, pv_ref, nx_ref, w_ref, o_ref):
    i = pl.program_id(0)
    tr = cv_ref.shape[0]
    cv = cv_ref[...].astype(F32)
    top_zero = (i == 0) | (i == n_lat_tiles)
    bot_zero = (i == n_lat_tiles - 1) | (i == pl.num_programs(0) - 1)
    hp = pv_ref[BF16_SUBLANES - 1:BF16_SUBLANES, :].astype(F32) * jnp.where(top_zero, 0.0, 1.0)
    hn = nx_ref[0:1, :].astype(F32) * jnp.where(bot_zero, 0.0, 1.0)
    rows = lax.broadcasted_iota(jnp.int32, cv.shape, 0)
    prev = jnp.where(rows == 0, hp, pltpu.roll(cv, 1, 0))
    nxt = jnp.where(rows == tr - 1, hn, pltpu.roll(cv, tr - 1, 0))
    w = w_ref[...]
    z = prev * w[0:1] + cv * w[1:2] + nxt * w[2:3]
    o_ref[...] = (b_ref[...].astype(F32) * z).astype(BF)


def _conv_mix(b, cv, conv_w, n_lat):
    R, D = cv.shape
    tr = _pick(math.gcd(n_lat, R), 256, BF16_SUBLANES)
    tc = _pick(D, 1024, LANES)
    hb = tr // BF16_SUBLANES
    last = R // BF16_SUBLANES - 1
    return pl.pallas_call(
        functools.partial(_conv_mix_kernel, n_lat // tr),
        grid=(R // tr, D // tc),
        in_specs=[pl.BlockSpec((tr, tc), lambda i, j: (i, j)),
                  pl.BlockSpec((tr, tc), lambda i, j: (i, j)),
                  pl.BlockSpec((BF16_SUBLANES, tc), lambda i, j: (jnp.maximum(i * hb - 1, 0), j)),
                  pl.BlockSpec((BF16_SUBLANES, tc), lambda i, j: (jnp.minimum((i + 1) * hb, last), j)),
                  pl.BlockSpec((3, tc), lambda i, j: (0, j))],
        out_specs=pl.BlockSpec((tr, tc), lambda i, j: (i, j)),
        out_shape=jax.ShapeDtypeStruct((R, D), BF),
        compiler_params=_params("arbitrary", "arbitrary"),
        name="conv_mix",
    )(b, cv, cv, cv, conv_w)


def _row_gate(g_ref, row0, shape, n_lat):
    rows = row0 + lax.broadcasted_iota(jnp.int32, shape, 0)
    return jnp.where(rows < n_lat, g_ref[0:1, :], g_ref[1:2, :])


def _proj_resid_kernel(n_lat, a_ref, w_ref, r_ref, g_ref, o_ref, w_bf):
    @pl.when(pl.program_id(1) == 0)
    def _():
        w_bf[...] = w_ref[0].astype(BF)

    tm = o_ref.shape[0]
    acc = jnp.dot(a_ref[...], w_bf[...], preferred_element_type=F32)
    g = _row_gate(g_ref, pl.program_id(1) * tm, acc.shape, n_lat)
    o_ref[...] = r_ref[...] + g * acc


def _proj_resid(a, w, resid, gate2, n_lat):
    M, K = a.shape
    N = w.shape[2]
    tm = _pick(M, 1100, BF16_SUBLANES)
    tn = _pick(N, 512, LANES)
    return pl.pallas_call(
        functools.partial(_proj_resid_kernel, n_lat),
        grid=(N // tn, M // tm),
        in_specs=[pl.BlockSpec((tm, K), lambda j, i: (i, 0)),
                  pl.BlockSpec((1, K, tn), lambda j, i: (0, 0, j)),
                  pl.BlockSpec((tm, tn), lambda j, i: (i, j)),
                  pl.BlockSpec((2, tn), lambda j, i: (0, j))],
        out_specs=pl.BlockSpec((tm, tn), lambda j, i: (i, j)),
        out_shape=jax.ShapeDtypeStruct((M, N), F32),
        scratch_shapes=[pltpu.VMEM((K, tn), BF)],
        compiler_params=_params("arbitrary", "arbitrary"),
        name="proj_resid",
    )(a, w, resid, gate2)


def _gu_kernel(n_real, a_ref, wg_ref, wu_ref, o_ref):
    j = pl.program_id(1)

    @pl.when(j < n_real)
    def _():
        a = a_ref[...]
        g = jnp.dot(a, wg_ref[0].astype(BF), preferred_element_type=F32)
        u = jnp.dot(a, wu_ref[0].astype(BF), preferred_element_type=F32)
        o_ref[...] = ((g * _sigmoid(g)) * u).astype(BF)

    @pl.when(j >= n_real)
    def _():
        o_ref[...] = jnp.zeros_like(o_ref)


def _ffn_gu(u, w_gu, f_pad):
    R, D = u.shape
    F = w_gu.shape[2] // 2
    tm = _pick(R, 1100, BF16_SUBLANES)
    tn = _pick(math.gcd(F, f_pad), 256, LANES)
    nf = F // tn
    col = lambda j: jnp.minimum(j, nf - 1)
    return pl.pallas_call(
        functools.partial(_gu_kernel, nf),
        grid=(R // tm, f_pad // tn),
        in_specs=[pl.BlockSpec((tm, D), lambda i, j: (i, 0)),
                  pl.BlockSpec((1, D, tn), lambda i, j: (0, 0, col(j))),
                  pl.BlockSpec((1, D, tn), lambda i, j: (0, 0, col(j) + nf))],
        out_specs=pl.BlockSpec((tm, tn), lambda i, j: (i, j)),
        out_shape=jax.ShapeDtypeStruct((R, f_pad), BF),
        compiler_params=_params("arbitrary", "arbitrary"),
        name="ffn_gu",
    )(u, w_gu, w_gu)


def _down_kernel(n_lat, a_ref, w_ref, r_ref, g_ref, o_ref, acc_ref):
    k = pl.program_id(2)

    @pl.when(k == 0)
    def _():
        acc_ref[...] = jnp.zeros_like(acc_ref)

    acc_ref[...] += jnp.dot(a_ref[...], w_ref[...], preferred_element_type=F32)

    @pl.when(k == pl.num_programs(2) - 1)
    def _():
        acc = acc_ref[...]
        g = _row_gate(g_ref, pl.program_id(0) * acc.shape[0], acc.shape, n_lat)
        o_ref[...] = r_ref[...] + g * acc


def _ffn_down(h, w_down, resid, gate2, n_lat):
    R, Fp = h.shape
    D = w_down.shape[1]
    tm = _pick(R, 1100, BF16_SUBLANES)
    tn = _pick(D, 1024, LANES)
    tk = _pick(Fp, 2816, LANES)
    return pl.pallas_call(
        functools.partial(_down_kernel, n_lat),
        grid=(R // tm, D // tn, Fp // tk),
        in_specs=[pl.BlockSpec((tm, tk), lambda i, j, k: (i, k)),
                  pl.BlockSpec((tk, tn), lambda i, j, k: (k, j)),
                  pl.BlockSpec((tm, tn), lambda i, j, k: (i, j)),
                  pl.BlockSpec((2, tn), lambda i, j, k: (0, j))],
        out_specs=pl.BlockSpec((tm, tn), lambda i, j, k: (i, j)),
        out_shape=jax.ShapeDtypeStruct((R, D), F32),
        scratch_shapes=[pltpu.VMEM((tm, tn), F32)],
        compiler_params=_params("arbitrary", "arbitrary", "arbitrary"),
        name="ffn_down",
    )(h, w_down, resid, gate2)


def _qk_kernel(n_q_tiles, n_row_tiles, q_scale, a_ref, w_hbm, gain_ref, cos_ref, sa_ref, sb_ref, o_ref,
               wstage, w_bf, sem, acc_even, acc_odd):
    j, i = pl.program_id(0), pl.program_id(1)
    _staged_column_weights(w_hbm, lambda jj: (jj,), wstage, w_bf, sem)
    post = jnp.where(j < n_q_tiles, q_scale, 1.0)

    def project(acc_ref):
        acc_ref[...] = jnp.dot(a_ref[...], w_bf[0], preferred_element_type=F32)

    def finish(acc_ref):
        gain = gain_ref[0]
        cos, sa, sb = cos_ref[...], sa_ref[...], sb_ref[...]
        half = LANES // 4
        for g in range(acc_ref.shape[1] // LANES):
            t = acc_ref[:, g * LANES:(g + 1) * LANES]
            t = t * lax.rsqrt(jnp.mean(t * t, axis=-1, keepdims=True) + NORM_EPS)
            t = t * gain
            t = t * cos + pltpu.roll(t, LANES - half, 1) * sa + pltpu.roll(t, half, 1) * sb
            o_ref[:, g * LANES:(g + 1) * LANES] = (t * post).astype(BF)

    @pl.when(i == 0)
    def _():
        project(acc_even)

    @pl.when((i > 0) & (i < n_row_tiles) & (i % 2 == 0))
    def _():
        project(acc_even)
        finish(acc_odd)

    @pl.when((i < n_row_tiles) & (i % 2 == 1))
    def _():
        project(acc_odd)
        finish(acc_even)

    @pl.when(i == n_row_tiles)
    def _():
        finish(acc_odd if n_row_tiles % 2 == 0 else acc_even)


def _qk_proj(u, w_qkv, qk_gain2, cos, sa, sb, q_scale):
    R, D = u.shape
    tm = _pick(R, 1100, BF16_SUBLANES)
    tn = _pick(D, 512, LANES)
    nq, ni = D // tn, R // tm
    done = lambda i: jnp.maximum(i - 1, 0)
    tab = pl.BlockSpec((tm, LANES), lambda j, i: (done(i), 0))
    return pl.pallas_call(
        functools.partial(_qk_kernel, nq, ni, q_scale),
        grid=(2 * nq, ni + 1),
        in_specs=[pl.BlockSpec((tm, D), lambda j, i: (jnp.minimum(i, ni - 1), 0)),
                  pl.BlockSpec(memory_space=pl.ANY),
                  pl.BlockSpec((1, 1, LANES), lambda j, i: (jnp.where(j < nq, 0, 1), 0, 0)),
                  tab, tab, tab],
        out_specs=pl.BlockSpec((tm, tn), lambda j, i: (done(i), j)),
        out_shape=jax.ShapeDtypeStruct((R, 2 * D), BF),
        scratch_shapes=[pltpu.VMEM((1, D, tn), F32), pltpu.VMEM((1, D, tn), BF), pltpu.SemaphoreType.DMA((1,)),
                        pltpu.VMEM((tm, tn), F32), pltpu.VMEM((tm, tn), F32)],
        compiler_params=_params("arbitrary", "arbitrary"),
        name="qk_proj",
    )(u, w_qkv, qk_gain2, cos, sa, sb)


def _v_kernel(first_col, a_ref, w_hbm, o_ref, wstage, w_bf, sem):
    _staged_column_weights(w_hbm, lambda jj: (first_col + jj,), wstage, w_bf, sem)
    o_ref[...] = jnp.dot(a_ref[...], w_bf[0], preferred_element_type=F32).astype(BF)


def _v_proj(u, w_qkv):
    R, D = u.shape
    tm = _pick(R, 1100, BF16_SUBLANES)
    tn = _pick(D, 512, LANES)
    return pl.pallas_call(
        functools.partial(_v_kernel, 2 * D // tn),
        grid=(D // tn, R // tm),
        in_specs=[pl.BlockSpec((tm, D), lambda j, i: (i, 0)), pl.BlockSpec(memory_space=pl.ANY)],
        out_specs=pl.BlockSpec((tm, tn), lambda j, i: (i, j)),
        out_shape=jax.ShapeDtypeStruct((R, D), BF),
        scratch_shapes=[pltpu.VMEM((1, D, tn), F32), pltpu.VMEM((1, D, tn), BF), pltpu.SemaphoreType.DMA((1,))],
        compiler_params=_params("arbitrary", "arbitrary"),
        name="v_proj",
    )(u, w_qkv)


def _attn_kernel(n_lat, tk, lambda_init, q_ref, k_ref, v_ref, lam_ref, sg_ref, o_ref,
                 sa_ref, sb_ref, sc_ref, m_ref, l_ref, acc_ref):
    hd = q_ref.shape[1] // 2
    n_kv = k_ref.shape[0]
    n_chunks = n_lat // tk
    has_ctx = n_kv > n_lat
    m_ref[...] = jnp.full_like(m_ref, -jnp.inf)
    l_ref[...] = jnp.zeros_like(l_ref)
    acc_ref[...] = jnp.zeros_like(acc_ref)

    def scores(kc, dst):
        for m in range(2):
            dst[m] = lax.dot_general(q_ref[:, m * hd:(m + 1) * hd], kc[:, m * hd:(m + 1) * hd],
                                     (((1,), (1,)), ((), ())), preferred_element_type=F32)

    def absorb(src, vc):
        scaled = []
        for m in range(2):
            s = src[m]
            m_old = m_ref[m]
            m_new = jnp.maximum(m_old, jnp.max(s, axis=-1, keepdims=True))
            alpha = jnp.exp2(m_old - m_new)
            p = jnp.exp2(s - m_new)
            l_ref[m] = alpha * l_ref[m] + jnp.sum(p, axis=-1, keepdims=True)
            m_ref[m] = m_new
            scaled.append((alpha, p.astype(BF)))
        for m in range(2):
            alpha, p = scaled[m]
            acc_ref[m] = alpha * acc_ref[m] + jnp.dot(p, vc, preferred_element_type=F32)

    def k_chunk(c):
        return k_ref[pl.ds(pl.multiple_of(c * tk, tk), tk), :]

    def v_chunk(c):
        return v_ref[pl.ds(pl.multiple_of(c * tk, tk), tk), :]

    scores(k_chunk(0), sa_ref)

    def pair(i, carry):
        c = 2 * i
        scores(k_chunk(c + 1), sb_ref)
        absorb(sa_ref, v_chunk(c))
        scores(k_chunk(c + 2), sa_ref)
        absorb(sb_ref, v_chunk(c + 1))
        return carry

    n_pairs = (n_chunks - 1) // 2
    lax.fori_loop(0, n_pairs, pair, 0)
    c = 2 * n_pairs
    if n_chunks - c == 2:
        scores(k_chunk(c + 1), sb_ref)
        absorb(sa_ref, v_chunk(c))
        if has_ctx:
            scores(k_ref[n_lat:n_kv, :], sc_ref)
        absorb(sb_ref, v_chunk(c + 1))
    else:
        if has_ctx:
            scores(k_ref[n_lat:n_kv, :], sc_ref)
        absorb(sa_ref, v_chunk(c))
    if has_ctx:
        absorb(sc_ref, v_ref[n_lat:n_kv, :])

    lf = lam_ref[...]
    lam = (jnp.exp(jnp.sum(lf[0:1] * lf[1:2], axis=-1, keepdims=True))
           - jnp.exp(jnp.sum(lf[2:3] * lf[3:4], axis=-1, keepdims=True)) + lambda_init)
    o = acc_ref[0] / l_ref[0] - lam * (acc_ref[1] / l_ref[1])
    o = o * lax.rsqrt(jnp.mean(o * o, axis=-1, keepdims=True) + NORM_EPS)
    o_ref[...] = ((o * sg_ref[...]) * (1.0 - lambda_init)).astype(BF)


def _diff_attention(qk, v, lambdas, subln_gain, n_lat, n_heads, lambda_init):
    R, D = v.shape
    vd = D // n_heads
    tq = _pick(n_lat, 512, BF16_SUBLANES)
    tk = _pick(n_lat, 1024, LANES)
    return pl.pallas_call(
        functools.partial(_attn_kernel, n_lat, tk, lambda_init),
        grid=(n_heads, n_lat // tq),
        in_specs=[pl.BlockSpec((tq, vd), lambda h, i: (i, h)),
                  pl.BlockSpec((R, vd), lambda h, i: (0, n_heads + h)),
                  pl.BlockSpec((R, vd), lambda h, i: (0, h)),
                  pl.BlockSpec(lambdas.shape, lambda h, i: (0, 0)),
                  pl.BlockSpec((1, vd), lambda h, i: (0, 0))],
        out_specs=pl.BlockSpec((tq, vd), lambda h, i: (i, h)),
        out_shape=jax.ShapeDtypeStruct((n_lat, D), BF),
        scratch_shapes=[pltpu.VMEM((2, tq, tk), F32), pltpu.VMEM((2, tq, tk), F32),
                        pltpu.VMEM((2, tq, (R - n_lat) or LANES), F32),
                        pltpu.VMEM((2, tq, 1), F32), pltpu.VMEM((2, tq, 1), F32),
                        pltpu.VMEM((2, tq, vd), F32)],
        compiler_params=_params("arbitrary", "arbitrary"),
        name="diff_attention",
    )(qk, qk, v, lambdas, subln_gain.reshape(1, vd))


def _router_kernel(n_experts, x_ref, g_ref, sh_ref, sc_ref, w_ref, comb_ref, sel_ref):
    u = _modnorm_rows(x_ref[...], g_ref[...], sh_ref[...], sc_ref[...]).astype(BF)
    logits = jnp.dot(u, w_ref[...], preferred_element_type=F32)
    lane = lax.broadcasted_iota(jnp.int32, logits.shape, 1)
    neg = jnp.float32(-jnp.inf)
    l1 = jnp.where(lane < n_experts, logits, neg)
    m1 = jnp.max(l1, axis=-1, keepdims=True)
    i1 = jnp.min(jnp.where(l1 == m1, lane, LANES), axis=-1, keepdims=True)
    sel1 = lane == i1
    l2 = jnp.where(sel1, neg, l1)
    m2 = jnp.max(l2, axis=-1, keepdims=True)
    i2 = jnp.min(jnp.where(l2 == m2, lane, LANES), axis=-1, keepdims=True)
    sel2 = lane == i2
    e2 = jnp.exp(m2 - m1)
    den = 1.0 + e2
    comb_ref[...] = jnp.where(sel1, 1.0 / den, 0.0) + jnp.where(sel2, e2 / den, 0.0)
    sel_ref[...] = (sel1 | sel2).astype(jnp.int32)


def _router(x, gain, shift, scale, w_router_pad, n_experts):
    S, D = x.shape
    tm = _pick(S, 512, 8)
    vec = pl.BlockSpec((1, D), lambda i: (0, 0))
    out = pl.BlockSpec((tm, LANES), lambda i: (i, 0))
    return pl.pallas_call(
        functools.partial(_router_kernel, n_experts),
        grid=(S // tm,),
        in_specs=[pl.BlockSpec((tm, D), lambda i: (i, 0)), vec, vec, vec,
                  pl.BlockSpec((D, LANES), lambda i: (0, 0))],
        out_specs=[out, out],
        out_shape=[jax.ShapeDtypeStruct((S, LANES), F32), jax.ShapeDtypeStruct((S, LANES), jnp.int32)],
        compiler_params=_params("arbitrary"),
        name="moe_router",
    )(x, gain.reshape(1, D), shift.reshape(1, D), scale.reshape(1, D), w_router_pad)


def _row_copy(src_hbm, row, dst, r, sem):
    return pltpu.make_async_copy(src_hbm.at[pl.ds(row, 1)], dst.at[pl.ds(r, 1)], sem)


ROW_DMA_UNROLL = 8


def _gather_norm_kernel(tok_ref, nrows_ref, x_hbm, g_ref, sh_ref, sc_ref, o_ref, buf, sem):
    tg = buf.shape[1]
    t = pl.program_id(0)

    def used(tile):
        return tile * tg < nrows_ref[0]

    def issue(tile):
        slot = tile % 2

        def one(r, c):
            _row_copy(x_hbm, tok_ref[tile * tg + r], buf.at[slot], r, sem.at[slot]).start()
            return c

        lax.fori_loop(0, tg, one, 0, unroll=ROW_DMA_UNROLL)

    def wait(tile):
        slot = tile % 2

        def one(r, c):
            _row_copy(x_hbm, 0, buf.at[slot], r, sem.at[slot]).wait()
            return c

        lax.fori_loop(0, tg, one, 0, unroll=ROW_DMA_UNROLL)

    @pl.when((t == 0) & used(0))
    def _():
        issue(0)

    @pl.when((t + 1 < pl.num_programs(0)) & used(t + 1))
    def _():
        issue(t + 1)

    @pl.when(used(t))
    def _():
        wait(t)
        o_ref[...] = _modnorm_rows(buf[t % 2], g_ref[...], sh_ref[...], sc_ref[...]).astype(BF)

    @pl.when(jnp.logical_not(used(t)))
    def _():
        o_ref[...] = jnp.zeros_like(o_ref)


def _gather_norm(x, row_token, n_rows_used, gain, shift, scale):
    D = x.shape[1]
    Rs = row_token.shape[0]
    tg = _pick(Rs, 256, BF16_SUBLANES)
    vec = pl.BlockSpec((1, D), lambda t, tok, n: (0, 0))
    return pl.pallas_call(
        _gather_norm_kernel,
        grid_spec=pltpu.PrefetchScalarGridSpec(
            num_scalar_prefetch=2,
            grid=(Rs // tg,),
            in_specs=[pl.BlockSpec(memory_space=pl.ANY), vec, vec, vec],
            out_specs=pl.BlockSpec((tg, D), lambda t, tok, n: (t, 0)),
            scratch_shapes=[pltpu.VMEM((2, tg, D), F32), pltpu.SemaphoreType.DMA((2,))]),
        out_shape=jax.ShapeDtypeStruct((Rs, D), BF),
        compiler_params=_params("arbitrary"),
        name="moe_gather_norm",
    )(row_token, n_rows_used, x, gain.reshape(1, D), shift.reshape(1, D), scale.reshape(1, D))


SCHED_EXPERT, SCHED_RUN_START, SCHED_FIRST_RUN, SCHED_NEXT_EXPERT, SCHED_LAST_RUN = range(5)


def _staged_expert_weights(sched_ref, cnt_ref, w_hbm, col_blocks, wstage, w_bf, sem):
    j, t = pl.program_id(0), pl.program_id(1)
    nj, n_t = pl.num_programs(0), pl.num_programs(1)
    tn = wstage.shape[-1]

    def copies(e, jj):
        return [pltpu.make_async_copy(w_hbm.at[0, e, :, pl.ds(pl.multiple_of(cb * tn, tn), tn)],
                                      wstage.at[k], sem.at[k])
                for k, cb in enumerate(col_blocks(jj))]

    def sched(row):
        return sched_ref[row * n_t + t]

    @pl.when(sched(SCHED_RUN_START) == 1)
    def _():
        e = sched(SCHED_EXPERT)

        @pl.when((j == 0) & (sched(SCHED_FIRST_RUN) == 1))
        def _():
            for cp in copies(e, j):
                cp.start()

        for k, cp in enumerate(copies(e, j)):
            cp.wait()
            w_bf[k] = wstage[k].astype(BF)

        next_j = j + sched(SCHED_LAST_RUN)

        @pl.when(next_j < nj)
        def _():
            for cp in copies(sched(SCHED_NEXT_EXPERT), next_j):
                cp.start()


def _moe_gu_kernel(sched_ref, cnt_ref, a_ref, w_hbm, o_ref, wstage, w_bf, sem):
    t = pl.program_id(1)
    nj = pl.num_programs(0)

    @pl.when(t < cnt_ref[0])
    def _():
        _staged_expert_weights(sched_ref, cnt_ref, w_hbm, lambda jj: (jj, jj + nj), wstage, w_bf, sem)
        a = a_ref[...]
        g = jnp.dot(a, w_bf[0], preferred_element_type=F32)
        u = jnp.dot(a, w_bf[1], preferred_element_type=F32)
        o_ref[...] = ((g * _sigmoid(g)) * u).astype(BF)

    @pl.when(t >= cnt_ref[0])
    def _():
        o_ref[...] = jnp.zeros_like(o_ref)


def _moe_gu(a, w_gu, sched, counts, tm):
    Rs, D = a.shape
    F = w_gu.shape[3] // 2
    tn = _pick(F, 512, LANES)
    tile = lambda t, cnt: jnp.minimum(t, cnt[0] - 1)
    return pl.pallas_call(
        _moe_gu_kernel,
        grid_spec=pltpu.PrefetchScalarGridSpec(
            num_scalar_prefetch=2,
            grid=(F // tn, Rs // tm),
            in_specs=[pl.BlockSpec((tm, D), lambda j, t, sc, cnt: (tile(t, cnt), 0)),
                      pl.BlockSpec(memory_space=pl.ANY)],
            out_specs=pl.BlockSpec((tm, tn), lambda j, t, sc, cnt: (t, j)),
            scratch_shapes=[pltpu.VMEM((2, D, tn), F32), pltpu.VMEM((2, D, tn), BF),
                            pltpu.SemaphoreType.DMA((2,))]),
        out_shape=jax.ShapeDtypeStruct((Rs, F), BF),
        compiler_params=_params("arbitrary", "arbitrary"),
        name="moe_gu",
    )(sched, counts, a, w_gu)


def _moe_down_kernel(sched_ref, cnt_ref, a_ref, w_hbm, o_ref, wstage, w_bf, sem):
    t = pl.program_id(1)

    @pl.when(t < cnt_ref[0])
    def _():
        _staged_expert_weights(sched_ref, cnt_ref, w_hbm, lambda jj: (jj,), wstage, w_bf, sem)
        o_ref[...] = jnp.dot(a_ref[...], w_bf[0], preferred_element_type=F32)

    @pl.when(t >= cnt_ref[0])
    def _():
        o_ref[...] = jnp.zeros_like(o_ref)


def _moe_down(h, w_down, sched, counts, tm):
    Rs, F = h.shape
    D = w_down.shape[3]
    tn = _pick(D, 512, LANES)
    tile = lambda t, cnt: jnp.minimum(t, cnt[0] - 1)
    return pl.pallas_call(
        _moe_down_kernel,
        grid_spec=pltpu.PrefetchScalarGridSpec(
            num_scalar_prefetch=2,
            grid=(D // tn, Rs // tm),
            in_specs=[pl.BlockSpec((tm, F), lambda j, t, sc, cnt: (tile(t, cnt), 0)),
                      pl.BlockSpec(memory_space=pl.ANY)],
            out_specs=pl.BlockSpec((tm, tn), lambda j, t, sc, cnt: (t, j)),
            scratch_shapes=[pltpu.VMEM((1, F, tn), F32), pltpu.VMEM((1, F, tn), BF),
                            pltpu.SemaphoreType.DMA((1,))]),
        out_shape=jax.ShapeDtypeStruct((Rs, D), F32),
        compiler_params=_params("arbitrary", "arbitrary"),
        name="moe_down",
    )(sched, counts, h, w_down)


def _combine_kernel(pa_ref, pb_ref, y_hbm, x_ref, g_ref, wa_ref, wb_ref, o_ref, ya, yb, sem):
    tt = ya.shape[1]
    t = pl.program_id(0)

    def issue(tile):
        slot = tile % 2

        def one(r, c):
            _row_copy(y_hbm, pa_ref[tile * tt + r], ya.at[slot], r, sem.at[0, slot]).start()
            _row_copy(y_hbm, pb_ref[tile * tt + r], yb.at[slot], r, sem.at[1, slot]).start()
            return c

        lax.fori_loop(0, tt, one, 0, unroll=ROW_DMA_UNROLL)

    def wait(tile):
        slot = tile % 2

        def one(r, c):
            _row_copy(y_hbm, 0, ya.at[slot], r, sem.at[0, slot]).wait()
            _row_copy(y_hbm, 0, yb.at[slot], r, sem.at[1, slot]).wait()
            return c

        lax.fori_loop(0, tt, one, 0, unroll=ROW_DMA_UNROLL)

    @pl.when(t == 0)
    def _():
        issue(0)

    @pl.when(t + 1 < pl.num_programs(0))
    def _():
        issue(t + 1)

    wait(t)
    slot = t % 2
    o_ref[...] = x_ref[...] + g_ref[...] * (wa_ref[...] * ya[slot] + wb_ref[...] * yb[slot])


def _moe_combine(x, y, pos_a, pos_b, w_a, w_b, gate):
    S, D = x.shape
    tt = _pick(S, 256, 8)
    col = pl.BlockSpec((tt, 1), lambda t, pa, pb: (t, 0))
    return pl.pallas_call(
        _combine_kernel,
        grid_spec=pltpu.PrefetchScalarGridSpec(
            num_scalar_prefetch=2,
            grid=(S // tt,),
            in_specs=[pl.BlockSpec(memory_space=pl.ANY),
                      pl.BlockSpec((tt, D), lambda t, pa, pb: (t, 0)),
                      pl.BlockSpec((1, D), lambda t, pa, pb: (0, 0)),
                      col, col],
            out_specs=pl.BlockSpec((tt, D), lambda t, pa, pb: (t, 0)),
            scratch_shapes=[pltpu.VMEM((2, tt, D), F32), pltpu.VMEM((2, tt, D), F32),
                            pltpu.SemaphoreType.DMA((2, 2))]),
        out_shape=jax.ShapeDtypeStruct((S, D), F32),
        compiler_params=_params("arbitrary"),
        name="moe_combine",
    )(pos_a, pos_b, y, x, gate.reshape(1, D), w_a, w_b)


def _routing_tables(sel, comb, tm):
    S, E = sel.shape
    Rs = TOP_K * S + E * tm
    seli = sel.astype(jnp.int32)
    cnt = jnp.sum(seli, axis=0)
    padded = (cnt + tm - 1) // tm * tm
    gend = jnp.cumsum(padded)
    gstart = gend - padded
    pos = gstart[None, :] + jnp.cumsum(seli, axis=0) - seli
    pos_a = jnp.min(jnp.where(sel, pos, Rs), axis=1).astype(jnp.int32)
    pos_b = jnp.max(jnp.where(sel, pos, -1), axis=1).astype(jnp.int32)
    w_a = jnp.sum(jnp.where(sel & (pos == pos_a[:, None]), comb, 0.0), axis=1, keepdims=True)
    w_b = jnp.sum(jnp.where(sel & (pos == pos_b[:, None]), comb, 0.0), axis=1, keepdims=True)
    tok = jnp.arange(S, dtype=jnp.int32)
    row_token = jnp.zeros((Rs,), jnp.int32).at[jnp.concatenate([pos_a, pos_b])].set(
        jnp.concatenate([tok, tok]), mode="drop")
    n_t = Rs // tm
    tix = jnp.arange(n_t, dtype=jnp.int32)
    tile_expert = jnp.minimum(
        jnp.sum((tix[:, None] * tm >= gend[None, :]).astype(jnp.int32), axis=1), E - 1).astype(jnp.int32)
    n_rows = gend[-1].astype(jnp.int32)
    n_tiles = n_rows // tm
    prev_expert = jnp.concatenate([jnp.full((1,), -1, jnp.int32), tile_expert[:-1]])
    first = (tix < n_tiles) & (tile_expert != prev_expert)
    first_run = first & (tix == 0)
    later_first = first[None, :] & (tix[None, :] > tix[:, None])
    next_first = jnp.min(jnp.where(later_first, tix[None, :], n_t), axis=1)
    last_run = next_first >= n_t
    next_expert = jnp.where(last_run, tile_expert[0], jnp.sum(
        jnp.where(tix[None, :] == next_first[:, None], tile_expert[None, :], 0), axis=1))
    sched = jnp.concatenate([tile_expert, first.astype(jnp.int32), first_run.astype(jnp.int32), next_expert,
                             last_run.astype(jnp.int32)]).astype(jnp.int32)
    return row_token, pos_a, pos_b, w_a, w_b, sched, n_tiles.reshape(1), n_rows.reshape(1)


def _rope_tables(n_lat, n_rows, head_dim):
    n_freq = head_dim // 4
    s = jnp.arange(n_lat, dtype=jnp.int32)
    inv = ROPE_BASE ** (-jnp.arange(n_freq, dtype=F32) / n_freq)
    ang = jnp.stack([s // GRID_W, s % GRID_W], axis=-1).astype(F32)[..., None] * inv
    cos, sin = jnp.cos(ang), jnp.sin(ang)
    zero = jnp.zeros_like(sin)
    cos_t = jnp.stack([cos, cos], axis=2).reshape(n_lat, head_dim)
    sa_t = jnp.stack([-sin, zero], axis=2).reshape(n_lat, head_dim)
    sb_t = jnp.stack([zero, sin], axis=2).reshape(n_lat, head_dim)
    pad = ((0, n_rows - n_lat), (0, 0))
    return jnp.pad(cos_t, pad, constant_values=1.0), jnp.pad(sa_t, pad), jnp.pad(sb_t, pad)


def kernel(x, c, ctx, c_ctx, w_mod, b_mod, norm_gain, conv_w_in, conv_w, conv_w_out, attn_w_qkv, attn_w_o,
           attn_q_gain, attn_k_gain, attn_lambdas, attn_subln_gain, ffn_w_gu, ffn_w_down, moe_router,
           moe_w_gu, moe_w_down):
    assert x.shape[0] == 1 and w_mod.shape[0] == 2
    S, D = x.shape[1], x.shape[2]
    C = ctx.shape[1]
    R = S + C
    head_dim = attn_q_gain.shape[-1]
    n_heads = D // (2 * head_dim)
    E = moe_router.shape[-1]
    F = ffn_w_down.shape[1]
    f_pad = _round_up(F, 512)

    cc = jnp.stack([c[0], c_ctx]).reshape(2, D, 1)
    mod = _mod_vectors(cc, w_mod, b_mod)[:, :2, :].reshape(2, 2, N_MOD, D)

    xa = jnp.concatenate([x[0], ctx[0]], axis=0)

    m0 = mod[0]
    u = _modnorm(xa, norm_gain[0, 0], m0[:, 0], m0[:, 1], S, R)
    b, cv = _conv_in(u, conv_w_in)
    bz = _conv_mix(b, cv, conv_w[0], S)
    xa = _proj_resid(bz, conv_w_out, xa, m0[:, 2], S)
    u = _modnorm(xa, norm_gain[0, 1], m0[:, 3], m0[:, 4], S, R)
    w_dn = jnp.pad(ffn_w_down[0], ((0, f_pad - F), (0, 0))).astype(BF)
    h = _ffn_gu(u, ffn_w_gu, f_pad)
    xa = _ffn_down(h, w_dn, xa, m0[:, 5], S)

    m1 = mod[1]
    lambda_init = 0.8 - 0.6 * math.exp(-0.3 * 1)
    u = _modnorm(xa, norm_gain[1, 0], m1[:, 0], m1[:, 1], S, R)
    cos, sa, sb = _rope_tables(S, R, head_dim)
    qk_gain = jnp.stack([attn_q_gain[0], attn_k_gain[0]]).reshape(2, 1, head_dim)
    qk = _qk_proj(u, attn_w_qkv, qk_gain, cos, sa, sb, head_dim ** -0.5 * math.log2(math.e))
    v = _v_proj(u, attn_w_qkv)
    o = _diff_attention(qk, v, attn_lambdas[0], attn_subln_gain[0], S, n_heads, lambda_init)
    x1 = _proj_resid(o, attn_w_o, xa, m1[:, 2], S)

    tm = _pick(S, 512, BF16_SUBLANES)
    w_r = jnp.pad(moe_router[0], ((0, 0), (0, LANES - E))).astype(BF)
    comb, selm = _router(x1, norm_gain[1, 1], m1[0, 3], m1[0, 4], w_r, E)
    row_token, pos_a, pos_b, w_a, w_b, sched, counts, n_rows = _routing_tables(
        selm[:, :E] > 0, comb[:, :E], tm)
    us = _gather_norm(x1, row_token, n_rows, norm_gain[1, 1], m1[0, 3], m1[0, 4])
    hs = _moe_gu(us, moe_w_gu, sched, counts, tm)
    ys = _moe_down(hs, moe_w_down, sched, counts, tm)
    out = _moe_combine(x1, ys, pos_a, pos_b, w_a, w_b, m1[0, 5])
    return out[None]
```

```python
import functools
import math

import jax
import jax.numpy as jnp
from jax import lax
from jax.experimental import pallas as pl
from jax.experimental.pallas import tpu as pltpu

BF = jnp.bfloat16
F32 = jnp.float32

GRID_W = 64
ROPE_BASE = 10000.0
NORM_EPS = 1e-6
TOP_K = 2
N_MOD = 6

LANES = 128
BF16_SUBLANES = 16
VMEM_LIMIT_BYTES = 56 * 1024 * 1024


def _pick(n, target, align):
    best = None
    d = align
    while d <= min(n, target):
        if n % d == 0:
            best = d
        d += align
    return best if best is not None else n


def _round_up(n, m):
    return (n + m - 1) // m * m


def _params(*sem):
    return pltpu.CompilerParams(dimension_semantics=sem, vmem_limit_bytes=VMEM_LIMIT_BYTES)


def _sigmoid(x):
    return 1.0 / (1.0 + jnp.exp(-x))


def _mod_kernel(c_ref, w_ref, b_ref, o_ref, s_ref):
    @pl.when((pl.program_id(0) == 0) & (pl.program_id(1) == 0))
    def _():
        c = c_ref[...]
        s_ref[...] = jnp.broadcast_to(c * _sigmoid(c), s_ref.shape)

    o_ref[...] = jnp.zeros_like(o_ref)
    for g in range(w_ref.shape[2] // LANES):
        w = w_ref[0, :, g * LANES:(g + 1) * LANES]
        for r in range(2):
            o_ref[0, r:r + 1, g * LANES:(g + 1) * LANES] = (
                jnp.sum(w * s_ref[r], axis=0, keepdims=True) + b_ref[0, :, g * LANES:(g + 1) * LANES])


def _mod_vectors(cc, w_mod, b_mod):
    L, D, N = w_mod.shape
    tn = _pick(N, 512, LANES)
    return pl.pallas_call(
        _mod_kernel,
        grid=(L, N // tn),
        in_specs=[pl.BlockSpec((2, D, 1), lambda l, j: (0, 0, 0)),
                  pl.BlockSpec((1, D, tn), lambda l, j: (l, 0, j)),
                  pl.BlockSpec((1, 1, tn), lambda l, j: (l, 0, j))],
        out_specs=pl.BlockSpec((1, 8, tn), lambda l, j: (l, 0, j)),
        out_shape=jax.ShapeDtypeStruct((L, 8, N), F32),
        scratch_shapes=[pltpu.VMEM((2, D, LANES), F32)],
        compiler_params=_params("arbitrary", "arbitrary"),
        name="mod_vectors",
    )(cc, w_mod, b_mod.reshape(L, 1, N))


def _modnorm_rows(x, gain, shift, scale):
    ms = jnp.mean(x * x, axis=-1, keepdims=True)
    xn = x * lax.rsqrt(ms + NORM_EPS)
    return (xn * gain) * (1.0 + scale) + shift


def _modnorm_kernel(x_ref, g_ref, sh_ref, sc_ref, o_ref):
    o_ref[...] = _modnorm_rows(x_ref[...], g_ref[...], sh_ref[0], sc_ref[0]).astype(BF)


def _modnorm(x, gain, shift2, scale2, n_lat, n_rows):
    D = x.shape[1]
    tr = _pick(math.gcd(n_lat, n_rows), 256, BF16_SUBLANES)
    n_lat_tiles = n_lat // tr
    grp = lambda i: (jnp.where(i < n_lat_tiles, 0, 1), 0, 0)
    return pl.pallas_call(
        _modnorm_kernel,
        grid=(n_rows // tr,),
        in_specs=[pl.BlockSpec((tr, D), lambda i: (i, 0)),
                  pl.BlockSpec((1, D), lambda i: (0, 0)),
                  pl.BlockSpec((1, 1, D), grp),
                  pl.BlockSpec((1, 1, D), grp)],
        out_specs=pl.BlockSpec((tr, D), lambda i: (i, 0)),
        out_shape=jax.ShapeDtypeStruct((n_rows, D), BF),
        compiler_params=_params("arbitrary"),
        name="modnorm",
    )(x, gain.reshape(1, D), shift2.reshape(2, 1, D), scale2.reshape(2, 1, D))


def _staged_column_weights(w_hbm, col_blocks, wstage, w_bf, sem):
    j, i = pl.program_id(0), pl.program_id(1)
    tn = wstage.shape[-1]

    def copies(jj):
        return [pltpu.make_async_copy(w_hbm.at[0, :, pl.ds(pl.multiple_of(cb * tn, tn), tn)],
                                      wstage.at[k], sem.at[k])
                for k, cb in enumerate(col_blocks(jj))]

    @pl.when(i == 0)
    def _():
        @pl.when(j == 0)
        def _():
            for cp in copies(j):
                cp.start()

        for k, cp in enumerate(copies(j)):
            cp.wait()
            w_bf[k] = wstage[k].astype(BF)

        @pl.when(j + 1 < pl.num_programs(0))
        def _():
            for cp in copies(j + 1):
                cp.start()


def _conv_in_kernel(a_ref, w_hbm, b_ref, cv_ref, wstage, w_bf, sem):
    nj = pl.num_programs(0)
    _staged_column_weights(w_hbm, lambda jj: (jj, jj + nj, jj + 2 * nj), wstage, w_bf, sem)
    a = a_ref[...]
    b_ref[...] = jnp.dot(a, w_bf[0], preferred_element_type=F32).astype(BF)
    c = jnp.dot(a, w_bf[1], preferred_element_type=F32)
    v = jnp.dot(a, w_bf[2], preferred_element_type=F32)
    cv_ref[...] = (c * v).astype(BF)


def _conv_in(u, w_in):
    R, D = u.shape
    tm = _pick(R, 1100, BF16_SUBLANES)
    tn = _pick(D, 256, LANES)
    return pl.pallas_call(
        _conv_in_kernel,
        grid=(D // tn, R // tm),
        in_specs=[pl.BlockSpec((tm, D), lambda j, i: (i, 0)), pl.BlockSpec(memory_space=pl.ANY)],
        out_specs=[pl.BlockSpec((tm, tn), lambda j, i: (i, j)),
                   pl.BlockSpec((tm, tn), lambda j, i: (i, j))],
        out_shape=[jax.ShapeDtypeStruct((R, D), BF), jax.ShapeDtypeStruct((R, D), BF)],
        scratch_shapes=[pltpu.VMEM((3, D, tn), F32), pltpu.VMEM((3, D, tn), BF),
                        pltpu.SemaphoreType.DMA((3,))],
        compiler_params=_params("arbitrary", "arbitrary"),
        name="conv_in",
    )(u, w_in)


def _conv_mix_kernel(n_lat_tiles, b_ref, cv_ref, pv_ref, nx_ref, w_ref, o_ref):
    i = pl.program_id(0)
    tr = cv_ref.shape[0]
    cv = cv_ref[...].astype(F32)
    top_zero = (i == 0) | (i == n_lat_tiles)
    bot_zero = (i == n_lat_tiles - 1) | (i == pl.num_programs(0) - 1)
    hp = pv_ref[BF16_SUBLANES - 1:BF16_SUBLANES, :].astype(F32) * jnp.where(top_zero, 0.0, 1.0)
    hn = nx_ref[0:1, :].astype(F32) * jnp.where(bot_zero, 0.0, 1.0)
    rows = lax.broadcasted_iota(jnp.int32, cv.shape, 0)
    prev = jnp.where(rows == 0, hp, pltpu.roll(cv, 1, 0))
    nxt = jnp.where(rows == tr - 1, hn, pltpu.roll(cv, tr - 1, 0))
    w = w_ref[...]
    z = prev * w[0:1] + cv * w[1:2] + nxt * w[2:3]
    o_ref[...] = (b_ref[...].astype(F32) * z).astype(BF)


def _conv_mix(b, cv, conv_w, n_lat):
    R, D = cv.shape
    tr = _pick(math.gcd(n_lat, R), 256, BF16_SUBLANES)
    tc = _pick(D, 1024, LANES)
    hb = tr // BF16_SUBLANES
    last = R // BF16_SUBLANES - 1
    return pl.pallas_call(
        functools.partial(_conv_mix_kernel, n_lat // tr),
        grid=(R // tr, D // tc),
        in_specs=[pl.BlockSpec((tr, tc), lambda i, j: (i, j)),
                  pl.BlockSpec((tr, tc), lambda i, j: (i, j)),
                  pl.BlockSpec((BF16_SUBLANES, tc), lambda i, j: (jnp.maximum(i * hb - 1, 0), j)),
                  pl.BlockSpec((BF16_SUBLANES, tc), lambda i, j: (jnp.minimum((i + 1) * hb, last), j)),
                  pl.BlockSpec((3, tc), lambda i, j: (0, j))],
        out_specs=pl.BlockSpec((tr, tc), lambda i, j: (i, j)),
        out_shape=jax.ShapeDtypeStruct((R, D), BF),
        compiler_params=_params("arbitrary", "arbitrary"),
        name="conv_mix",
    )(b, cv, cv, cv, conv_w)


def _row_gate(g_ref, row0, shape, n_lat):
    rows = row0 + lax.broadcasted_iota(jnp.int32, shape, 0)
    return jnp.where(rows < n_lat, g_ref[0:1, :], g_ref[1:2, :])


def _proj_resid_kernel(n_lat, a_ref, w_ref, r_ref, g_ref, o_ref, w_bf):
    @pl.when(pl.program_id(1) == 0)
    def _():
        w_bf[...] = w_ref[0].astype(BF)

    tm = o_ref.shape[0]
    acc = jnp.dot(a_ref[...], w_bf[...], preferred_element_type=F32)
    g = _row_gate(g_ref, pl.program_id(1) * tm, acc.shape, n_lat)
    o_ref[...] = r_ref[...] + g * acc


def _proj_resid(a, w, resid, gate2, n_lat):
    M, K = a.shape
    N = w.shape[2]
    tm = _pick(M, 1100, BF16_SUBLANES)
    tn = _pick(N, 512, LANES)
    return pl.pallas_call(
        functools.partial(_proj_resid_kernel, n_lat),
        grid=(N // tn, M // tm),
        in_specs=[pl.BlockSpec((tm, K), lambda j, i: (i, 0)),
                  pl.BlockSpec((1, K, tn), lambda j, i: (0, 0, j)),
                  pl.BlockSpec((tm, tn), lambda j, i: (i, j)),
                  pl.BlockSpec((2, tn), lambda j, i: (0, j))],
        out_specs=pl.BlockSpec((tm, tn), lambda j, i: (i, j)),
        out_shape=jax.ShapeDtypeStruct((M, N), F32),
        scratch_shapes=[pltpu.VMEM((K, tn), BF)],
        compiler_params=_params("arbitrary", "arbitrary"),
        name="proj_resid",
    )(a, w, resid, gate2)


def _gu_kernel(n_real, a_ref, wg_ref, wu_ref, o_ref):
    j = pl.program_id(1)

    @pl.when(j < n_real)
    def _():
        a = a_ref[...]
        g = jnp.dot(a, wg_ref[0].astype(BF), preferred_element_type=F32)
        u = jnp.dot(a, wu_ref[0].astype(BF), preferred_element_type=F32)
        o_ref[...] = ((g * _sigmoid(g)) * u).astype(BF)

    @pl.when(j >= n_real)
    def _():
        o_ref[...] = jnp.zeros_like(o_ref)


def _ffn_gu(u, w_gu, f_pad):
    R, D = u.shape
    F = w_gu.shape[2] // 2
    tm = _pick(R, 1100, BF16_SUBLANES)
    tn = _pick(math.gcd(F, f_pad), 256, LANES)
    nf = F // tn
    col = lambda j: jnp.minimum(j, nf - 1)
    return pl.pallas_call(
        functools.partial(_gu_kernel, nf),
        grid=(R // tm, f_pad // tn),
        in_specs=[pl.BlockSpec((tm, D), lambda i, j: (i, 0)),
                  pl.BlockSpec((1, D, tn), lambda i, j: (0, 0, col(j))),
                  pl.BlockSpec((1, D, tn), lambda i, j: (0, 0, col(j) + nf))],
        out_specs=pl.BlockSpec((tm, tn), lambda i, j: (i, j)),
        out_shape=jax.ShapeDtypeStruct((R, f_pad), BF),
        compiler_params=_params("arbitrary", "arbitrary"),
        name="ffn_gu",
    )(u, w_gu, w_gu)


def _down_kernel(n_lat, a_ref, w_ref, r_ref, g_ref, o_ref, acc_ref):
    k = pl.program_id(2)

    @pl.when(k == 0)
    def _():
        acc_ref[...] = jnp.zeros_like(acc_ref)

    acc_ref[...] += jnp.dot(a_ref[...], w_ref[...], preferred_element_type=F32)

    @pl.when(k == pl.num_programs(2) - 1)
    def _():
        acc = acc_ref[...]
        g = _row_gate(g_ref, pl.program_id(0) * acc.shape[0], acc.shape, n_lat)
        o_ref[...] = r_ref[...] + g * acc


def _ffn_down(h, w_down, resid, gate2, n_lat):
    R, Fp = h.shape
    D = w_down.shape[1]
    tm = _pick(R, 1100, BF16_SUBLANES)
    tn = _pick(D, 1024, LANES)
    tk = _pick(Fp, 2816, LANES)
    return pl.pallas_call(
        functools.partial(_down_kernel, n_lat),
        grid=(R // tm, D // tn, Fp // tk),
        in_specs=[pl.BlockSpec((tm, tk), lambda i, j, k: (i, k)),
                  pl.BlockSpec((tk, tn), lambda i, j, k: (k, j)),
                  pl.BlockSpec((tm, tn), lambda i, j, k: (i, j)),
                  pl.BlockSpec((2, tn), lambda i, j, k: (0, j))],
        out_specs=pl.BlockSpec((tm, tn), lambda i, j, k: (i, j)),
        out_shape=jax.ShapeDtypeStruct((R, D), F32),
        scratch_shapes=[pltpu.VMEM((tm, tn), F32)],
        compiler_params=_params("arbitrary", "arbitrary", "arbitrary"),
        name="ffn_down",
    )(h, w_down, resid, gate2)


def _qk_kernel(n_q_tiles, n_row_tiles, q_scale, a_ref, w_hbm, gain_ref, cos_ref, sa_ref, sb_ref, o_ref,
               wstage, w_bf, sem, acc_even, acc_odd):
    j, i = pl.program_id(0), pl.program_id(1)
    _staged_column_weights(w_hbm, lambda jj: (jj,), wstage, w_bf, sem)
    post = jnp.where(j < n_q_tiles, q_scale, 1.0)

    def project(acc_ref):
        acc_ref[...] = jnp.dot(a_ref[...], w_bf[0], preferred_element_type=F32)

    def finish(acc_ref):
        gain = gain_ref[0]
        cos, sa, sb = cos_ref[...], sa_ref[...], sb_ref[...]
        half = LANES // 4
        for g in range(acc_ref.shape[1] // LANES):
            t = acc_ref[:, g * LANES:(g + 1) * LANES]
            t = t * lax.rsqrt(jnp.mean(t * t, axis=-1, keepdims=True) + NORM_EPS)
            t = t * gain
            t = t * cos + pltpu.roll(t, LANES - half, 1) * sa + pltpu.roll(t, half, 1) * sb
            o_ref[:, g * LANES:(g + 1) * LANES] = (t * post).astype(BF)

    @pl.when(i == 0)
    def _():
        project(acc_even)

    @pl.when((i > 0) & (i < n_row_tiles) & (i % 2 == 0))
    def _():
        project(acc_even)
        finish(acc_odd)

    @pl.when((i < n_row_tiles) & (i % 2 == 1))
    def _():
        project(acc_odd)
        finish(acc_even)

    @pl.when(i == n_row_tiles)
    def _():
        finish(acc_odd if n_row_tiles % 2 == 0 else acc_even)


def _qk_proj(u, w_qkv, qk_gain2, cos, sa, sb, q_scale):
    R, D = u.shape
    tm = _pick(R, 1100, BF16_SUBLANES)
    tn = _pick(D, 512, LANES)
    nq, ni = D // tn, R // tm
    done = lambda i: jnp.maximum(i - 1, 0)
    tab = pl.BlockSpec((tm, LANES), lambda j, i: (done(i), 0))
    return pl.pallas_call(
        functools.partial(_qk_kernel, nq, ni, q_scale),
        grid=(2 * nq, ni + 1),
        in_specs=[pl.BlockSpec((tm, D), lambda j, i: (jnp.minimum(i, ni - 1), 0)),
                  pl.BlockSpec(memory_space=pl.ANY),
                  pl.BlockSpec((1, 1, LANES), lambda j, i: (jnp.where(j < nq, 0, 1), 0, 0)),
                  tab, tab, tab],
        out_specs=pl.BlockSpec((tm, tn), lambda j, i: (done(i), j)),
        out_shape=jax.ShapeDtypeStruct((R, 2 * D), BF),
        scratch_shapes=[pltpu.VMEM((1, D, tn), F32), pltpu.VMEM((1, D, tn), BF), pltpu.SemaphoreType.DMA((1,)),
                        pltpu.VMEM((tm, tn), F32), pltpu.VMEM((tm, tn), F32)],
        compiler_params=_params("arbitrary", "arbitrary"),
        name="qk_proj",
    )(u, w_qkv, qk_gain2, cos, sa, sb)


def _v_kernel(first_col, a_ref, w_hbm, o_ref, wstage, w_bf, sem):
    _staged_column_weights(w_hbm, lambda jj: (first_col + jj,), wstage, w_bf, sem)
    o_ref[...] = jnp.dot(a_ref[...], w_bf[0], preferred_element_type=F32).astype(BF)


def _v_proj(u, w_qkv):
    R, D = u.shape
    tm = _pick(R, 1100, BF16_SUBLANES)
    tn = _pick(D, 512, LANES)
    return pl.pallas_call(
        functools.partial(_v_kernel, 2 * D // tn),
        grid=(D // tn, R // tm),
        in_specs=[pl.BlockSpec((tm, D), lambda j, i: (i, 0)), pl.BlockSpec(memory_space=pl.ANY)],
        out_specs=pl.BlockSpec((tm, tn), lambda j, i: (i, j)),
        out_shape=jax.ShapeDtypeStruct((R, D), BF),
        scratch_shapes=[pltpu.VMEM((1, D, tn), F32), pltpu.VMEM((1, D, tn), BF), pltpu.SemaphoreType.DMA((1,))],
        compiler_params=_params("arbitrary", "arbitrary"),
        name="v_proj",
    )(u, w_qkv)


def _attn_kernel(n_lat, tk, lambda_init, q_ref, k_ref, v_ref, lam_ref, sg_ref, o_ref,
                 sa_ref, sb_ref, sc_ref, m_ref, l_ref, acc_ref):
    hd = q_ref.shape[1] // 2
    n_kv = k_ref.shape[0]
    n_chunks = n_lat // tk
    has_ctx = n_kv > n_lat
    m_ref[...] = jnp.full_like(m_ref, -jnp.inf)
    l_ref[...] = jnp.zeros_like(l_ref)
    acc_ref[...] = jnp.zeros_like(acc_ref)

    def scores(kc, dst):
        for m in range(2):
            dst[m] = lax.dot_general(q_ref[:, m * hd:(m + 1) * hd], kc[:, m * hd:(m + 1) * hd],
                                     (((1,), (1,)), ((), ())), preferred_element_type=F32)

    def absorb(src, vc):
        scaled = []
        for m in range(2):
            s = src[m]
            m_old = m_ref[m]
            m_new = jnp.maximum(m_old, jnp.max(s, axis=-1, keepdims=True))
            alpha = jnp.exp2(m_old - m_new)
            p = jnp.exp2(s - m_new)
            l_ref[m] = alpha * l_ref[m] + jnp.sum(p, axis=-1, keepdims=True)
            m_ref[m] = m_new
            scaled.append((alpha, p.astype(BF)))
        for m in range(2):
            alpha, p = scaled[m]
            acc_ref[m] = alpha * acc_ref[m] + jnp.dot(p, vc, preferred_element_type=F32)

    def k_chunk(c):
        return k_ref[pl.ds(pl.multiple_of(c * tk, tk), tk), :]

    def v_chunk(c):
        return v_ref[pl.ds(pl.multiple_of(c * tk, tk), tk), :]

    scores(k_chunk(0), sa_ref)

    def pair(i, carry):
        c = 2 * i
        scores(k_chunk(c + 1), sb_ref)
        absorb(sa_ref, v_chunk(c))
        scores(k_chunk(c + 2), sa_ref)
        absorb(sb_ref, v_chunk(c + 1))
        return carry

    n_pairs = (n_chunks - 1) // 2
    lax.fori_loop(0, n_pairs, pair, 0)
    c = 2 * n_pairs
    if n_chunks - c == 2:
        scores(k_chunk(c + 1), sb_ref)
        absorb(sa_ref, v_chunk(c))
        if has_ctx:
            scores(k_ref[n_lat:n_kv, :], sc_ref)
        absorb(sb_ref, v_chunk(c + 1))
    else:
        if has_ctx:
            scores(k_ref[n_lat:n_kv, :], sc_ref)
        absorb(sa_ref, v_chunk(c))
    if has_ctx:
        absorb(sc_ref, v_ref[n_lat:n_kv, :])

    lf = lam_ref[...]
    lam = (jnp.exp(jnp.sum(lf[0:1] * lf[1:2], axis=-1, keepdims=True))
           - jnp.exp(jnp.sum(lf[2:3] * lf[3:4], axis=-1, keepdims=True)) + lambda_init)
    o = acc_ref[0] / l_ref[0] - lam * (acc_ref[1] / l_ref[1])
    o = o * lax.rsqrt(jnp.mean(o * o, axis=-1, keepdims=True) + NORM_EPS)
    o_ref[...] = ((o * sg_ref[...]) * (1.0 - lambda_init)).astype(BF)


def _diff_attention(qk, v, lambdas, subln_gain, n_lat, n_heads, lambda_init):
    R, D = v.shape
    vd = D // n_heads
    tq = _pick(n_lat, 512, BF16_SUBLANES)
    tk = _pick(n_lat, 2048, LANES)
    return pl.pallas_call(
        functools.partial(_attn_kernel, n_lat, tk, lambda_init),
        grid=(n_heads, n_lat // tq),
        in_specs=[pl.BlockSpec((tq, vd), lambda h, i: (i, h)),
                  pl.BlockSpec((R, vd), lambda h, i: (0, n_heads + h)),
                  pl.BlockSpec((R, vd), lambda h, i: (0, h)),
                  pl.BlockSpec(lambdas.shape, lambda h, i: (0, 0)),
                  pl.BlockSpec((1, vd), lambda h, i: (0, 0))],
        out_specs=pl.BlockSpec((tq, vd), lambda h, i: (i, h)),
        out_shape=jax.ShapeDtypeStruct((n_lat, D), BF),
        scratch_shapes=[pltpu.VMEM((2, tq, tk), F32), pltpu.VMEM((2, tq, tk), F32),
                        pltpu.VMEM((2, tq, (R - n_lat) or LANES), F32),
                        pltpu.VMEM((2, tq, 1), F32), pltpu.VMEM((2, tq, 1), F32),
                        pltpu.VMEM((2, tq, vd), F32)],
        compiler_params=_params("arbitrary", "arbitrary"),
        name="diff_attention",
    )(qk, qk, v, lambdas, subln_gain.reshape(1, vd))


def _router_kernel(n_experts, x_ref, g_ref, sh_ref, sc_ref, w_ref, comb_ref, sel_ref):
    u = _modnorm_rows(x_ref[...], g_ref[...], sh_ref[...], sc_ref[...]).astype(BF)
    logits = jnp.dot(u, w_ref[...], preferred_element_type=F32)
    lane = lax.broadcasted_iota(jnp.int32, logits.shape, 1)
    neg = jnp.float32(-jnp.inf)
    l1 = jnp.where(lane < n_experts, logits, neg)
    m1 = jnp.max(l1, axis=-1, keepdims=True)
    i1 = jnp.min(jnp.where(l1 == m1, lane, LANES), axis=-1, keepdims=True)
    sel1 = lane == i1
    l2 = jnp.where(sel1, neg, l1)
    m2 = jnp.max(l2, axis=-1, keepdims=True)
    i2 = jnp.min(jnp.where(l2 == m2, lane, LANES), axis=-1, keepdims=True)
    sel2 = lane == i2
    e2 = jnp.exp(m2 - m1)
    den = 1.0 + e2
    comb_ref[...] = jnp.where(sel1, 1.0 / den, 0.0) + jnp.where(sel2, e2 / den, 0.0)
    sel_ref[...] = (sel1 | sel2).astype(jnp.int32)


def _router(x, gain, shift, scale, w_router_pad, n_experts):
    S, D = x.shape
    tm = _pick(S, 512, 8)
    vec = pl.BlockSpec((1, D), lambda i: (0, 0))
    out = pl.BlockSpec((tm, LANES), lambda i: (i, 0))
    return pl.pallas_call(
        functools.partial(_router_kernel, n_experts),
        grid=(S // tm,),
        in_specs=[pl.BlockSpec((tm, D), lambda i: (i, 0)), vec, vec, vec,
                  pl.BlockSpec((D, LANES), lambda i: (0, 0))],
        out_specs=[out, out],
        out_shape=[jax.ShapeDtypeStruct((S, LANES), F32), jax.ShapeDtypeStruct((S, LANES), jnp.int32)],
        compiler_params=_params("arbitrary"),
        name="moe_router",
    )(x, gain.reshape(1, D), shift.reshape(1, D), scale.reshape(1, D), w_router_pad)


def _row_copy(src_hbm, row, dst, r, sem):
    return pltpu.make_async_copy(src_hbm.at[pl.ds(row, 1)], dst.at[pl.ds(r, 1)], sem)


ROW_DMA_UNROLL = 8


def _gather_norm_kernel(tok_ref, nrows_ref, x_hbm, g_ref, sh_ref, sc_ref, o_ref, buf, sem):
    tg = buf.shape[1]
    t = pl.program_id(0)

    def used(tile):
        return tile * tg < nrows_ref[0]

    def issue(tile):
        slot = tile % 2

        def one(r, c):
            _row_copy(x_hbm, tok_ref[tile * tg + r], buf.at[slot], r, sem.at[slot]).start()
            return c

        lax.fori_loop(0, tg, one, 0, unroll=ROW_DMA_UNROLL)

    def wait(tile):
        slot = tile % 2

        def one(r, c):
            _row_copy(x_hbm, 0, buf.at[slot], r, sem.at[slot]).wait()
            return c

        lax.fori_loop(0, tg, one, 0, unroll=ROW_DMA_UNROLL)

    @pl.when((t == 0) & used(0))
    def _():
        issue(0)

    @pl.when((t + 1 < pl.num_programs(0)) & used(t + 1))
    def _():
        issue(t + 1)

    @pl.when(used(t))
    def _():
        wait(t)
        o_ref[...] = _modnorm_rows(buf[t % 2], g_ref[...], sh_ref[...], sc_ref[...]).astype(BF)

    @pl.when(jnp.logical_not(used(t)))
    def _():
        o_ref[...] = jnp.zeros_like(o_ref)


def _gather_norm(x, row_token, n_rows_used, gain, shift, scale):
    D = x.shape[1]
    Rs = row_token.shape[0]
    tg = _pick(Rs, 256, BF16_SUBLANES)
    vec = pl.BlockSpec((1, D), lambda t, tok, n: (0, 0))
    return pl.pallas_call(
        _gather_norm_kernel,
        grid_spec=pltpu.PrefetchScalarGridSpec(
            num_scalar_prefetch=2,
            grid=(Rs // tg,),
            in_specs=[pl.BlockSpec(memory_space=pl.ANY), vec, vec, vec],
            out_specs=pl.BlockSpec((tg, D), lambda t, tok, n: (t, 0)),
            scratch_shapes=[pltpu.VMEM((2, tg, D), F32), pltpu.SemaphoreType.DMA((2,))]),
        out_shape=jax.ShapeDtypeStruct((Rs, D), BF),
        compiler_params=_params("arbitrary"),
        name="moe_gather_norm",
    )(row_token, n_rows_used, x, gain.reshape(1, D), shift.reshape(1, D), scale.reshape(1, D))


SCHED_EXPERT, SCHED_RUN_START, SCHED_FIRST_RUN, SCHED_NEXT_EXPERT, SCHED_LAST_RUN = range(5)


def _staged_expert_weights(sched_ref, cnt_ref, w_hbm, col_blocks, wstage, w_bf, sem):
    j, t = pl.program_id(0), pl.program_id(1)
    nj, n_t = pl.num_programs(0), pl.num_programs(1)
    tn = wstage.shape[-1]

    def copies(e, jj):
        return [pltpu.make_async_copy(w_hbm.at[0, e, :, pl.ds(pl.multiple_of(cb * tn, tn), tn)],
                                      wstage.at[k], sem.at[k])
                for k, cb in enumerate(col_blocks(jj))]

    def sched(row):
        return sched_ref[row * n_t + t]

    @pl.when(sched(SCHED_RUN_START) == 1)
    def _():
        e = sched(SCHED_EXPERT)

        @pl.when((j == 0) & (sched(SCHED_FIRST_RUN) == 1))
        def _():
            for cp in copies(e, j):
                cp.start()

        for k, cp in enumerate(copies(e, j)):
            cp.wait()
            w_bf[k] = wstage[k].astype(BF)

        next_j = j + sched(SCHED_LAST_RUN)

        @pl.when(next_j < nj)
        def _():
            for cp in copies(sched(SCHED_NEXT_EXPERT), next_j):
                cp.start()


def _moe_gu_kernel(sched_ref, cnt_ref, a_ref, w_hbm, o_ref, wstage, w_bf, sem):
    t = pl.program_id(1)
    nj = pl.num_programs(0)

    @pl.when(t < cnt_ref[0])
    def _():
        _staged_expert_weights(sched_ref, cnt_ref, w_hbm, lambda jj: (jj, jj + nj), wstage, w_bf, sem)
        a = a_ref[...]
        g = jnp.dot(a, w_bf[0], preferred_element_type=F32)
        u = jnp.dot(a, w_bf[1], preferred_element_type=F32)
        o_ref[...] = ((g * _sigmoid(g)) * u).astype(BF)

    @pl.when(t >= cnt_ref[0])
    def _():
        o_ref[...] = jnp.zeros_like(o_ref)


def _moe_gu(a, w_gu, sched, counts, tm):
    Rs, D = a.shape
    F = w_gu.shape[3] // 2
    tn = _pick(F, 512, LANES)
    tile = lambda t, cnt: jnp.minimum(t, cnt[0] - 1)
    return pl.pallas_call(
        _moe_gu_kernel,
        grid_spec=pltpu.PrefetchScalarGridSpec(
            num_scalar_prefetch=2,
            grid=(F // tn, Rs // tm),
            in_specs=[pl.BlockSpec((tm, D), lambda j, t, sc, cnt: (tile(t, cnt), 0)),
                      pl.BlockSpec(memory_space=pl.ANY)],
            out_specs=pl.BlockSpec((tm, tn), lambda j, t, sc, cnt: (t, j)),
            scratch_shapes=[pltpu.VMEM((2, D, tn), F32), pltpu.VMEM((2, D, tn), BF),
                            pltpu.SemaphoreType.DMA((2,))]),
        out_shape=jax.ShapeDtypeStruct((Rs, F), BF),
        compiler_params=_params("arbitrary", "arbitrary"),
        name="moe_gu",
    )(sched, counts, a, w_gu)


def _moe_down_kernel(sched_ref, cnt_ref, a_ref, w_hbm, o_ref, wstage, w_bf, sem):
    t = pl.program_id(1)

    @pl.when(t < cnt_ref[0])
    def _():
        _staged_expert_weights(sched_ref, cnt_ref, w_hbm, lambda jj: (jj,), wstage, w_bf, sem)
        o_ref[...] = jnp.dot(a_ref[...], w_bf[0], preferred_element_type=F32)

    @pl.when(t >= cnt_ref[0])
    def _():
        o_ref[...] = jnp.zeros_like(o_ref)


def _moe_down(h, w_down, sched, counts, tm):
    Rs, F = h.shape
    D = w_down.shape[3]
    tn = _pick(D, 512, LANES)
    tile = lambda t, cnt: jnp.minimum(t, cnt[0] - 1)
    return pl.pallas_call(
        _moe_down_kernel,
        grid_spec=pltpu.PrefetchScalarGridSpec(
            num_scalar_prefetch=2,
            grid=(D // tn, Rs // tm),
            in_specs=[pl.BlockSpec((tm, F), lambda j, t, sc, cnt: (tile(t, cnt), 0)),
                      pl.BlockSpec(memory_space=pl.ANY)],
            out_specs=pl.BlockSpec((tm, tn), lambda j, t, sc, cnt: (t, j)),
            scratch_shapes=[pltpu.VMEM((1, F, tn), F32), pltpu.VMEM((1, F, tn), BF),
                            pltpu.SemaphoreType.DMA((1,))]),
        out_shape=jax.ShapeDtypeStruct((Rs, D), F32),
        compiler_params=_params("arbitrary", "arbitrary"),
        name="moe_down",
    )(sched, counts, h, w_down)


def _combine_kernel(pa_ref, pb_ref, y_hbm, x_ref, g_ref, wa_ref, wb_ref, o_ref, ya, yb, sem):
    tt = ya.shape[1]
    t = pl.program_id(0)

    def issue(tile):
        slot = tile % 2

        def one(r, c):
            _row_copy(y_hbm, pa_ref[tile * tt + r], ya.at[slot], r, sem.at[0, slot]).start()
            _row_copy(y_hbm, pb_ref[tile * tt + r], yb.at[slot], r, sem.at[1, slot]).start()
            return c

        lax.fori_loop(0, tt, one, 0, unroll=ROW_DMA_UNROLL)

    def wait(tile):
        slot = tile % 2

        def one(r, c):
            _row_copy(y_hbm, 0, ya.at[slot], r, sem.at[0, slot]).wait()
            _row_copy(y_hbm, 0, yb.at[slot], r, sem.at[1, slot]).wait()
            return c

        lax.fori_loop(0, tt, one, 0, unroll=ROW_DMA_UNROLL)

    @pl.when(t == 0)
    def _():
        issue(0)

    @pl.when(t + 1 < pl.num_programs(0))
    def _():
        issue(t + 1)

    wait(t)
    slot = t % 2
    o_ref[...] = x_ref[...] + g_ref[...] * (wa_ref[...] * ya[slot] + wb_ref[...] * yb[slot])


def _moe_combine(x, y, pos_a, pos_b, w_a, w_b, gate):
    S, D = x.shape
    tt = _pick(S, 256, 8)
    col = pl.BlockSpec((tt, 1), lambda t, pa, pb: (t, 0))
    return pl.pallas_call(
        _combine_kernel,
        grid_spec=pltpu.PrefetchScalarGridSpec(
            num_scalar_prefetch=2,
            grid=(S // tt,),
            in_specs=[pl.BlockSpec(memory_space=pl.ANY),
                      pl.BlockSpec((tt, D), lambda t, pa, pb: (t, 0)),
                      pl.BlockSpec((1, D), lambda t, pa, pb: (0, 0)),
                      col, col],
            out_specs=pl.BlockSpec((tt, D), lambda t, pa, pb: (t, 0)),
            scratch_shapes=[pltpu.VMEM((2, tt, D), F32), pltpu.VMEM((2, tt, D), F32),
                            pltpu.SemaphoreType.DMA((2, 2))]),
        out_shape=jax.ShapeDtypeStruct((S, D), F32),
        compiler_params=_params("arbitrary"),
        name="moe_combine",
    )(pos_a, pos_b, y, x, gate.reshape(1, D), w_a, w_b)


def _routing_tables(sel, comb, tm):
    S, E = sel.shape
    Rs = TOP_K * S + E * tm
    seli = sel.astype(jnp.int32)
    cnt = jnp.sum(seli, axis=0)
    padded = (cnt + tm - 1) // tm * tm
    gend = jnp.cumsum(padded)
    gstart = gend - padded
    pos = gstart[None, :] + jnp.cumsum(seli, axis=0) - seli
    pos_a = jnp.min(jnp.where(sel, pos, Rs), axis=1).astype(jnp.int32)
    pos_b = jnp.max(jnp.where(sel, pos, -1), axis=1).astype(jnp.int32)
    w_a = jnp.sum(jnp.where(sel & (pos == pos_a[:, None]), comb, 0.0), axis=1, keepdims=True)
    w_b = jnp.sum(jnp.where(sel & (pos == pos_b[:, None]), comb, 0.0), axis=1, keepdims=True)
    tok = jnp.arange(S, dtype=jnp.int32)
    row_token = jnp.zeros((Rs,), jnp.int32).at[jnp.concatenate([pos_a, pos_b])].set(
        jnp.concatenate([tok, tok]), mode="drop")
    n_t = Rs // tm
    tix = jnp.arange(n_t, dtype=jnp.int32)
    tile_expert = jnp.minimum(
        jnp.sum((tix[:, None] * tm >= gend[None, :]).astype(jnp.int32), axis=1), E - 1).astype(jnp.int32)
    n_rows = gend[-1].astype(jnp.int32)
    n_tiles = n_rows // tm
    prev_expert = jnp.concatenate([jnp.full((1,), -1, jnp.int32), tile_expert[:-1]])
    first = (tix < n_tiles) & (tile_expert != prev_expert)
    first_run = first & (tix == 0)
    later_first = first[None, :] & (tix[None, :] > tix[:, None])
    next_first = jnp.min(jnp.where(later_first, tix[None, :], n_t), axis=1)
    last_run = next_first >= n_t
    next_expert = jnp.where(last_run, tile_expert[0], jnp.sum(
        jnp.where(tix[None, :] == next_first[:, None], tile_expert[None, :], 0), axis=1))
    sched = jnp.concatenate([tile_expert, first.astype(jnp.int32), first_run.astype(jnp.int32), next_expert,
                             last_run.astype(jnp.int32)]).astype(jnp.int32)
    return row_token, pos_a, pos_b, w_a, w_b, sched, n_tiles.reshape(1), n_rows.reshape(1)


def _rope_tables(n_lat, n_rows, head_dim):
    n_freq = head_dim // 4
    s = jnp.arange(n_lat, dtype=jnp.int32)
    inv = ROPE_BASE ** (-jnp.arange(n_freq, dtype=F32) / n_freq)
    ang = jnp.stack([s // GRID_W, s % GRID_W], axis=-1).astype(F32)[..., None] * inv
    cos, sin = jnp.cos(ang), jnp.sin(ang)
    zero = jnp.zeros_like(sin)
    cos_t = jnp.stack([cos, cos], axis=2).reshape(n_lat, head_dim)
    sa_t = jnp.stack([-sin, zero], axis=2).reshape(n_lat, head_dim)
    sb_t = jnp.stack([zero, sin], axis=2).reshape(n_lat, head_dim)
    pad = ((0, n_rows - n_lat), (0, 0))
    return jnp.pad(cos_t, pad, constant_values=1.0), jnp.pad(sa_t, pad), jnp.pad(sb_t, pad)


def kernel(x, c, ctx, c_ctx, w_mod, b_mod, norm_gain, conv_w_in, conv_w, conv_w_out, attn_w_qkv, attn_w_o,
           attn_q_gain, attn_k_gain, attn_lambdas, attn_subln_gain, ffn_w_gu, ffn_w_down, moe_router,
           moe_w_gu, moe_w_down):
    assert x.shape[0] == 1 and w_mod.shape[0] == 2
    S, D = x.shape[1], x.shape[2]
    C = ctx.shape[1]
    R = S + C
    head_dim = attn_q_gain.shape[-1]
    n_heads = D // (2 * head_dim)
    E = moe_router.shape[-1]
    F = ffn_w_down.shape[1]
    f_pad = _round_up(F, 512)

    cc = jnp.stack([c[0], c_ctx]).reshape(2, D, 1)
    mod = _mod_vectors(cc, w_mod, b_mod)[:, :2, :].reshape(2, 2, N_MOD, D)

    xa = jnp.concatenate([x[0], ctx[0]], axis=0)

    m0 = mod[0]
    u = _modnorm(xa, norm_gain[0, 0], m0[:, 0], m0[:, 1], S, R)
    b, cv = _conv_in(u, conv_w_in)
    bz = _conv_mix(b, cv, conv_w[0], S)
    xa = _proj_resid(bz, conv_w_out, xa, m0[:, 2], S)
    u = _modnorm(xa, norm_gain[0, 1], m0[:, 3], m0[:, 4], S, R)
    w_dn = jnp.pad(ffn_w_down[0], ((0, f_pad - F), (0, 0))).astype(BF)
    h = _ffn_gu(u, ffn_w_gu, f_pad)
    xa = _ffn_down(h, w_dn, xa, m0[:, 5], S)

    m1 = mod[1]
    lambda_init = 0.8 - 0.6 * math.exp(-0.3 * 1)
    u = _modnorm(xa, norm_gain[1, 0], m1[:, 0], m1[:, 1], S, R)
    cos, sa, sb = _rope_tables(S, R, head_dim)
    qk_gain = jnp.stack([attn_q_gain[0], attn_k_gain[0]]).reshape(2, 1, head_dim)
    qk = _qk_proj(u, attn_w_qkv, qk_gain, cos, sa, sb, head_dim ** -0.5 * math.log2(math.e))
    v = _v_proj(u, attn_w_qkv)
    o = _diff_attention(qk, v, attn_lambdas[0], attn_subln_gain[0], S, n_heads, lambda_init)
    x1 = _proj_resid(o, attn_w_o, xa, m1[:, 2], S)

    tm = _pick(S, 512, BF16_SUBLANES)
    w_r = jnp.pad(moe_router[0], ((0, 0), (0, LANES - E))).astype(BF)
    comb, selm = _router(x1, norm_gain[1, 1], m1[0, 3], m1[0, 4], w_r, E)
    row_token, pos_a, pos_b, w_a, w_b, sched, counts, n_rows = _routing_tables(
        selm[:, :E] > 0, comb[:, :E], tm)
    us = _gather_norm(x1, row_token, n_rows, norm_gain[1, 1], m1[0, 3], m1[0, 4])
    hs = _moe_gu(us, moe_w_gu, sched, counts, tm)
    ys = _moe_down(hs, moe_w_down, sched, counts, tm)
    out = _moe_combine(x1, ys, pos_a, pos_b, w_a, w_b, m1[0, 5])
    return out[None]
```

```python
import functools
import math

import jax
import jax.numpy as jnp
from jax import lax
from jax.experimental import pallas as pl
from jax.experimental.pallas import tpu as pltpu

BF = jnp.bfloat16
F32 = jnp.float32

GRID_W = 64
ROPE_BASE = 10000.0
NORM_EPS = 1e-6
TOP_K = 2
N_MOD = 6

LANES = 128
BF16_SUBLANES = 16
VMEM_LIMIT_BYTES = 56 * 1024 * 1024


def _pick(n, target, align):
    best = None
    d = align
    while d <= min(n, target):
        if n % d == 0:
            best = d
        d += align
    return best if best is not None else n


def _round_up(n, m):
    return (n + m - 1) // m * m


def _params(*sem):
    return pltpu.CompilerParams(dimension_semantics=sem, vmem_limit_bytes=VMEM_LIMIT_BYTES)


def _sigmoid(x):
    return 1.0 / (1.0 + jnp.exp(-x))


def _mod_kernel(c_ref, w_ref, b_ref, o_ref, s_ref):
    @pl.when((pl.program_id(0) == 0) & (pl.program_id(1) == 0))
    def _():
        c = c_ref[...]
        s_ref[...] = jnp.broadcast_to(c * _sigmoid(c), s_ref.shape)

    o_ref[...] = jnp.zeros_like(o_ref)
    for g in range(w_ref.shape[2] // LANES):
        w = w_ref[0, :, g * LANES:(g + 1) * LANES]
        for r in range(2):
            o_ref[0, r:r + 1, g * LANES:(g + 1) * LANES] = (
                jnp.sum(w * s_ref[r], axis=0, keepdims=True) + b_ref[0, :, g * LANES:(g + 1) * LANES])


def _mod_vectors(cc, w_mod, b_mod):
    L, D, N = w_mod.shape
    tn = _pick(N, 512, LANES)
    return pl.pallas_call(
        _mod_kernel,
        grid=(L, N // tn),
        in_specs=[pl.BlockSpec((2, D, 1), lambda l, j: (0, 0, 0)),
                  pl.BlockSpec((1, D, tn), lambda l, j: (l, 0, j)),
                  pl.BlockSpec((1, 1, tn), lambda l, j: (l, 0, j))],
        out_specs=pl.BlockSpec((1, 8, tn), lambda l, j: (l, 0, j)),
        out_shape=jax.ShapeDtypeStruct((L, 8, N), F32),
        scratch_shapes=[pltpu.VMEM((2, D, LANES), F32)],
        compiler_params=_params("arbitrary", "arbitrary"),
        name="mod_vectors",
    )(cc, w_mod, b_mod.reshape(L, 1, N))


def _modnorm_rows(x, gain, shift, scale):
    ms = jnp.mean(x * x, axis=-1, keepdims=True)
    xn = x * lax.rsqrt(ms + NORM_EPS)
    return (xn * gain) * (1.0 + scale) + shift


def _modnorm_kernel(x_ref, g_ref, sh_ref, sc_ref, o_ref):
    o_ref[...] = _modnorm_rows(x_ref[...], g_ref[...], sh_ref[0], sc_ref[0]).astype(BF)


def _modnorm(x, gain, shift2, scale2, n_lat, n_rows):
    D = x.shape[1]
    tr = _pick(math.gcd(n_lat, n_rows), 256, BF16_SUBLANES)
    n_lat_tiles = n_lat // tr
    grp = lambda i: (jnp.where(i < n_lat_tiles, 0, 1), 0, 0)
    return pl.pallas_call(
        _modnorm_kernel,
        grid=(n_rows // tr,),
        in_specs=[pl.BlockSpec((tr, D), lambda i: (i, 0)),
                  pl.BlockSpec((1, D), lambda i: (0, 0)),
                  pl.BlockSpec((1, 1, D), grp),
                  pl.BlockSpec((1, 1, D), grp)],
        out_specs=pl.BlockSpec((tr, D), lambda i: (i, 0)),
        out_shape=jax.ShapeDtypeStruct((n_rows, D), BF),
        compiler_params=_params("arbitrary"),
        name="modnorm",
    )(x, gain.reshape(1, D), shift2.reshape(2, 1, D), scale2.reshape(2, 1, D))


def _staged_column_weights(w_hbm, col_blocks, wstage, w_bf, sem):
    j, i = pl.program_id(0), pl.program_id(1)
    tn = wstage.shape[-1]

    def copies(jj):
        return [pltpu.make_async_copy(w_hbm.at[0, :, pl.ds(pl.multiple_of(cb * tn, tn), tn)],
                                      wstage.at[k], sem.at[k])
                for k, cb in enumerate(col_blocks(jj))]

    @pl.when(i == 0)
    def _():
        @pl.when(j == 0)
        def _():
            for cp in copies(j):
                cp.start()

        for k, cp in enumerate(copies(j)):
            cp.wait()
            w_bf[k] = wstage[k].astype(BF)

        @pl.when(j + 1 < pl.num_programs(0))
        def _():
            for cp in copies(j + 1):
                cp.start()


def _conv_in_kernel(a_ref, w_hbm, b_ref, cv_ref, wstage, w_bf, sem):
    nj = pl.num_programs(0)
    _staged_column_weights(w_hbm, lambda jj: (jj, jj + nj, jj + 2 * nj), wstage, w_bf, sem)
    a = a_ref[...]
    b_ref[...] = jnp.dot(a, w_bf[0], preferred_element_type=F32).astype(BF)
    c = jnp.dot(a, w_bf[1], preferred_element_type=F32)
    v = jnp.dot(a, w_bf[2], preferred_element_type=F32)
    cv_ref[...] = (c * v).astype(BF)


def _conv_in(u, w_in):
    R, D = u.shape
    tm = _pick(R, 1100, BF16_SUBLANES)
    tn = _pick(D, 256, LANES)
    return pl.pallas_call(
        _conv_in_kernel,
        grid=(D // tn, R // tm),
        in_specs=[pl.BlockSpec((tm, D), lambda j, i: (i, 0)), pl.BlockSpec(memory_space=pl.ANY)],
        out_specs=[pl.BlockSpec((tm, tn), lambda j, i: (i, j)),
                   pl.BlockSpec((tm, tn), lambda j, i: (i, j))],
        out_shape=[jax.ShapeDtypeStruct((R, D), BF), jax.ShapeDtypeStruct((R, D), BF)],
        scratch_shapes=[pltpu.VMEM((3, D, tn), F32), pltpu.VMEM((3, D, tn), BF),
                        pltpu.SemaphoreType.DMA((3,))],
        compiler_params=_params("arbitrary", "arbitrary"),
        name="conv_in",
    )(u, w_in)


def _conv_mix_kernel(n_lat_tiles, b_ref, cv_ref, pv_ref, nx_ref, w_ref, o_ref):
    i = pl.program_id(0)
    tr = cv_ref.shape[0]
    cv = cv_ref[...].astype(F32)
    top_zero = (i == 0) | (i == n_lat_tiles)
    bot_zero = (i == n_lat_tiles - 1) | (i == pl.num_programs(0) - 1)
    hp = pv_ref[BF16_SUBLANES - 1:BF16_SUBLANES, :].astype(F32) * jnp.where(top_zero, 0.0, 1.0)
    hn = nx_ref[0:1, :].astype(F32) * jnp.where(bot_zero, 0.0, 1.0)
    rows = lax.broadcasted_iota(jnp.int32, cv.shape, 0)
    prev = jnp.where(rows == 0, hp, pltpu.roll(cv, 1, 0))
    nxt = jnp.where(rows == tr - 1, hn, pltpu.roll(cv, tr - 1, 0))
    w = w_ref[...]
    z = prev * w[0:1] + cv * w[1:2] + nxt * w[2:3]
    o_ref[...] = (b_ref[...].astype(F32) * z).astype(BF)


def _conv_mix(b, cv, conv_w, n_lat):
    R, D = cv.shape
    tr = _pick(math.gcd(n_lat, R), 256, BF16_SUBLANES)
    tc = _pick(D, 1024, LANES)
    hb = tr // BF16_SUBLANES
    last = R // BF16_SUBLANES - 1
    return pl.pallas_call(
        functools.partial(_conv_mix_kernel, n_lat // tr),
        grid=(R // tr, D // tc),
        in_specs=[pl.BlockSpec((tr, tc), lambda i, j: (i, j)),
                  pl.BlockSpec((tr, tc), lambda i, j: (i, j)),
                  pl.BlockSpec((BF16_SUBLANES, tc), lambda i, j: (jnp.maximum(i * hb - 1, 0), j)),
                  pl.BlockSpec((BF16_SUBLANES, tc), lambda i, j: (jnp.minimum((i + 1) * hb, last), j)),
                  pl.BlockSpec((3, tc), lambda i, j: (0, j))],
        out_specs=pl.BlockSpec((tr, tc), lambda i, j: (i, j)),
        out_shape=jax.ShapeDtypeStruct((R, D), BF),
        compiler_params=_params("arbitrary", "arbitrary"),
        name="conv_mix",
    )(b, cv, cv, cv, conv_w)


def _row_gate(g_ref, row0, shape, n_lat):
    rows = row0 + lax.broadcasted_iota(jnp.int32, shape, 0)
    return jnp.where(rows < n_lat, g_ref[0:1, :], g_ref[1:2, :])


def _proj_resid_kernel(n_lat, a_ref, w_ref, r_ref, g_ref, o_ref, w_bf):
    @pl.when(pl.program_id(1) == 0)
    def _():
        w_bf[...] = w_ref[0].astype(BF)

    tm = o_ref.shape[0]
    acc = jnp.dot(a_ref[...], w_bf[...], preferred_element_type=F32)
    g = _row_gate(g_ref, pl.program_id(1) * tm, acc.shape, n_lat)
    o_ref[...] = r_ref[...] + g * acc


def _proj_resid(a, w, resid, gate2, n_lat):
    M, K = a.shape
    N = w.shape[2]
    tm = _pick(M, 1100, BF16_SUBLANES)
    tn = _pick(N, 512, LANES)
    return pl.pallas_call(
        functools.partial(_proj_resid_kernel, n_lat),
        grid=(N // tn, M // tm),
        in_specs=[pl.BlockSpec((tm, K), lambda j, i: (i, 0)),
                  pl.BlockSpec((1, K, tn), lambda j, i: (0, 0, j)),
                  pl.BlockSpec((tm, tn), lambda j, i: (i, j)),
                  pl.BlockSpec((2, tn), lambda j, i: (0, j))],
        out_specs=pl.BlockSpec((tm, tn), lambda j, i: (i, j)),
        out_shape=jax.ShapeDtypeStruct((M, N), F32),
        scratch_shapes=[pltpu.VMEM((K, tn), BF)],
        compiler_params=_params("arbitrary", "arbitrary"),
        name="proj_resid",
    )(a, w, resid, gate2)


def _gu_kernel(n_real, a_ref, wg_ref, wu_ref, o_ref):
    j = pl.program_id(1)

    @pl.when(j < n_real)
    def _():
        a = a_ref[...]
        g = jnp.dot(a, wg_ref[0].astype(BF), preferred_element_type=F32)
        u = jnp.dot(a, wu_ref[0].astype(BF), preferred_element_type=F32)
        o_ref[...] = ((g * _sigmoid(g)) * u).astype(BF)

    @pl.when(j >= n_real)
    def _():
        o_ref[...] = jnp.zeros_like(o_ref)


def _ffn_gu(u, w_gu, f_pad):
    R, D = u.shape
    F = w_gu.shape[2] // 2
    tm = _pick(R, 2200, BF16_SUBLANES)
    tn = _pick(math.gcd(F, f_pad), 256, LANES)
    nf = F // tn
    col = lambda j: jnp.minimum(j, nf - 1)
    return pl.pallas_call(
        functools.partial(_gu_kernel, nf),
        grid=(R // tm, f_pad // tn),
        in_specs=[pl.BlockSpec((tm, D), lambda i, j: (i, 0), pipeline_mode=pl.Buffered(1)),
                  pl.BlockSpec((1, D, tn), lambda i, j: (0, 0, col(j))),
                  pl.BlockSpec((1, D, tn), lambda i, j: (0, 0, col(j) + nf))],
        out_specs=pl.BlockSpec((tm, tn), lambda i, j: (i, j)),
        out_shape=jax.ShapeDtypeStruct((R, f_pad), BF),
        compiler_params=_params("arbitrary", "arbitrary"),
        name="ffn_gu",
    )(u, w_gu, w_gu)


def _down_kernel(n_lat, a_ref, w_ref, r_ref, g_ref, o_ref, acc_ref):
    k = pl.program_id(2)

    @pl.when(k == 0)
    def _():
        acc_ref[...] = jnp.zeros_like(acc_ref)

    acc_ref[...] += jnp.dot(a_ref[...], w_ref[...], preferred_element_type=F32)

    @pl.when(k == pl.num_programs(2) - 1)
    def _():
        acc = acc_ref[...]
        g = _row_gate(g_ref, pl.program_id(0) * acc.shape[0], acc.shape, n_lat)
        o_ref[...] = r_ref[...] + g * acc


def _ffn_down(h, w_down, resid, gate2, n_lat):
    R, Fp = h.shape
    D = w_down.shape[1]
    tm = _pick(R, 1100, BF16_SUBLANES)
    tn = _pick(D, 1024, LANES)
    tk = _pick(Fp, 2816, LANES)
    return pl.pallas_call(
        functools.partial(_down_kernel, n_lat),
        grid=(R // tm, D // tn, Fp // tk),
        in_specs=[pl.BlockSpec((tm, tk), lambda i, j, k: (i, k)),
                  pl.BlockSpec((tk, tn), lambda i, j, k: (k, j)),
                  pl.BlockSpec((tm, tn), lambda i, j, k: (i, j)),
                  pl.BlockSpec((2, tn), lambda i, j, k: (0, j))],
        out_specs=pl.BlockSpec((tm, tn), lambda i, j, k: (i, j)),
        out_shape=jax.ShapeDtypeStruct((R, D), F32),
        scratch_shapes=[pltpu.VMEM((tm, tn), F32)],
        compiler_params=_params("arbitrary", "arbitrary", "arbitrary"),
        name="ffn_down",
    )(h, w_down, resid, gate2)


def _qk_kernel(n_q_tiles, n_row_tiles, q_scale, a_ref, w_hbm, gain_ref, cos_ref, sa_ref, sb_ref, o_ref,
               wstage, w_bf, sem, acc_even, acc_odd):
    j, i = pl.program_id(0), pl.program_id(1)
    _staged_column_weights(w_hbm, lambda jj: (jj,), wstage, w_bf, sem)
    post = jnp.where(j < n_q_tiles, q_scale, 1.0)

    def project(acc_ref):
        acc_ref[...] = jnp.dot(a_ref[...], w_bf[0], preferred_element_type=F32)

    def finish(acc_ref):
        gain = gain_ref[0]
        cos, sa, sb = cos_ref[...], sa_ref[...], sb_ref[...]
        half = LANES // 4
        for g in range(acc_ref.shape[1] // LANES):
            t = acc_ref[:, g * LANES:(g + 1) * LANES]
            t = t * lax.rsqrt(jnp.mean(t * t, axis=-1, keepdims=True) + NORM_EPS)
            t = t * gain
            t = t * cos + pltpu.roll(t, LANES - half, 1) * sa + pltpu.roll(t, half, 1) * sb
            o_ref[:, g * LANES:(g + 1) * LANES] = (t * post).astype(BF)

    @pl.when(i == 0)
    def _():
        project(acc_even)

    @pl.when((i > 0) & (i < n_row_tiles) & (i % 2 == 0))
    def _():
        project(acc_even)
        finish(acc_odd)

    @pl.when((i < n_row_tiles) & (i % 2 == 1))
    def _():
        project(acc_odd)
        finish(acc_even)

    @pl.when(i == n_row_tiles)
    def _():
        finish(acc_odd if n_row_tiles % 2 == 0 else acc_even)


def _qk_proj(u, w_qkv, qk_gain2, cos, sa, sb, q_scale):
    R, D = u.shape
    tm = _pick(R, 1100, BF16_SUBLANES)
    tn = _pick(D, 512, LANES)
    nq, ni = D // tn, R // tm
    done = lambda i: jnp.maximum(i - 1, 0)
    tab = pl.BlockSpec((tm, LANES), lambda j, i: (done(i), 0))
    return pl.pallas_call(
        functools.partial(_qk_kernel, nq, ni, q_scale),
        grid=(2 * nq, ni + 1),
        in_specs=[pl.BlockSpec((tm, D), lambda j, i: (jnp.minimum(i, ni - 1), 0)),
                  pl.BlockSpec(memory_space=pl.ANY),
                  pl.BlockSpec((1, 1, LANES), lambda j, i: (jnp.where(j < nq, 0, 1), 0, 0)),
                  tab, tab, tab],
        out_specs=pl.BlockSpec((tm, tn), lambda j, i: (done(i), j)),
        out_shape=jax.ShapeDtypeStruct((R, 2 * D), BF),
        scratch_shapes=[pltpu.VMEM((1, D, tn), F32), pltpu.VMEM((1, D, tn), BF), pltpu.SemaphoreType.DMA((1,)),
                        pltpu.VMEM((tm, tn), F32), pltpu.VMEM((tm, tn), F32)],
        compiler_params=_params("arbitrary", "arbitrary"),
        name="qk_proj",
    )(u, w_qkv, qk_gain2, cos, sa, sb)


def _v_kernel(first_col, a_ref, w_hbm, o_ref, wstage, w_bf, sem):
    _staged_column_weights(w_hbm, lambda jj: (first_col + jj,), wstage, w_bf, sem)
    o_ref[...] = jnp.dot(a_ref[...], w_bf[0], preferred_element_type=F32).astype(BF)


def _v_proj(u, w_qkv):
    R, D = u.shape
    tm = _pick(R, 1100, BF16_SUBLANES)
    tn = _pick(D, 512, LANES)
    return pl.pallas_call(
        functools.partial(_v_kernel, 2 * D // tn),
        grid=(D // tn, R // tm),
        in_specs=[pl.BlockSpec((tm, D), lambda j, i: (i, 0)), pl.BlockSpec(memory_space=pl.ANY)],
        out_specs=pl.BlockSpec((tm, tn), lambda j, i: (i, j)),
        out_shape=jax.ShapeDtypeStruct((R, D), BF),
        scratch_shapes=[pltpu.VMEM((1, D, tn), F32), pltpu.VMEM((1, D, tn), BF), pltpu.SemaphoreType.DMA((1,))],
        compiler_params=_params("arbitrary", "arbitrary"),
        name="v_proj",
    )(u, w_qkv)


def _attn_kernel(n_lat, tk, lambda_init, q_ref, k_ref, v_ref, lam_ref, sg_ref, o_ref,
                 sa_ref, sb_ref, sc_ref, m_ref, l_ref, acc_ref):
    hd = q_ref.shape[1] // 2
    n_kv = k_ref.shape[0]
    n_chunks = n_lat // tk
    has_ctx = n_kv > n_lat
    m_ref[...] = jnp.full_like(m_ref, -jnp.inf)
    l_ref[...] = jnp.zeros_like(l_ref)
    acc_ref[...] = jnp.zeros_like(acc_ref)

    def scores(kc, dst):
        for m in range(2):
            dst[m] = lax.dot_general(q_ref[:, m * hd:(m + 1) * hd], kc[:, m * hd:(m + 1) * hd],
                                     (((1,), (1,)), ((), ())), preferred_element_type=F32)

    def absorb(src, vc):
        scaled = []
        for m in range(2):
            s = src[m]
            m_old = m_ref[m]
            m_new = jnp.maximum(m_old, jnp.max(s, axis=-1, keepdims=True))
            alpha = jnp.exp2(m_old - m_new)
            p = jnp.exp2(s - m_new)
            l_ref[m] = alpha * l_ref[m] + jnp.sum(p, axis=-1, keepdims=True)
            m_ref[m] = m_new
            scaled.append((alpha, p.astype(BF)))
        for m in range(2):
            alpha, p = scaled[m]
            acc_ref[m] = alpha * acc_ref[m] + jnp.dot(p, vc, preferred_element_type=F32)

    def k_chunk(c):
        return k_ref[pl.ds(pl.multiple_of(c * tk, tk), tk), :]

    def v_chunk(c):
        return v_ref[pl.ds(pl.multiple_of(c * tk, tk), tk), :]

    scores(k_chunk(0), sa_ref)

    def pair(i, carry):
        c = 2 * i
        scores(k_chunk(c + 1), sb_ref)
        absorb(sa_ref, v_chunk(c))
        scores(k_chunk(c + 2), sa_ref)
        absorb(sb_ref, v_chunk(c + 1))
        return carry

    n_pairs = (n_chunks - 1) // 2
    lax.fori_loop(0, n_pairs, pair, 0)
    c = 2 * n_pairs
    if n_chunks - c == 2:
        scores(k_chunk(c + 1), sb_ref)
        absorb(sa_ref, v_chunk(c))
        if has_ctx:
            scores(k_ref[n_lat:n_kv, :], sc_ref)
        absorb(sb_ref, v_chunk(c + 1))
    else:
        if has_ctx:
            scores(k_ref[n_lat:n_kv, :], sc_ref)
        absorb(sa_ref, v_chunk(c))
    if has_ctx:
        absorb(sc_ref, v_ref[n_lat:n_kv, :])

    lf = lam_ref[...]
    lam = (jnp.exp(jnp.sum(lf[0:1] * lf[1:2], axis=-1, keepdims=True))
           - jnp.exp(jnp.sum(lf[2:3] * lf[3:4], axis=-1, keepdims=True)) + lambda_init)
    o = acc_ref[0] / l_ref[0] - lam * (acc_ref[1] / l_ref[1])
    o = o * lax.rsqrt(jnp.mean(o * o, axis=-1, keepdims=True) + NORM_EPS)
    o_ref[...] = ((o * sg_ref[...]) * (1.0 - lambda_init)).astype(BF)


def _diff_attention(qk, v, lambdas, subln_gain, n_lat, n_heads, lambda_init):
    R, D = v.shape
    vd = D // n_heads
    tq = _pick(n_lat, 512, BF16_SUBLANES)
    tk = _pick(n_lat, 2048, LANES)
    return pl.pallas_call(
        functools.partial(_attn_kernel, n_lat, tk, lambda_init),
        grid=(n_heads, n_lat // tq),
        in_specs=[pl.BlockSpec((tq, vd), lambda h, i: (i, h)),
                  pl.BlockSpec((R, vd), lambda h, i: (0, n_heads + h)),
                  pl.BlockSpec((R, vd), lambda h, i: (0, h)),
                  pl.BlockSpec(lambdas.shape, lambda h, i: (0, 0)),
                  pl.BlockSpec((1, vd), lambda h, i: (0, 0))],
        out_specs=pl.BlockSpec((tq, vd), lambda h, i: (i, h)),
        out_shape=jax.ShapeDtypeStruct((n_lat, D), BF),
        scratch_shapes=[pltpu.VMEM((2, tq, tk), F32), pltpu.VMEM((2, tq, tk), F32),
                        pltpu.VMEM((2, tq, (R - n_lat) or LANES), F32),
                        pltpu.VMEM((2, tq, 1), F32), pltpu.VMEM((2, tq, 1), F32),
                        pltpu.VMEM((2, tq, vd), F32)],
        compiler_params=_params("arbitrary", "arbitrary"),
        name="diff_attention",
    )(qk, qk, v, lambdas, subln_gain.reshape(1, vd))


def _router_kernel(n_experts, x_ref, g_ref, sh_ref, sc_ref, w_ref, comb_ref, sel_ref):
    u = _modnorm_rows(x_ref[...], g_ref[...], sh_ref[...], sc_ref[...]).astype(BF)
    logits = jnp.dot(u, w_ref[...], preferred_element_type=F32)
    lane = lax.broadcasted_iota(jnp.int32, logits.shape, 1)
    neg = jnp.float32(-jnp.inf)
    l1 = jnp.where(lane < n_experts, logits, neg)
    m1 = jnp.max(l1, axis=-1, keepdims=True)
    i1 = jnp.min(jnp.where(l1 == m1, lane, LANES), axis=-1, keepdims=True)
    sel1 = lane == i1
    l2 = jnp.where(sel1, neg, l1)
    m2 = jnp.max(l2, axis=-1, keepdims=True)
    i2 = jnp.min(jnp.where(l2 == m2, lane, LANES), axis=-1, keepdims=True)
    sel2 = lane == i2
    e2 = jnp.exp(m2 - m1)
    den = 1.0 + e2
    comb_ref[...] = jnp.where(sel1, 1.0 / den, 0.0) + jnp.where(sel2, e2 / den, 0.0)
    sel_ref[...] = (sel1 | sel2).astype(jnp.int32)


def _router(x, gain, shift, scale, w_router_pad, n_experts):
    S, D = x.shape
    tm = _pick(S, 512, 8)
    vec = pl.BlockSpec((1, D), lambda i: (0, 0))
    out = pl.BlockSpec((tm, LANES), lambda i: (i, 0))
    return pl.pallas_call(
        functools.partial(_router_kernel, n_experts),
        grid=(S // tm,),
        in_specs=[pl.BlockSpec((tm, D), lambda i: (i, 0)), vec, vec, vec,
                  pl.BlockSpec((D, LANES), lambda i: (0, 0))],
        out_specs=[out, out],
        out_shape=[jax.ShapeDtypeStruct((S, LANES), F32), jax.ShapeDtypeStruct((S, LANES), jnp.int32)],
        compiler_params=_params("arbitrary"),
        name="moe_router",
    )(x, gain.reshape(1, D), shift.reshape(1, D), scale.reshape(1, D), w_router_pad)


def _row_copy(src_hbm, row, dst, r, sem):
    return pltpu.make_async_copy(src_hbm.at[pl.ds(row, 1)], dst.at[pl.ds(r, 1)], sem)


ROW_DMA_UNROLL = 8


def _gather_norm_kernel(tok_ref, nrows_ref, x_hbm, g_ref, sh_ref, sc_ref, o_ref, buf, sem):
    tg = buf.shape[1]
    t = pl.program_id(0)

    def used(tile):
        return tile * tg < nrows_ref[0]

    def issue(tile):
        slot = tile % 2

        def one(r, c):
            _row_copy(x_hbm, tok_ref[tile * tg + r], buf.at[slot], r, sem.at[slot]).start()
            return c

        lax.fori_loop(0, tg, one, 0, unroll=ROW_DMA_UNROLL)

    def wait(tile):
        slot = tile % 2

        def one(r, c):
            _row_copy(x_hbm, 0, buf.at[slot], r, sem.at[slot]).wait()
            return c

        lax.fori_loop(0, tg, one, 0, unroll=ROW_DMA_UNROLL)

    @pl.when((t == 0) & used(0))
    def _():
        issue(0)

    @pl.when((t + 1 < pl.num_programs(0)) & used(t + 1))
    def _():
        issue(t + 1)

    @pl.when(used(t))
    def _():
        wait(t)
        o_ref[...] = _modnorm_rows(buf[t % 2], g_ref[...], sh_ref[...], sc_ref[...]).astype(BF)

    @pl.when(jnp.logical_not(used(t)))
    def _():
        o_ref[...] = jnp.zeros_like(o_ref)


def _gather_norm(x, row_token, n_rows_used, gain, shift, scale):
    D = x.shape[1]
    Rs = row_token.shape[0]
    tg = _pick(Rs, 256, BF16_SUBLANES)
    vec = pl.BlockSpec((1, D), lambda t, tok, n: (0, 0))
    return pl.pallas_call(
        _gather_norm_kernel,
        grid_spec=pltpu.PrefetchScalarGridSpec(
            num_scalar_prefetch=2,
            grid=(Rs // tg,),
            in_specs=[pl.BlockSpec(memory_space=pl.ANY), vec, vec, vec],
            out_specs=pl.BlockSpec((tg, D), lambda t, tok, n: (t, 0)),
            scratch_shapes=[pltpu.VMEM((2, tg, D), F32), pltpu.SemaphoreType.DMA((2,))]),
        out_shape=jax.ShapeDtypeStruct((Rs, D), BF),
        compiler_params=_params("arbitrary"),
        name="moe_gather_norm",
    )(row_token, n_rows_used, x, gain.reshape(1, D), shift.reshape(1, D), scale.reshape(1, D))


SCHED_EXPERT, SCHED_RUN_START, SCHED_FIRST_RUN, SCHED_NEXT_EXPERT, SCHED_LAST_RUN = range(5)


def _staged_expert_weights(sched_ref, cnt_ref, w_hbm, col_blocks, wstage, w_bf, sem):
    j, t = pl.program_id(0), pl.program_id(1)
    nj, n_t = pl.num_programs(0), pl.num_programs(1)
    tn = wstage.shape[-1]

    def copies(e, jj):
        return [pltpu.make_async_copy(w_hbm.at[0, e, :, pl.ds(pl.multiple_of(cb * tn, tn), tn)],
                                      wstage.at[k], sem.at[k])
                for k, cb in enumerate(col_blocks(jj))]

    def sched(row):
        return sched_ref[row * n_t + t]

    @pl.when(sched(SCHED_RUN_START) == 1)
    def _():
        e = sched(SCHED_EXPERT)

        @pl.when((j == 0) & (sched(SCHED_FIRST_RUN) == 1))
        def _():
            for cp in copies(e, j):
                cp.start()

        for k, cp in enumerate(copies(e, j)):
            cp.wait()
            w_bf[k] = wstage[k].astype(BF)

        next_j = j + sched(SCHED_LAST_RUN)

        @pl.when(next_j < nj)
        def _():
            for cp in copies(sched(SCHED_NEXT_EXPERT), next_j):
                cp.start()


def _moe_gu_kernel(sched_ref, cnt_ref, a_ref, w_hbm, o_ref, wstage, w_bf, sem):
    t = pl.program_id(1)
    nj = pl.num_programs(0)

    @pl.when(t < cnt_ref[0])
    def _():
        _staged_expert_weights(sched_ref, cnt_ref, w_hbm, lambda jj: (jj, jj + nj), wstage, w_bf, sem)
        a = a_ref[...]
        g = jnp.dot(a, w_bf[0], preferred_element_type=F32)
        u = jnp.dot(a, w_bf[1], preferred_element_type=F32)
        o_ref[...] = ((g * _sigmoid(g)) * u).astype(BF)

    @pl.when(t >= cnt_ref[0])
    def _():
        o_ref[...] = jnp.zeros_like(o_ref)


def _moe_gu(a, w_gu, sched, counts, tm):
    Rs, D = a.shape
    F = w_gu.shape[3] // 2
    tn = _pick(F, 512, LANES)
    tile = lambda t, cnt: jnp.minimum(t, cnt[0] - 1)
    return pl.pallas_call(
        _moe_gu_kernel,
        grid_spec=pltpu.PrefetchScalarGridSpec(
            num_scalar_prefetch=2,
            grid=(F // tn, Rs // tm),
            in_specs=[pl.BlockSpec((tm, D), lambda j, t, sc, cnt: (tile(t, cnt), 0)),
                      pl.BlockSpec(memory_space=pl.ANY)],
            out_specs=pl.BlockSpec((tm, tn), lambda j, t, sc, cnt: (t, j)),
            scratch_shapes=[pltpu.VMEM((2, D, tn), F32), pltpu.VMEM((2, D, tn), BF),
                            pltpu.SemaphoreType.DMA((2,))]),
        out_shape=jax.ShapeDtypeStruct((Rs, F), BF),
        compiler_params=_params("arbitrary", "arbitrary"),
        name="moe_gu",
    )(sched, counts, a, w_gu)


def _moe_down_kernel(sched_ref, cnt_ref, a_ref, w_hbm, o_ref, wstage, w_bf, sem):
    t = pl.program_id(1)

    @pl.when(t < cnt_ref[0])
    def _():
        _staged_expert_weights(sched_ref, cnt_ref, w_hbm, lambda jj: (jj,), wstage, w_bf, sem)
        o_ref[...] = jnp.dot(a_ref[...], w_bf[0], preferred_element_type=F32)

    @pl.when(t >= cnt_ref[0])
    def _():
        o_ref[...] = jnp.zeros_like(o_ref)


def _moe_down(h, w_down, sched, counts, tm):
    Rs, F = h.shape
    D = w_down.shape[3]
    tn = _pick(D, 1024, LANES)
    tile = lambda t, cnt: jnp.minimum(t, cnt[0] - 1)
    return pl.pallas_call(
        _moe_down_kernel,
        grid_spec=pltpu.PrefetchScalarGridSpec(
            num_scalar_prefetch=2,
            grid=(D // tn, Rs // tm),
            in_specs=[pl.BlockSpec((tm, F), lambda j, t, sc, cnt: (tile(t, cnt), 0)),
                      pl.BlockSpec(memory_space=pl.ANY)],
            out_specs=pl.BlockSpec((tm, tn), lambda j, t, sc, cnt: (t, j)),
            scratch_shapes=[pltpu.VMEM((1, F, tn), F32), pltpu.VMEM((1, F, tn), BF),
                            pltpu.SemaphoreType.DMA((1,))]),
        out_shape=jax.ShapeDtypeStruct((Rs, D), F32),
        compiler_params=_params("arbitrary", "arbitrary"),
        name="moe_down",
    )(sched, counts, h, w_down)


def _combine_kernel(pa_ref, pb_ref, y_hbm, x_ref, g_ref, wa_ref, wb_ref, o_ref, ya, yb, sem):
    tt = ya.shape[1]
    t = pl.program_id(0)

    def issue(tile):
        slot = tile % 2

        def one(r, c):
            _row_copy(y_hbm, pa_ref[tile * tt + r], ya.at[slot], r, sem.at[0, slot]).start()
            _row_copy(y_hbm, pb_ref[tile * tt + r], yb.at[slot], r, sem.at[1, slot]).start()
            return c

        lax.fori_loop(0, tt, one, 0, unroll=ROW_DMA_UNROLL)

    def wait(tile):
        slot = tile % 2

        def one(r, c):
            _row_copy(y_hbm, 0, ya.at[slot], r, sem.at[0, slot]).wait()
            _row_copy(y_hbm, 0, yb.at[slot], r, sem.at[1, slot]).wait()
            return c

        lax.fori_loop(0, tt, one, 0, unroll=ROW_DMA_UNROLL)

    @pl.when(t == 0)
    def _():
        issue(0)

    @pl.when(t + 1 < pl.num_programs(0))
    def _():
        issue(t + 1)

    wait(t)
    slot = t % 2
    o_ref[...] = x_ref[...] + g_ref[...] * (wa_ref[...] * ya[slot] + wb_ref[...] * yb[slot])


def _moe_combine(x, y, pos_a, pos_b, w_a, w_b, gate):
    S, D = x.shape
    tt = _pick(S, 256, 8)
    col = pl.BlockSpec((tt, 1), lambda t, pa, pb: (t, 0))
    return pl.pallas_call(
        _combine_kernel,
        grid_spec=pltpu.PrefetchScalarGridSpec(
            num_scalar_prefetch=2,
            grid=(S // tt,),
            in_specs=[pl.BlockSpec(memory_space=pl.ANY),
                      pl.BlockSpec((tt, D), lambda t, pa, pb: (t, 0)),
                      pl.BlockSpec((1, D), lambda t, pa, pb: (0, 0)),
                      col, col],
            out_specs=pl.BlockSpec((tt, D), lambda t, pa, pb: (t, 0)),
            scratch_shapes=[pltpu.VMEM((2, tt, D), F32), pltpu.VMEM((2, tt, D), F32),
                            pltpu.SemaphoreType.DMA((2, 2))]),
        out_shape=jax.ShapeDtypeStruct((S, D), F32),
        compiler_params=_params("arbitrary"),
        name="moe_combine",
    )(pos_a, pos_b, y, x, gate.reshape(1, D), w_a, w_b)


def _routing_tables(sel, comb, tm):
    S, E = sel.shape
    Rs = TOP_K * S + E * tm
    seli = sel.astype(jnp.int32)
    cnt = jnp.sum(seli, axis=0)
    padded = (cnt + tm - 1) // tm * tm
    gend = jnp.cumsum(padded)
    gstart = gend - padded
    pos = gstart[None, :] + jnp.cumsum(seli, axis=0) - seli
    pos_a = jnp.min(jnp.where(sel, pos, Rs), axis=1).astype(jnp.int32)
    pos_b = jnp.max(jnp.where(sel, pos, -1), axis=1).astype(jnp.int32)
    w_a = jnp.sum(jnp.where(sel & (pos == pos_a[:, None]), comb, 0.0), axis=1, keepdims=True)
    w_b = jnp.sum(jnp.where(sel & (pos == pos_b[:, None]), comb, 0.0), axis=1, keepdims=True)
    tok = jnp.arange(S, dtype=jnp.int32)
    row_token = jnp.zeros((Rs,), jnp.int32).at[jnp.concatenate([pos_a, pos_b])].set(
        jnp.concatenate([tok, tok]), mode="drop")
    n_t = Rs // tm
    tix = jnp.arange(n_t, dtype=jnp.int32)
    tile_expert = jnp.minimum(
        jnp.sum((tix[:, None] * tm >= gend[None, :]).astype(jnp.int32), axis=1), E - 1).astype(jnp.int32)
    n_rows = gend[-1].astype(jnp.int32)
    n_tiles = n_rows // tm
    prev_expert = jnp.concatenate([jnp.full((1,), -1, jnp.int32), tile_expert[:-1]])
    first = (tix < n_tiles) & (tile_expert != prev_expert)
    first_run = first & (tix == 0)
    later_first = first[None, :] & (tix[None, :] > tix[:, None])
    next_first = jnp.min(jnp.where(later_first, tix[None, :], n_t), axis=1)
    last_run = next_first >= n_t
    next_expert = jnp.where(last_run, tile_expert[0], jnp.sum(
        jnp.where(tix[None, :] == next_first[:, None], tile_expert[None, :], 0), axis=1))
    sched = jnp.concatenate([tile_expert, first.astype(jnp.int32), first_run.astype(jnp.int32), next_expert,
                             last_run.astype(jnp.int32)]).astype(jnp.int32)
    return row_token, pos_a, pos_b, w_a, w_b, sched, n_tiles.reshape(1), n_rows.reshape(1)


def _rope_tables(n_lat, n_rows, head_dim):
    n_freq = head_dim // 4
    s = jnp.arange(n_lat, dtype=jnp.int32)
    inv = ROPE_BASE ** (-jnp.arange(n_freq, dtype=F32) / n_freq)
    ang = jnp.stack([s // GRID_W, s % GRID_W], axis=-1).astype(F32)[..., None] * inv
    cos, sin = jnp.cos(ang), jnp.sin(ang)
    zero = jnp.zeros_like(sin)
    cos_t = jnp.stack([cos, cos], axis=2).reshape(n_lat, head_dim)
    sa_t = jnp.stack([-sin, zero], axis=2).reshape(n_lat, head_dim)
    sb_t = jnp.stack([zero, sin], axis=2).reshape(n_lat, head_dim)
    pad = ((0, n_rows - n_lat), (0, 0))
    return jnp.pad(cos_t, pad, constant_values=1.0), jnp.pad(sa_t, pad), jnp.pad(sb_t, pad)


def kernel(x, c, ctx, c_ctx, w_mod, b_mod, norm_gain, conv_w_in, conv_w, conv_w_out, attn_w_qkv, attn_w_o,
           attn_q_gain, attn_k_gain, attn_lambdas, attn_subln_gain, ffn_w_gu, ffn_w_down, moe_router,
           moe_w_gu, moe_w_down):
    assert x.shape[0] == 1 and w_mod.shape[0] == 2
    S, D = x.shape[1], x.shape[2]
    C = ctx.shape[1]
    R = S + C
    head_dim = attn_q_gain.shape[-1]
    n_heads = D // (2 * head_dim)
    E = moe_router.shape[-1]
    F = ffn_w_down.shape[1]
    f_pad = _round_up(F, 512)

    cc = jnp.stack([c[0], c_ctx]).reshape(2, D, 1)
    mod = _mod_vectors(cc, w_mod, b_mod)[:, :2, :].reshape(2, 2, N_MOD, D)

    xa = jnp.concatenate([x[0], ctx[0]], axis=0)

    m0 = mod[0]
    u = _modnorm(xa, norm_gain[0, 0], m0[:, 0], m0[:, 1], S, R)
    b, cv = _conv_in(u, conv_w_in)
    bz = _conv_mix(b, cv, conv_w[0], S)
    xa = _proj_resid(bz, conv_w_out, xa, m0[:, 2], S)
    u = _modnorm(xa, norm_gain[0, 1], m0[:, 3], m0[:, 4], S, R)
    w_dn = jnp.pad(ffn_w_down[0], ((0, f_pad - F), (0, 0))).astype(BF)
    h = _ffn_gu(u, ffn_w_gu, f_pad)
    xa = _ffn_down(h, w_dn, xa, m0[:, 5], S)

    m1 = mod[1]
    lambda_init = 0.8 - 0.6 * math.exp(-0.3 * 1)
    u = _modnorm(xa, norm_gain[1, 0], m1[:, 0], m1[:, 1], S, R)
    cos, sa, sb = _rope_tables(S, R, head_dim)
    qk_gain = jnp.stack([attn_q_gain[0], attn_k_gain[0]]).reshape(2, 1, head_dim)
    qk = _qk_proj(u, attn_w_qkv, qk_gain, cos, sa, sb, head_dim ** -0.5 * math.log2(math.e))
    v = _v_proj(u, attn_w_qkv)
    o = _diff_attention(qk, v, attn_lambdas[0], attn_subln_gain[0], S, n_heads, lambda_init)
    x1 = _proj_resid(o, attn_w_o, xa, m1[:, 2], S)

    tm = _pick(S, 512, BF16_SUBLANES)
    w_r = jnp.pad(moe_router[0], ((0, 0), (0, LANES - E))).astype(BF)
    comb, selm = _router(x1, norm_gain[1, 1], m1[0, 3], m1[0, 4], w_r, E)
    row_token, pos_a, pos_b, w_a, w_b, sched, counts, n_rows = _routing_tables(
        selm[:, :E] > 0, comb[:, :E], tm)
    us = _gather_norm(x1, row_token, n_rows, norm_gain[1, 1], m1[0, 3], m1[0, 4])
    hs = _moe_gu(us, moe_w_gu, sched, counts, tm)
    ys = _moe_down(hs, moe_w_down, sched, counts, tm)
    out = _moe_combine(x1, ys, pos_a, pos_b, w_a, w_b, m1[0, 5])
    return out[None]
```

```python
import functools
import math

import jax
import jax.numpy as jnp
from jax import lax
from jax.experimental import pallas as pl
from jax.experimental.pallas import tpu as pltpu

BF = jnp.bfloat16
F32 = jnp.float32

GRID_W = 64
ROPE_BASE = 10000.0
NORM_EPS = 1e-6
TOP_K = 2
N_MOD = 6

LANES = 128
BF16_SUBLANES = 16
VMEM_LIMIT_BYTES = 56 * 1024 * 1024


def _pick(n, target, align):
    best = None
    d = align
    while d <= min(n, target):
        if n % d == 0:
            best = d
        d += align
    return best if best is not None else n


def _round_up(n, m):
    return (n + m - 1) // m * m


def _params(*sem):
    return pltpu.CompilerParams(dimension_semantics=sem, vmem_limit_bytes=VMEM_LIMIT_BYTES)


def _sigmoid(x):
    return 1.0 / (1.0 + jnp.exp(-x))


def _mod_kernel(c_ref, w_ref, b_ref, o_ref, s_ref):
    @pl.when((pl.program_id(0) == 0) & (pl.program_id(1) == 0))
    def _():
        c = c_ref[...]
        s_ref[...] = jnp.broadcast_to(c * _sigmoid(c), s_ref.shape)

    o_ref[...] = jnp.zeros_like(o_ref)
    for g in range(w_ref.shape[2] // LANES):
        w = w_ref[0, :, g * LANES:(g + 1) * LANES]
        for r in range(2):
            o_ref[0, r:r + 1, g * LANES:(g + 1) * LANES] = (
                jnp.sum(w * s_ref[r], axis=0, keepdims=True) + b_ref[0, :, g * LANES:(g + 1) * LANES])


def _mod_vectors(cc, w_mod, b_mod):
    L, D, N = w_mod.shape
    tn = _pick(N, 512, LANES)
    return pl.pallas_call(
        _mod_kernel,
        grid=(L, N // tn),
        in_specs=[pl.BlockSpec((2, D, 1), lambda l, j: (0, 0, 0)),
                  pl.BlockSpec((1, D, tn), lambda l, j: (l, 0, j)),
                  pl.BlockSpec((1, 1, tn), lambda l, j: (l, 0, j))],
        out_specs=pl.BlockSpec((1, 8, tn), lambda l, j: (l, 0, j)),
        out_shape=jax.ShapeDtypeStruct((L, 8, N), F32),
        scratch_shapes=[pltpu.VMEM((2, D, LANES), F32)],
        compiler_params=_params("arbitrary", "arbitrary"),
        name="mod_vectors",
    )(cc, w_mod, b_mod.reshape(L, 1, N))


def _modnorm_rows(x, gain, shift, scale):
    ms = jnp.mean(x * x, axis=-1, keepdims=True)
    xn = x * lax.rsqrt(ms + NORM_EPS)
    return (xn * gain) * (1.0 + scale) + shift


def _modnorm_kernel(x_ref, g_ref, sh_ref, sc_ref, o_ref):
    o_ref[...] = _modnorm_rows(x_ref[...], g_ref[...], sh_ref[0], sc_ref[0]).astype(BF)


def _modnorm(x, gain, shift2, scale2, n_lat, n_rows):
    D = x.shape[1]
    tr = _pick(math.gcd(n_lat, n_rows), 256, BF16_SUBLANES)
    n_lat_tiles = n_lat // tr
    grp = lambda i: (jnp.where(i < n_lat_tiles, 0, 1), 0, 0)
    return pl.pallas_call(
        _modnorm_kernel,
        grid=(n_rows // tr,),
        in_specs=[pl.BlockSpec((tr, D), lambda i: (i, 0)),
                  pl.BlockSpec((1, D), lambda i: (0, 0)),
                  pl.BlockSpec((1, 1, D), grp),
                  pl.BlockSpec((1, 1, D), grp)],
        out_specs=pl.BlockSpec((tr, D), lambda i: (i, 0)),
        out_shape=jax.ShapeDtypeStruct((n_rows, D), BF),
        compiler_params=_params("arbitrary"),
        name="modnorm",
    )(x, gain.reshape(1, D), shift2.reshape(2, 1, D), scale2.reshape(2, 1, D))


def _staged_column_weights(w_hbm, col_blocks, wstage, w_bf, sem):
    j, i = pl.program_id(0), pl.program_id(1)
    tn = wstage.shape[-1]

    def copies(jj):
        return [pltpu.make_async_copy(w_hbm.at[0, :, pl.ds(pl.multiple_of(cb * tn, tn), tn)],
                                      wstage.at[k], sem.at[k])
                for k, cb in enumerate(col_blocks(jj))]

    @pl.when(i == 0)
    def _():
        @pl.when(j == 0)
        def _():
            for cp in copies(j):
                cp.start()

        for k, cp in enumerate(copies(j)):
            cp.wait()
            w_bf[k] = wstage[k].astype(BF)

        @pl.when(j + 1 < pl.num_programs(0))
        def _():
            for cp in copies(j + 1):
                cp.start()


def _conv_in_kernel(a_ref, w_hbm, b_ref, cv_ref, wstage, w_bf, sem):
    nj = pl.num_programs(0)
    _staged_column_weights(w_hbm, lambda jj: (jj, jj + nj, jj + 2 * nj), wstage, w_bf, sem)
    a = a_ref[...]
    b_ref[...] = jnp.dot(a, w_bf[0], preferred_element_type=F32).astype(BF)
    c = jnp.dot(a, w_bf[1], preferred_element_type=F32)
    v = jnp.dot(a, w_bf[2], preferred_element_type=F32)
    cv_ref[...] = (c * v).astype(BF)


def _conv_in(u, w_in):
    R, D = u.shape
    tm = _pick(R, 1100, BF16_SUBLANES)
    tn = _pick(D, 256, LANES)
    return pl.pallas_call(
        _conv_in_kernel,
        grid=(D // tn, R // tm),
        in_specs=[pl.BlockSpec((tm, D), lambda j, i: (i, 0)), pl.BlockSpec(memory_space=pl.ANY)],
        out_specs=[pl.BlockSpec((tm, tn), lambda j, i: (i, j)),
                   pl.BlockSpec((tm, tn), lambda j, i: (i, j))],
        out_shape=[jax.ShapeDtypeStruct((R, D), BF), jax.ShapeDtypeStruct((R, D), BF)],
        scratch_shapes=[pltpu.VMEM((3, D, tn), F32), pltpu.VMEM((3, D, tn), BF),
                        pltpu.SemaphoreType.DMA((3,))],
        compiler_params=_params("arbitrary", "arbitrary"),
        name="conv_in",
    )(u, w_in)


def _conv_mix_kernel(n_lat_tiles, b_ref, cv_ref, pv_ref, nx_ref, w_ref, o_ref):
    i = pl.program_id(0)
    tr = cv_ref.shape[0]
    cv = cv_ref[...].astype(F32)
    top_zero = (i == 0) | (i == n_lat_tiles)
    bot_zero = (i == n_lat_tiles - 1) | (i == pl.num_programs(0) - 1)
    hp = pv_ref[BF16_SUBLANES - 1:BF16_SUBLANES, :].astype(F32) * jnp.where(top_zero, 0.0, 1.0)
    hn = nx_ref[0:1, :].astype(F32) * jnp.where(bot_zero, 0.0, 1.0)
    rows = lax.broadcasted_iota(jnp.int32, cv.shape, 0)
    prev = jnp.where(rows == 0, hp, pltpu.roll(cv, 1, 0))
    nxt = jnp.where(rows == tr - 1, hn, pltpu.roll(cv, tr - 1, 0))
    w = w_ref[...]
    z = prev * w[0:1] + cv * w[1:2] + nxt * w[2:3]
    o_ref[...] = (b_ref[...].astype(F32) * z).astype(BF)


def _conv_mix(b, cv, conv_w, n_lat):
    R, D = cv.shape
    tr = _pick(math.gcd(n_lat, R), 256, BF16_SUBLANES)
    tc = _pick(D, 1024, LANES)
    hb = tr // BF16_SUBLANES
    last = R // BF16_SUBLANES - 1
    return pl.pallas_call(
        functools.partial(_conv_mix_kernel, n_lat // tr),
        grid=(R // tr, D // tc),
        in_specs=[pl.BlockSpec((tr, tc), lambda i, j: (i, j)),
                  pl.BlockSpec((tr, tc), lambda i, j: (i, j)),
                  pl.BlockSpec((BF16_SUBLANES, tc), lambda i, j: (jnp.maximum(i * hb - 1, 0), j)),
                  pl.BlockSpec((BF16_SUBLANES, tc), lambda i, j: (jnp.minimum((i + 1) * hb, last), j)),
                  pl.BlockSpec((3, tc), lambda i, j: (0, j))],
        out_specs=pl.BlockSpec((tr, tc), lambda i, j: (i, j)),
        out_shape=jax.ShapeDtypeStruct((R, D), BF),
        compiler_params=_params("arbitrary", "arbitrary"),
        name="conv_mix",
    )(b, cv, cv, cv, conv_w)


def _row_gate(g_ref, row0, shape, n_lat):
    rows = row0 + lax.broadcasted_iota(jnp.int32, shape, 0)
    return jnp.where(rows < n_lat, g_ref[0:1, :], g_ref[1:2, :])


def _proj_resid_kernel(n_lat, a_ref, w_ref, r_ref, g_ref, o_ref, w_bf):
    @pl.when(pl.program_id(1) == 0)
    def _():
        w_bf[...] = w_ref[0].astype(BF)

    tm = o_ref.shape[0]
    acc = jnp.dot(a_ref[...], w_bf[...], preferred_element_type=F32)
    g = _row_gate(g_ref, pl.program_id(1) * tm, acc.shape, n_lat)
    o_ref[...] = r_ref[...] + g * acc


def _proj_resid(a, w, resid, gate2, n_lat):
    M, K = a.shape
    N = w.shape[2]
    tm = _pick(M, 1100, BF16_SUBLANES)
    tn = _pick(N, 512, LANES)
    return pl.pallas_call(
        functools.partial(_proj_resid_kernel, n_lat),
        grid=(N // tn, M // tm),
        in_specs=[pl.BlockSpec((tm, K), lambda j, i: (i, 0)),
                  pl.BlockSpec((1, K, tn), lambda j, i: (0, 0, j)),
                  pl.BlockSpec((tm, tn), lambda j, i: (i, j)),
                  pl.BlockSpec((2, tn), lambda j, i: (0, j))],
        out_specs=pl.BlockSpec((tm, tn), lambda j, i: (i, j)),
        out_shape=jax.ShapeDtypeStruct((M, N), F32),
        scratch_shapes=[pltpu.VMEM((K, tn), BF)],
        compiler_params=_params("arbitrary", "arbitrary"),
        name="proj_resid",
    )(a, w, resid, gate2)


def _gu_kernel(n_real, a_ref, wg_ref, wu_ref, o_ref):
    j = pl.program_id(1)

    @pl.when(j < n_real)
    def _():
        a = a_ref[...]
        g = jnp.dot(a, wg_ref[0].astype(BF), preferred_element_type=F32)
        u = jnp.dot(a, wu_ref[0].astype(BF), preferred_element_type=F32)
        o_ref[...] = ((g * _sigmoid(g)) * u).astype(BF)

    @pl.when(j >= n_real)
    def _():
        o_ref[...] = jnp.zeros_like(o_ref)


def _ffn_gu(u, w_gu, f_pad):
    R, D = u.shape
    F = w_gu.shape[2] // 2
    tm = _pick(R, 2200, BF16_SUBLANES)
    tn = _pick(math.gcd(F, f_pad), 256, LANES)
    nf = F // tn
    col = lambda j: jnp.minimum(j, nf - 1)
    return pl.pallas_call(
        functools.partial(_gu_kernel, nf),
        grid=(R // tm, f_pad // tn),
        in_specs=[pl.BlockSpec((tm, D), lambda i, j: (i, 0), pipeline_mode=pl.Buffered(1)),
                  pl.BlockSpec((1, D, tn), lambda i, j: (0, 0, col(j))),
                  pl.BlockSpec((1, D, tn), lambda i, j: (0, 0, col(j) + nf))],
        out_specs=pl.BlockSpec((tm, tn), lambda i, j: (i, j)),
        out_shape=jax.ShapeDtypeStruct((R, f_pad), BF),
        compiler_params=_params("arbitrary", "arbitrary"),
        name="ffn_gu",
    )(u, w_gu, w_gu)


def _down_kernel(n_lat, a_ref, w_ref, r_ref, g_ref, o_ref, acc_ref):
    k = pl.program_id(2)

    @pl.when(k == 0)
    def _():
        acc_ref[...] = jnp.zeros_like(acc_ref)

    acc_ref[...] += jnp.dot(a_ref[...], w_ref[...], preferred_element_type=F32)

    @pl.when(k == pl.num_programs(2) - 1)
    def _():
        acc = acc_ref[...]
        g = _row_gate(g_ref, pl.program_id(0) * acc.shape[0], acc.shape, n_lat)
        o_ref[...] = r_ref[...] + g * acc


def _ffn_down(h, w_down, resid, gate2, n_lat):
    R, Fp = h.shape
    D = w_down.shape[1]
    tm = _pick(R, 1100, BF16_SUBLANES)
    tn = _pick(D, 1024, LANES)
    tk = _pick(Fp, 2816, LANES)
    return pl.pallas_call(
        functools.partial(_down_kernel, n_lat),
        grid=(R // tm, D // tn, Fp // tk),
        in_specs=[pl.BlockSpec((tm, tk), lambda i, j, k: (i, k)),
                  pl.BlockSpec((tk, tn), lambda i, j, k: (k, j)),
                  pl.BlockSpec((tm, tn), lambda i, j, k: (i, j)),
                  pl.BlockSpec((2, tn), lambda i, j, k: (0, j))],
        out_specs=pl.BlockSpec((tm, tn), lambda i, j, k: (i, j)),
        out_shape=jax.ShapeDtypeStruct((R, D), F32),
        scratch_shapes=[pltpu.VMEM((tm, tn), F32)],
        compiler_params=_params("arbitrary", "arbitrary", "arbitrary"),
        name="ffn_down",
    )(h, w_down, resid, gate2)


def _qk_kernel(n_q_tiles, n_row_tiles, q_scale, a_ref, w_hbm, gain_ref, cos_ref, sa_ref, sb_ref, o_ref,
               wstage, w_bf, sem, acc_even, acc_odd):
    j, i = pl.program_id(0), pl.program_id(1)
    _staged_column_weights(w_hbm, lambda jj: (jj,), wstage, w_bf, sem)
    post = jnp.where(j < n_q_tiles, q_scale, 1.0)

    def project(acc_ref):
        acc_ref[...] = jnp.dot(a_ref[...], w_bf[0], preferred_element_type=F32)

    def finish(acc_ref):
        gain = gain_ref[0]
        cos, sa, sb = cos_ref[...], sa_ref[...], sb_ref[...]
        half = LANES // 4
        for g in range(acc_ref.shape[1] // LANES):
            t = acc_ref[:, g * LANES:(g + 1) * LANES]
            t = t * lax.rsqrt(jnp.mean(t * t, axis=-1, keepdims=True) + NORM_EPS)
            t = t * gain
            t = t * cos + pltpu.roll(t, LANES - half, 1) * sa + pltpu.roll(t, half, 1) * sb
            o_ref[:, g * LANES:(g + 1) * LANES] = (t * post).astype(BF)

    @pl.when(i == 0)
    def _():
        project(acc_even)

    @pl.when((i > 0) & (i < n_row_tiles) & (i % 2 == 0))
    def _():
        project(acc_even)
        finish(acc_odd)

    @pl.when((i < n_row_tiles) & (i % 2 == 1))
    def _():
        project(acc_odd)
        finish(acc_even)

    @pl.when(i == n_row_tiles)
    def _():
        finish(acc_odd if n_row_tiles % 2 == 0 else acc_even)


def _qk_proj(u, w_qkv, qk_gain2, cos, sa, sb, q_scale):
    R, D = u.shape
    tm = _pick(R, 1100, BF16_SUBLANES)
    tn = _pick(D, 512, LANES)
    nq, ni = D // tn, R // tm
    done = lambda i: jnp.maximum(i - 1, 0)
    tab = pl.BlockSpec((tm, LANES), lambda j, i: (done(i), 0))
    return pl.pallas_call(
        functools.partial(_qk_kernel, nq, ni, q_scale),
        grid=(2 * nq, ni + 1),
        in_specs=[pl.BlockSpec((tm, D), lambda j, i: (jnp.minimum(i, ni - 1), 0)),
                  pl.BlockSpec(memory_space=pl.ANY),
                  pl.BlockSpec((1, 1, LANES), lambda j, i: (jnp.where(j < nq, 0, 1), 0, 0)),
                  tab, tab, tab],
        out_specs=pl.BlockSpec((tm, tn), lambda j, i: (done(i), j)),
        out_shape=jax.ShapeDtypeStruct((R, 2 * D), BF),
        scratch_shapes=[pltpu.VMEM((1, D, tn), F32), pltpu.VMEM((1, D, tn), BF), pltpu.SemaphoreType.DMA((1,)),
                        pltpu.VMEM((tm, tn), F32), pltpu.VMEM((tm, tn), F32)],
        compiler_params=_params("arbitrary", "arbitrary"),
        name="qk_proj",
    )(u, w_qkv, qk_gain2, cos, sa, sb)


def _v_kernel(first_col, a_ref, w_hbm, o_ref, wstage, w_bf, sem):
    _staged_column_weights(w_hbm, lambda jj: (first_col + jj,), wstage, w_bf, sem)
    o_ref[...] = jnp.dot(a_ref[...], w_bf[0], preferred_element_type=F32).astype(BF)


def _v_proj(u, w_qkv):
    R, D = u.shape
    tm = _pick(R, 1100, BF16_SUBLANES)
    tn = _pick(D, 512, LANES)
    return pl.pallas_call(
        functools.partial(_v_kernel, 2 * D // tn),
        grid=(D // tn, R // tm),
        in_specs=[pl.BlockSpec((tm, D), lambda j, i: (i, 0)), pl.BlockSpec(memory_space=pl.ANY)],
        out_specs=pl.BlockSpec((tm, tn), lambda j, i: (i, j)),
        out_shape=jax.ShapeDtypeStruct((R, D), BF),
        scratch_shapes=[pltpu.VMEM((1, D, tn), F32), pltpu.VMEM((1, D, tn), BF), pltpu.SemaphoreType.DMA((1,))],
        compiler_params=_params("arbitrary", "arbitrary"),
        name="v_proj",
    )(u, w_qkv)


def _attn_merges_ctx(n_lat, n_kv, tk):
    n_chunks = n_lat // tk
    return n_kv > n_lat and n_chunks % 2 == 0 and (n_kv - n_lat) % LANES == 0


def _attn_kernel(n_lat, tk, lambda_init, q_ref, k_ref, v_ref, lam_ref, sg_ref, o_ref,
                 sa_ref, sb_ref, sc_ref, m_ref, l_ref, acc_ref):
    hd = q_ref.shape[1] // 2
    n_kv = k_ref.shape[0]
    n_chunks = n_lat // tk
    has_ctx = n_kv > n_lat
    m_ref[...] = jnp.full_like(m_ref, -jnp.inf)
    l_ref[...] = jnp.zeros_like(l_ref)
    acc_ref[...] = jnp.zeros_like(acc_ref)

    def scores(kc, dst):
        for m in range(2):
            dst[m, :, :kc.shape[0]] = lax.dot_general(
                q_ref[:, m * hd:(m + 1) * hd], kc[:, m * hd:(m + 1) * hd],
                (((1,), (1,)), ((), ())), preferred_element_type=F32)

    def absorb(src, vc):
        scaled = []
        for m in range(2):
            s = src[m, :, :vc.shape[0]]
            m_old = m_ref[m]
            m_new = jnp.maximum(m_old, jnp.max(s, axis=-1, keepdims=True))
            alpha = jnp.exp2(m_old - m_new)
            p = jnp.exp2(s - m_new)
            l_ref[m] = alpha * l_ref[m] + jnp.sum(p, axis=-1, keepdims=True)
            m_ref[m] = m_new
            scaled.append((alpha, p.astype(BF)))
        for m in range(2):
            alpha, p = scaled[m]
            acc_ref[m] = alpha * acc_ref[m] + jnp.dot(p, vc, preferred_element_type=F32)

    def k_chunk(c):
        return k_ref[pl.ds(pl.multiple_of(c * tk, tk), tk), :]

    def v_chunk(c):
        return v_ref[pl.ds(pl.multiple_of(c * tk, tk), tk), :]

    scores(k_chunk(0), sa_ref)

    def pair(i, carry):
        c = 2 * i
        scores(k_chunk(c + 1), sb_ref)
        absorb(sa_ref, v_chunk(c))
        scores(k_chunk(c + 2), sa_ref)
        absorb(sb_ref, v_chunk(c + 1))
        return carry

    n_pairs = (n_chunks - 1) // 2
    lax.fori_loop(0, n_pairs, pair, 0)
    c = 2 * n_pairs
    if _attn_merges_ctx(n_lat, n_kv, tk):
        scores(k_ref[(c + 1) * tk:n_kv, :], sb_ref)
        absorb(sa_ref, v_chunk(c))
        absorb(sb_ref, v_ref[(c + 1) * tk:n_kv, :])
    else:
        if n_chunks - c == 2:
            scores(k_chunk(c + 1), sb_ref)
            absorb(sa_ref, v_chunk(c))
            if has_ctx:
                scores(k_ref[n_lat:n_kv, :], sc_ref)
            absorb(sb_ref, v_chunk(c + 1))
        else:
            if has_ctx:
                scores(k_ref[n_lat:n_kv, :], sc_ref)
            absorb(sa_ref, v_chunk(c))
        if has_ctx:
            absorb(sc_ref, v_ref[n_lat:n_kv, :])

    lf = lam_ref[...]
    lam = (jnp.exp(jnp.sum(lf[0:1] * lf[1:2], axis=-1, keepdims=True))
           - jnp.exp(jnp.sum(lf[2:3] * lf[3:4], axis=-1, keepdims=True)) + lambda_init)
    o = acc_ref[0] / l_ref[0] - lam * (acc_ref[1] / l_ref[1])
    o = o * lax.rsqrt(jnp.mean(o * o, axis=-1, keepdims=True) + NORM_EPS)
    o_ref[...] = ((o * sg_ref[...]) * (1.0 - lambda_init)).astype(BF)


def _diff_attention(qk, v, lambdas, subln_gain, n_lat, n_heads, lambda_init):
    R, D = v.shape
    vd = D // n_heads
    tq = _pick(n_lat, 512, BF16_SUBLANES)
    tk = _pick(n_lat, 2048, LANES)
    return pl.pallas_call(
        functools.partial(_attn_kernel, n_lat, tk, lambda_init),
        grid=(n_heads, n_lat // tq),
        in_specs=[pl.BlockSpec((tq, vd), lambda h, i: (i, h)),
                  pl.BlockSpec((R, vd), lambda h, i: (0, n_heads + h)),
                  pl.BlockSpec((R, vd), lambda h, i: (0, h)),
                  pl.BlockSpec(lambdas.shape, lambda h, i: (0, 0)),
                  pl.BlockSpec((1, vd), lambda h, i: (0, 0))],
        out_specs=pl.BlockSpec((tq, vd), lambda h, i: (i, h)),
        out_shape=jax.ShapeDtypeStruct((n_lat, D), BF),
        scratch_shapes=[pltpu.VMEM((2, tq, tk), F32),
                        pltpu.VMEM((2, tq, tk + (R - n_lat if _attn_merges_ctx(n_lat, R, tk) else 0)), F32),
                        pltpu.VMEM((2, tq, (R - n_lat) or LANES), F32),
                        pltpu.VMEM((2, tq, 1), F32), pltpu.VMEM((2, tq, 1), F32),
                        pltpu.VMEM((2, tq, vd), F32)],
        compiler_params=_params("arbitrary", "arbitrary"),
        name="diff_attention",
    )(qk, qk, v, lambdas, subln_gain.reshape(1, vd))


def _router_kernel(n_experts, x_ref, g_ref, sh_ref, sc_ref, w_ref, comb_ref, sel_ref):
    u = _modnorm_rows(x_ref[...], g_ref[...], sh_ref[...], sc_ref[...]).astype(BF)
    logits = jnp.dot(u, w_ref[...], preferred_element_type=F32)
    lane = lax.broadcasted_iota(jnp.int32, logits.shape, 1)
    neg = jnp.float32(-jnp.inf)
    l1 = jnp.where(lane < n_experts, logits, neg)
    m1 = jnp.max(l1, axis=-1, keepdims=True)
    i1 = jnp.min(jnp.where(l1 == m1, lane, LANES), axis=-1, keepdims=True)
    sel1 = lane == i1
    l2 = jnp.where(sel1, neg, l1)
    m2 = jnp.max(l2, axis=-1, keepdims=True)
    i2 = jnp.min(jnp.where(l2 == m2, lane, LANES), axis=-1, keepdims=True)
    sel2 = lane == i2
    e2 = jnp.exp(m2 - m1)
    den = 1.0 + e2
    comb_ref[...] = jnp.where(sel1, 1.0 / den, 0.0) + jnp.where(sel2, e2 / den, 0.0)
    sel_ref[...] = (sel1 | sel2).astype(jnp.int32)


def _router(x, gain, shift, scale, w_router_pad, n_experts):
    S, D = x.shape
    tm = _pick(S, 512, 8)
    vec = pl.BlockSpec((1, D), lambda i: (0, 0))
    out = pl.BlockSpec((tm, LANES), lambda i: (i, 0))
    return pl.pallas_call(
        functools.partial(_router_kernel, n_experts),
        grid=(S // tm,),
        in_specs=[pl.BlockSpec((tm, D), lambda i: (i, 0)), vec, vec, vec,
                  pl.BlockSpec((D, LANES), lambda i: (0, 0))],
        out_specs=[out, out],
        out_shape=[jax.ShapeDtypeStruct((S, LANES), F32), jax.ShapeDtypeStruct((S, LANES), jnp.int32)],
        compiler_params=_params("arbitrary"),
        name="moe_router",
    )(x, gain.reshape(1, D), shift.reshape(1, D), scale.reshape(1, D), w_router_pad)


def _row_copy(src_hbm, row, dst, r, sem):
    return pltpu.make_async_copy(src_hbm.at[pl.ds(row, 1)], dst.at[pl.ds(r, 1)], sem)


ROW_DMA_UNROLL = 8


def _gather_norm_kernel(tok_ref, nrows_ref, x_hbm, g_ref, sh_ref, sc_ref, o_ref, buf, sem):
    tg = buf.shape[1]
    t = pl.program_id(0)

    def used(tile):
        return tile * tg < nrows_ref[0]

    def issue(tile):
        slot = tile % 2

        def one(r, c):
            _row_copy(x_hbm, tok_ref[tile * tg + r], buf.at[slot], r, sem.at[slot]).start()
            return c

        lax.fori_loop(0, tg, one, 0, unroll=ROW_DMA_UNROLL)

    def wait(tile):
        slot = tile % 2

        def one(r, c):
            _row_copy(x_hbm, 0, buf.at[slot], r, sem.at[slot]).wait()
            return c

        lax.fori_loop(0, tg, one, 0, unroll=ROW_DMA_UNROLL)

    @pl.when((t == 0) & used(0))
    def _():
        issue(0)

    @pl.when((t + 1 < pl.num_programs(0)) & used(t + 1))
    def _():
        issue(t + 1)

    @pl.when(used(t))
    def _():
        wait(t)
        o_ref[...] = _modnorm_rows(buf[t % 2], g_ref[...], sh_ref[...], sc_ref[...]).astype(BF)

    @pl.when(jnp.logical_not(used(t)))
    def _():
        o_ref[...] = jnp.zeros_like(o_ref)


def _gather_norm(x, row_token, n_rows_used, gain, shift, scale):
    D = x.shape[1]
    Rs = row_token.shape[0]
    tg = _pick(Rs, 256, BF16_SUBLANES)
    vec = pl.BlockSpec((1, D), lambda t, tok, n: (0, 0))
    return pl.pallas_call(
        _gather_norm_kernel,
        grid_spec=pltpu.PrefetchScalarGridSpec(
            num_scalar_prefetch=2,
            grid=(Rs // tg,),
            in_specs=[pl.BlockSpec(memory_space=pl.ANY), vec, vec, vec],
            out_specs=pl.BlockSpec((tg, D), lambda t, tok, n: (t, 0)),
            scratch_shapes=[pltpu.VMEM((2, tg, D), F32), pltpu.SemaphoreType.DMA((2,))]),
        out_shape=jax.ShapeDtypeStruct((Rs, D), BF),
        compiler_params=_params("arbitrary"),
        name="moe_gather_norm",
    )(row_token, n_rows_used, x, gain.reshape(1, D), shift.reshape(1, D), scale.reshape(1, D))


SCHED_EXPERT, SCHED_RUN_START, SCHED_FIRST_RUN, SCHED_NEXT_EXPERT, SCHED_LAST_RUN = range(5)


def _staged_expert_weights(sched_ref, cnt_ref, w_hbm, col_blocks, wstage, w_bf, sem):
    j, t = pl.program_id(0), pl.program_id(1)
    nj, n_t = pl.num_programs(0), pl.num_programs(1)
    tn = wstage.shape[-1]

    def copies(e, jj):
        return [pltpu.make_async_copy(w_hbm.at[0, e, :, pl.ds(pl.multiple_of(cb * tn, tn), tn)],
                                      wstage.at[k], sem.at[k])
                for k, cb in enumerate(col_blocks(jj))]

    def sched(row):
        return sched_ref[row * n_t + t]

    @pl.when(sched(SCHED_RUN_START) == 1)
    def _():
        e = sched(SCHED_EXPERT)

        @pl.when((j == 0) & (sched(SCHED_FIRST_RUN) == 1))
        def _():
            for cp in copies(e, j):
                cp.start()

        for k, cp in enumerate(copies(e, j)):
            cp.wait()
            w_bf[k] = wstage[k].astype(BF)

        next_j = j + sched(SCHED_LAST_RUN)

        @pl.when(next_j < nj)
        def _():
            for cp in copies(sched(SCHED_NEXT_EXPERT), next_j):
                cp.start()


def _moe_gu_kernel(sched_ref, cnt_ref, a_ref, w_hbm, o_ref, wstage, w_bf, sem):
    t = pl.program_id(1)
    nj = pl.num_programs(0)

    @pl.when(t < cnt_ref[0])
    def _():
        _staged_expert_weights(sched_ref, cnt_ref, w_hbm, lambda jj: (jj, jj + nj), wstage, w_bf, sem)
        a = a_ref[...]
        g = jnp.dot(a, w_bf[0], preferred_element_type=F32)
        u = jnp.dot(a, w_bf[1], preferred_element_type=F32)
        o_ref[...] = ((g * _sigmoid(g)) * u).astype(BF)

    @pl.when(t >= cnt_ref[0])
    def _():
        o_ref[...] = jnp.zeros_like(o_ref)


def _moe_gu(a, w_gu, sched, counts, tm):
    Rs, D = a.shape
    F = w_gu.shape[3] // 2
    tn = _pick(F, 512, LANES)
    tile = lambda t, cnt: jnp.minimum(t, cnt[0] - 1)
    return pl.pallas_call(
        _moe_gu_kernel,
        grid_spec=pltpu.PrefetchScalarGridSpec(
            num_scalar_prefetch=2,
            grid=(F // tn, Rs // tm),
            in_specs=[pl.BlockSpec((tm, D), lambda j, t, sc, cnt: (tile(t, cnt), 0)),
                      pl.BlockSpec(memory_space=pl.ANY)],
            out_specs=pl.BlockSpec((tm, tn), lambda j, t, sc, cnt: (t, j)),
            scratch_shapes=[pltpu.VMEM((2, D, tn), F32), pltpu.VMEM((2, D, tn), BF),
                            pltpu.SemaphoreType.DMA((2,))]),
        out_shape=jax.ShapeDtypeStruct((Rs, F), BF),
        compiler_params=_params("arbitrary", "arbitrary"),
        name="moe_gu",
    )(sched, counts, a, w_gu)


def _moe_down_kernel(sched_ref, cnt_ref, a_ref, w_hbm, o_ref, wstage, w_bf, sem):
    t = pl.program_id(1)

    @pl.when(t < cnt_ref[0])
    def _():
        _staged_expert_weights(sched_ref, cnt_ref, w_hbm, lambda jj: (jj,), wstage, w_bf, sem)
        o_ref[...] = jnp.dot(a_ref[...], w_bf[0], preferred_element_type=F32)

    @pl.when(t >= cnt_ref[0])
    def _():
        o_ref[...] = jnp.zeros_like(o_ref)


def _moe_down(h, w_down, sched, counts, tm):
    Rs, F = h.shape
    D = w_down.shape[3]
    tn = _pick(D, 1024, LANES)
    tile = lambda t, cnt: jnp.minimum(t, cnt[0] - 1)
    return pl.pallas_call(
        _moe_down_kernel,
        grid_spec=pltpu.PrefetchScalarGridSpec(
            num_scalar_prefetch=2,
            grid=(D // tn, Rs // tm),
            in_specs=[pl.BlockSpec((tm, F), lambda j, t, sc, cnt: (tile(t, cnt), 0)),
                      pl.BlockSpec(memory_space=pl.ANY)],
            out_specs=pl.BlockSpec((tm, tn), lambda j, t, sc, cnt: (t, j)),
            scratch_shapes=[pltpu.VMEM((1, F, tn), F32), pltpu.VMEM((1, F, tn), BF),
                            pltpu.SemaphoreType.DMA((1,))]),
        out_shape=jax.ShapeDtypeStruct((Rs, D), F32),
        compiler_params=_params("arbitrary", "arbitrary"),
        name="moe_down",
    )(sched, counts, h, w_down)


def _combine_kernel(pa_ref, pb_ref, y_hbm, x_ref, g_ref, wa_ref, wb_ref, o_ref, ya, yb, sem):
    tt = ya.shape[1]
    t = pl.program_id(0)

    def issue(tile):
        slot = tile % 2

        def one(r, c):
            _row_copy(y_hbm, pa_ref[tile * tt + r], ya.at[slot], r, sem.at[0, slot]).start()
            _row_copy(y_hbm, pb_ref[tile * tt + r], yb.at[slot], r, sem.at[1, slot]).start()
            return c

        lax.fori_loop(0, tt, one, 0, unroll=ROW_DMA_UNROLL)

    def wait(tile):
        slot = tile % 2

        def one(r, c):
            _row_copy(y_hbm, 0, ya.at[slot], r, sem.at[0, slot]).wait()
            _row_copy(y_hbm, 0, yb.at[slot], r, sem.at[1, slot]).wait()
            return c

        lax.fori_loop(0, tt, one, 0, unroll=ROW_DMA_UNROLL)

    @pl.when(t == 0)
    def _():
        issue(0)

    @pl.when(t + 1 < pl.num_programs(0))
    def _():
        issue(t + 1)

    wait(t)
    slot = t % 2
    o_ref[...] = x_ref[...] + g_ref[...] * (wa_ref[...] * ya[slot] + wb_ref[...] * yb[slot])


def _moe_combine(x, y, pos_a, pos_b, w_a, w_b, gate):
    S, D = x.shape
    tt = _pick(S, 256, 8)
    col = pl.BlockSpec((tt, 1), lambda t, pa, pb: (t, 0))
    return pl.pallas_call(
        _combine_kernel,
        grid_spec=pltpu.PrefetchScalarGridSpec(
            num_scalar_prefetch=2,
            grid=(S // tt,),
            in_specs=[pl.BlockSpec(memory_space=pl.ANY),
                      pl.BlockSpec((tt, D), lambda t, pa, pb: (t, 0)),
                      pl.BlockSpec((1, D), lambda t, pa, pb: (0, 0)),
                      col, col],
            out_specs=pl.BlockSpec((tt, D), lambda t, pa, pb: (t, 0)),
            scratch_shapes=[pltpu.VMEM((2, tt, D), F32), pltpu.VMEM((2, tt, D), F32),
                            pltpu.SemaphoreType.DMA((2, 2))]),
        out_shape=jax.ShapeDtypeStruct((S, D), F32),
        compiler_params=_params("arbitrary"),
        name="moe_combine",
    )(pos_a, pos_b, y, x, gate.reshape(1, D), w_a, w_b)


def _routing_tables(sel, comb, tm):
    S, E = sel.shape
    Rs = TOP_K * S + E * tm
    seli = sel.astype(jnp.int32)
    cnt = jnp.sum(seli, axis=0)
    padded = (cnt + tm - 1) // tm * tm
    gend = jnp.cumsum(padded)
    gstart = gend - padded
    pos = gstart[None, :] + jnp.cumsum(seli, axis=0) - seli
    pos_a = jnp.min(jnp.where(sel, pos, Rs), axis=1).astype(jnp.int32)
    pos_b = jnp.max(jnp.where(sel, pos, -1), axis=1).astype(jnp.int32)
    w_a = jnp.sum(jnp.where(sel & (pos == pos_a[:, None]), comb, 0.0), axis=1, keepdims=True)
    w_b = jnp.sum(jnp.where(sel & (pos == pos_b[:, None]), comb, 0.0), axis=1, keepdims=True)
    tok = jnp.arange(S, dtype=jnp.int32)
    row_token = jnp.zeros((Rs,), jnp.int32).at[jnp.concatenate([pos_a, pos_b])].set(
        jnp.concatenate([tok, tok]), mode="drop")
    n_t = Rs // tm
    tix = jnp.arange(n_t, dtype=jnp.int32)
    tile_expert = jnp.minimum(
        jnp.sum((tix[:, None] * tm >= gend[None, :]).astype(jnp.int32), axis=1), E - 1).astype(jnp.int32)
    n_rows = gend[-1].astype(jnp.int32)
    n_tiles = n_rows // tm
    prev_expert = jnp.concatenate([jnp.full((1,), -1, jnp.int32), tile_expert[:-1]])
    first = (tix < n_tiles) & (tile_expert != prev_expert)
    first_run = first & (tix == 0)
    later_first = first[None, :] & (tix[None, :] > tix[:, None])
    next_first = jnp.min(jnp.where(later_first, tix[None, :], n_t), axis=1)
    last_run = next_first >= n_t
    next_expert = jnp.where(last_run, tile_expert[0], jnp.sum(
        jnp.where(tix[None, :] == next_first[:, None], tile_expert[None, :], 0), axis=1))
    sched = jnp.concatenate([tile_expert, first.astype(jnp.int32), first_run.astype(jnp.int32), next_expert,
                             last_run.astype(jnp.int32)]).astype(jnp.int32)
    return row_token, pos_a, pos_b, w_a, w_b, sched, n_tiles.reshape(1), n_rows.reshape(1)


def _rope_tables(n_lat, n_rows, head_dim):
    n_freq = head_dim // 4
    s = jnp.arange(n_lat, dtype=jnp.int32)
    inv = ROPE_BASE ** (-jnp.arange(n_freq, dtype=F32) / n_freq)
    ang = jnp.stack([s // GRID_W, s % GRID_W], axis=-1).astype(F32)[..., None] * inv
    cos, sin = jnp.cos(ang), jnp.sin(ang)
    zero = jnp.zeros_like(sin)
    cos_t = jnp.stack([cos, cos], axis=2).reshape(n_lat, head_dim)
    sa_t = jnp.stack([-sin, zero], axis=2).reshape(n_lat, head_dim)
    sb_t = jnp.stack([zero, sin], axis=2).reshape(n_lat, head_dim)
    pad = ((0, n_rows - n_lat), (0, 0))
    return jnp.pad(cos_t, pad, constant_values=1.0), jnp.pad(sa_t, pad), jnp.pad(sb_t, pad)


def kernel(x, c, ctx, c_ctx, w_mod, b_mod, norm_gain, conv_w_in, conv_w, conv_w_out, attn_w_qkv, attn_w_o,
           attn_q_gain, attn_k_gain, attn_lambdas, attn_subln_gain, ffn_w_gu, ffn_w_down, moe_router,
           moe_w_gu, moe_w_down):
    assert x.shape[0] == 1 and w_mod.shape[0] == 2
    S, D = x.shape[1], x.shape[2]
    C = ctx.shape[1]
    R = S + C
    head_dim = attn_q_gain.shape[-1]
    n_heads = D // (2 * head_dim)
    E = moe_router.shape[-1]
    F = ffn_w_down.shape[1]
    f_pad = _round_up(F, 512)

    cc = jnp.stack([c[0], c_ctx]).reshape(2, D, 1)
    mod = _mod_vectors(cc, w_mod, b_mod)[:, :2, :].reshape(2, 2, N_MOD, D)

    xa = jnp.concatenate([x[0], ctx[0]], axis=0)

    m0 = mod[0]
    u = _modnorm(xa, norm_gain[0, 0], m0[:, 0], m0[:, 1], S, R)
    b, cv = _conv_in(u, conv_w_in)
    bz = _conv_mix(b, cv, conv_w[0], S)
    xa = _proj_resid(bz, conv_w_out, xa, m0[:, 2], S)
    u = _modnorm(xa, norm_gain[0, 1], m0[:, 3], m0[:, 4], S, R)
    w_dn = jnp.pad(ffn_w_down[0], ((0, f_pad - F), (0, 0))).astype(BF)
    h = _ffn_gu(u, ffn_w_gu, f_pad)
    xa = _ffn_down(h, w_dn, xa, m0[:, 5], S)

    m1 = mod[1]
    lambda_init = 0.8 - 0.6 * math.exp(-0.3 * 1)
    u = _modnorm(xa, norm_gain[1, 0], m1[:, 0], m1[:, 1], S, R)
    cos, sa, sb = _rope_tables(S, R, head_dim)
    qk_gain = jnp.stack([attn_q_gain[0], attn_k_gain[0]]).reshape(2, 1, head_dim)
    qk = _qk_proj(u, attn_w_qkv, qk_gain, cos, sa, sb, head_dim ** -0.5 * math.log2(math.e))
    v = _v_proj(u, attn_w_qkv)
    o = _diff_attention(qk, v, attn_lambdas[0], attn_subln_gain[0], S, n_heads, lambda_init)
    x1 = _proj_resid(o, attn_w_o, xa, m1[:, 2], S)

    tm = _pick(S, 512, BF16_SUBLANES)
    w_r = jnp.pad(moe_router[0], ((0, 0), (0, LANES - E))).astype(BF)
    comb, selm = _router(x1, norm_gain[1, 1], m1[0, 3], m1[0, 4], w_r, E)
    row_token, pos_a, pos_b, w_a, w_b, sched, counts, n_rows = _routing_tables(
        selm[:, :E] > 0, comb[:, :E], tm)
    us = _gather_norm(x1, row_token, n_rows, norm_gain[1, 1], m1[0, 3], m1[0, 4])
    hs = _moe_gu(us, moe_w_gu, sched, counts, tm)
    ys = _moe_down(hs, moe_w_down, sched, counts, tm)
    out = _moe_combine(x1, ys, pos_a, pos_b, w_a, w_b, m1[0, 5])
    return out[None]
```

```python
import functools
import math

import jax
import jax.numpy as jnp
from jax import lax
from jax.experimental import pallas as pl
from jax.experimental.pallas import tpu as pltpu

BF = jnp.bfloat16
F32 = jnp.float32

GRID_W = 64
ROPE_BASE = 10000.0
NORM_EPS = 1e-6
TOP_K = 2
N_MOD = 6

LANES = 128
BF16_SUBLANES = 16
VMEM_LIMIT_BYTES = 56 * 1024 * 1024


def _pick(n, target, align):
    best = None
    d = align
    while d <= min(n, target):
        if n % d == 0:
            best = d
        d += align
    return best if best is not None else n


def _round_up(n, m):
    return (n + m - 1) // m * m


def _params(*sem):
    return pltpu.CompilerParams(dimension_semantics=sem, vmem_limit_bytes=VMEM_LIMIT_BYTES)


def _sigmoid(x):
    return 1.0 / (1.0 + jnp.exp(-x))


def _mod_kernel(c_ref, w_ref, b_ref, o_ref, s_ref):
    @pl.when((pl.program_id(0) == 0) & (pl.program_id(1) == 0))
    def _():
        c = c_ref[...]
        s_ref[...] = jnp.broadcast_to(c * _sigmoid(c), s_ref.shape)

    o_ref[...] = jnp.zeros_like(o_ref)
    for g in range(w_ref.shape[2] // LANES):
        w = w_ref[0, :, g * LANES:(g + 1) * LANES]
        for r in range(2):
            o_ref[0, r:r + 1, g * LANES:(g + 1) * LANES] = (
                jnp.sum(w * s_ref[r], axis=0, keepdims=True) + b_ref[0, :, g * LANES:(g + 1) * LANES])


def _mod_vectors(cc, w_mod, b_mod):
    L, D, N = w_mod.shape
    tn = _pick(N, 512, LANES)
    return pl.pallas_call(
        _mod_kernel,
        grid=(L, N // tn),
        in_specs=[pl.BlockSpec((2, D, 1), lambda l, j: (0, 0, 0)),
                  pl.BlockSpec((1, D, tn), lambda l, j: (l, 0, j)),
                  pl.BlockSpec((1, 1, tn), lambda l, j: (l, 0, j))],
        out_specs=pl.BlockSpec((1, 8, tn), lambda l, j: (l, 0, j)),
        out_shape=jax.ShapeDtypeStruct((L, 8, N), F32),
        scratch_shapes=[pltpu.VMEM((2, D, LANES), F32)],
        compiler_params=_params("arbitrary", "arbitrary"),
        name="mod_vectors",
    )(cc, w_mod, b_mod.reshape(L, 1, N))


def _modnorm_rows(x, gain, shift, scale):
    ms = jnp.mean(x * x, axis=-1, keepdims=True)
    xn = x * lax.rsqrt(ms + NORM_EPS)
    return (xn * gain) * (1.0 + scale) + shift


def _modnorm_kernel(x_ref, g_ref, sh_ref, sc_ref, o_ref):
    o_ref[...] = _modnorm_rows(x_ref[...], g_ref[...], sh_ref[0], sc_ref[0]).astype(BF)


def _modnorm(x, gain, shift2, scale2, n_lat, n_rows):
    D = x.shape[1]
    tr = _pick(math.gcd(n_lat, n_rows), 256, BF16_SUBLANES)
    n_lat_tiles = n_lat // tr
    grp = lambda i: (jnp.where(i < n_lat_tiles, 0, 1), 0, 0)
    return pl.pallas_call(
        _modnorm_kernel,
        grid=(n_rows // tr,),
        in_specs=[pl.BlockSpec((tr, D), lambda i: (i, 0)),
                  pl.BlockSpec((1, D), lambda i: (0, 0)),
                  pl.BlockSpec((1, 1, D), grp),
                  pl.BlockSpec((1, 1, D), grp)],
        out_specs=pl.BlockSpec((tr, D), lambda i: (i, 0)),
        out_shape=jax.ShapeDtypeStruct((n_rows, D), BF),
        compiler_params=_params("arbitrary"),
        name="modnorm",
    )(x, gain.reshape(1, D), shift2.reshape(2, 1, D), scale2.reshape(2, 1, D))


def _staged_column_weights(w_hbm, col_blocks, wstage, w_bf, sem):
    j, i = pl.program_id(0), pl.program_id(1)
    tn = wstage.shape[-1]

    def copies(jj):
        return [pltpu.make_async_copy(w_hbm.at[0, :, pl.ds(pl.multiple_of(cb * tn, tn), tn)],
                                      wstage.at[k], sem.at[k])
                for k, cb in enumerate(col_blocks(jj))]

    @pl.when(i == 0)
    def _():
        @pl.when(j == 0)
        def _():
            for cp in copies(j):
                cp.start()

        for k, cp in enumerate(copies(j)):
            cp.wait()
            w_bf[k] = wstage[k].astype(BF)

        @pl.when(j + 1 < pl.num_programs(0))
        def _():
            for cp in copies(j + 1):
                cp.start()


def _conv_in_kernel(a_ref, w_hbm, b_ref, cv_ref, wstage, w_bf, sem):
    nj = pl.num_programs(0)
    _staged_column_weights(w_hbm, lambda jj: (jj, jj + nj, jj + 2 * nj), wstage, w_bf, sem)
    a = a_ref[...]
    b_ref[...] = jnp.dot(a, w_bf[0], preferred_element_type=F32).astype(BF)
    c = jnp.dot(a, w_bf[1], preferred_element_type=F32)
    v = jnp.dot(a, w_bf[2], preferred_element_type=F32)
    cv_ref[...] = (c * v).astype(BF)


def _conv_in(u, w_in):
    R, D = u.shape
    tm = _pick(R, 1100, BF16_SUBLANES)
    tn = _pick(D, 256, LANES)
    return pl.pallas_call(
        _conv_in_kernel,
        grid=(D // tn, R // tm),
        in_specs=[pl.BlockSpec((tm, D), lambda j, i: (i, 0)), pl.BlockSpec(memory_space=pl.ANY)],
        out_specs=[pl.BlockSpec((tm, tn), lambda j, i: (i, j)),
                   pl.BlockSpec((tm, tn), lambda j, i: (i, j))],
        out_shape=[jax.ShapeDtypeStruct((R, D), BF), jax.ShapeDtypeStruct((R, D), BF)],
        scratch_shapes=[pltpu.VMEM((3, D, tn), F32), pltpu.VMEM((3, D, tn), BF),
                        pltpu.SemaphoreType.DMA((3,))],
        compiler_params=_params("arbitrary", "arbitrary"),
        name="conv_in",
    )(u, w_in)


def _conv_mix_kernel(n_lat_tiles, b_ref, cv_ref, pv_ref, nx_ref, w_ref, o_ref):
    i = pl.program_id(0)
    tr = cv_ref.shape[0]
    cv = cv_ref[...].astype(F32)
    top_zero = (i == 0) | (i == n_lat_tiles)
    bot_zero = (i == n_lat_tiles - 1) | (i == pl.num_programs(0) - 1)
    hp = pv_ref[BF16_SUBLANES - 1:BF16_SUBLANES, :].astype(F32) * jnp.where(top_zero, 0.0, 1.0)
    hn = nx_ref[0:1, :].astype(F32) * jnp.where(bot_zero, 0.0, 1.0)
    rows = lax.broadcasted_iota(jnp.int32, cv.shape, 0)
    prev = jnp.where(rows == 0, hp, pltpu.roll(cv, 1, 0))
    nxt = jnp.where(rows == tr - 1, hn, pltpu.roll(cv, tr - 1, 0))
    w = w_ref[...]
    z = prev * w[0:1] + cv * w[1:2] + nxt * w[2:3]
    o_ref[...] = (b_ref[...].astype(F32) * z).astype(BF)


def _conv_mix(b, cv, conv_w, n_lat):
    R, D = cv.shape
    tr = _pick(math.gcd(n_lat, R), 256, BF16_SUBLANES)
    tc = _pick(D, 1024, LANES)
    hb = tr // BF16_SUBLANES
    last = R // BF16_SUBLANES - 1
    return pl.pallas_call(
        functools.partial(_conv_mix_kernel, n_lat // tr),
        grid=(R // tr, D // tc),
        in_specs=[pl.BlockSpec((tr, tc), lambda i, j: (i, j)),
                  pl.BlockSpec((tr, tc), lambda i, j: (i, j)),
                  pl.BlockSpec((BF16_SUBLANES, tc), lambda i, j: (jnp.maximum(i * hb - 1, 0), j)),
                  pl.BlockSpec((BF16_SUBLANES, tc), lambda i, j: (jnp.minimum((i + 1) * hb, last), j)),
                  pl.BlockSpec((3, tc), lambda i, j: (0, j))],
        out_specs=pl.BlockSpec((tr, tc), lambda i, j: (i, j)),
        out_shape=jax.ShapeDtypeStruct((R, D), BF),
        compiler_params=_params("arbitrary", "arbitrary"),
        name="conv_mix",
    )(b, cv, cv, cv, conv_w)


def _row_gate(g_ref, row0, shape, n_lat):
    rows = row0 + lax.broadcasted_iota(jnp.int32, shape, 0)
    return jnp.where(rows < n_lat, g_ref[0:1, :], g_ref[1:2, :])


def _proj_resid_kernel(n_lat, a_ref, w_ref, r_ref, g_ref, o_ref, w_bf):
    @pl.when(pl.program_id(1) == 0)
    def _():
        w_bf[...] = w_ref[0].astype(BF)

    tm = o_ref.shape[0]
    acc = jnp.dot(a_ref[...], w_bf[...], preferred_element_type=F32)
    g = _row_gate(g_ref, pl.program_id(1) * tm, acc.shape, n_lat)
    o_ref[...] = r_ref[...] + g * acc


def _proj_resid(a, w, resid, gate2, n_lat):
    M, K = a.shape
    N = w.shape[2]
    tm = _pick(M, 1100, BF16_SUBLANES)
    tn = _pick(N, 512, LANES)
    return pl.pallas_call(
        functools.partial(_proj_resid_kernel, n_lat),
        grid=(N // tn, M // tm),
        in_specs=[pl.BlockSpec((tm, K), lambda j, i: (i, 0)),
                  pl.BlockSpec((1, K, tn), lambda j, i: (0, 0, j)),
                  pl.BlockSpec((tm, tn), lambda j, i: (i, j)),
                  pl.BlockSpec((2, tn), lambda j, i: (0, j))],
        out_specs=pl.BlockSpec((tm, tn), lambda j, i: (i, j)),
        out_shape=jax.ShapeDtypeStruct((M, N), F32),
        scratch_shapes=[pltpu.VMEM((K, tn), BF)],
        compiler_params=_params("arbitrary", "arbitrary"),
        name="proj_resid",
    )(a, w, resid, gate2)


def _gu_kernel(n_real, a_ref, wg_ref, wu_ref, o_ref):
    j = pl.program_id(1)

    @pl.when(j < n_real)
    def _():
        a = a_ref[...]
        g = jnp.dot(a, wg_ref[0].astype(BF), preferred_element_type=F32)
        u = jnp.dot(a, wu_ref[0].astype(BF), preferred_element_type=F32)
        o_ref[...] = ((g * _sigmoid(g)) * u).astype(BF)

    @pl.when(j >= n_real)
    def _():
        o_ref[...] = jnp.zeros_like(o_ref)


def _ffn_gu(u, w_gu, f_pad):
    R, D = u.shape
    F = w_gu.shape[2] // 2
    tm = _pick(R, 2200, BF16_SUBLANES)
    tn = _pick(math.gcd(F, f_pad), 256, LANES)
    nf = F // tn
    col = lambda j: jnp.minimum(j, nf - 1)
    return pl.pallas_call(
        functools.partial(_gu_kernel, nf),
        grid=(R // tm, f_pad // tn),
        in_specs=[pl.BlockSpec((tm, D), lambda i, j: (i, 0), pipeline_mode=pl.Buffered(1)),
                  pl.BlockSpec((1, D, tn), lambda i, j: (0, 0, col(j))),
                  pl.BlockSpec((1, D, tn), lambda i, j: (0, 0, col(j) + nf))],
        out_specs=pl.BlockSpec((tm, tn), lambda i, j: (i, j)),
        out_shape=jax.ShapeDtypeStruct((R, f_pad), BF),
        compiler_params=_params("arbitrary", "arbitrary"),
        name="ffn_gu",
    )(u, w_gu, w_gu)


def _down_kernel(n_lat, a_ref, w_ref, r_ref, g_ref, o_ref, acc_ref):
    k = pl.program_id(2)

    @pl.when(k == 0)
    def _():
        acc_ref[...] = jnp.zeros_like(acc_ref)

    acc_ref[...] += jnp.dot(a_ref[...], w_ref[...], preferred_element_type=F32)

    @pl.when(k == pl.num_programs(2) - 1)
    def _():
        acc = acc_ref[...]
        g = _row_gate(g_ref, pl.program_id(0) * acc.shape[0], acc.shape, n_lat)
        o_ref[...] = r_ref[...] + g * acc


def _ffn_down(h, w_down, resid, gate2, n_lat):
    R, Fp = h.shape
    D = w_down.shape[1]
    tm = _pick(R, 1100, BF16_SUBLANES)
    tn = _pick(D, 1024, LANES)
    tk = _pick(Fp, 2816, LANES)
    return pl.pallas_call(
        functools.partial(_down_kernel, n_lat),
        grid=(R // tm, D // tn, Fp // tk),
        in_specs=[pl.BlockSpec((tm, tk), lambda i, j, k: (i, k)),
                  pl.BlockSpec((tk, tn), lambda i, j, k: (k, j)),
                  pl.BlockSpec((tm, tn), lambda i, j, k: (i, j)),
                  pl.BlockSpec((2, tn), lambda i, j, k: (0, j))],
        out_specs=pl.BlockSpec((tm, tn), lambda i, j, k: (i, j)),
        out_shape=jax.ShapeDtypeStruct((R, D), F32),
        scratch_shapes=[pltpu.VMEM((tm, tn), F32)],
        compiler_params=_params("arbitrary", "arbitrary", "arbitrary"),
        name="ffn_down",
    )(h, w_down, resid, gate2)


def _qk_kernel(n_q_tiles, n_row_tiles, q_scale, a_ref, w_hbm, gain_ref, cos_ref, sa_ref, sb_ref, o_ref,
               wstage, w_bf, sem, acc_even, acc_odd):
    j, i = pl.program_id(0), pl.program_id(1)
    _staged_column_weights(w_hbm, lambda jj: (jj,), wstage, w_bf, sem)
    post = jnp.where(j < n_q_tiles, q_scale, 1.0)

    def project(acc_ref):
        acc_ref[...] = jnp.dot(a_ref[...], w_bf[0], preferred_element_type=F32)

    def finish(acc_ref):
        gain = gain_ref[0]
        cos, sa, sb = cos_ref[...], sa_ref[...], sb_ref[...]
        half = LANES // 4
        for g in range(acc_ref.shape[1] // LANES):
            t = acc_ref[:, g * LANES:(g + 1) * LANES]
            t = t * lax.rsqrt(jnp.mean(t * t, axis=-1, keepdims=True) + NORM_EPS)
            t = t * gain
            t = t * cos + pltpu.roll(t, LANES - half, 1) * sa + pltpu.roll(t, half, 1) * sb
            o_ref[:, g * LANES:(g + 1) * LANES] = (t * post).astype(BF)

    @pl.when(i == 0)
    def _():
        project(acc_even)

    @pl.when((i > 0) & (i < n_row_tiles) & (i % 2 == 0))
    def _():
        project(acc_even)
        finish(acc_odd)

    @pl.when((i < n_row_tiles) & (i % 2 == 1))
    def _():
        project(acc_odd)
        finish(acc_even)

    @pl.when(i == n_row_tiles)
    def _():
        finish(acc_odd if n_row_tiles % 2 == 0 else acc_even)


def _qk_proj(u, w_qkv, qk_gain2, cos, sa, sb, q_scale):
    R, D = u.shape
    tm = _pick(R, 1100, BF16_SUBLANES)
    tn = _pick(D, 512, LANES)
    nq, ni = D // tn, R // tm
    done = lambda i: jnp.maximum(i - 1, 0)
    tab = pl.BlockSpec((tm, LANES), lambda j, i: (done(i), 0))
    return pl.pallas_call(
        functools.partial(_qk_kernel, nq, ni, q_scale),
        grid=(2 * nq, ni + 1),
        in_specs=[pl.BlockSpec((tm, D), lambda j, i: (jnp.minimum(i, ni - 1), 0)),
                  pl.BlockSpec(memory_space=pl.ANY),
                  pl.BlockSpec((1, 1, LANES), lambda j, i: (jnp.where(j < nq, 0, 1), 0, 0)),
                  tab, tab, tab],
        out_specs=pl.BlockSpec((tm, tn), lambda j, i: (done(i), j)),
        out_shape=jax.ShapeDtypeStruct((R, 2 * D), BF),
        scratch_shapes=[pltpu.VMEM((1, D, tn), F32), pltpu.VMEM((1, D, tn), BF), pltpu.SemaphoreType.DMA((1,)),
                        pltpu.VMEM((tm, tn), F32), pltpu.VMEM((tm, tn), F32)],
        compiler_params=_params("arbitrary", "arbitrary"),
        name="qk_proj",
    )(u, w_qkv, qk_gain2, cos, sa, sb)


def _v_kernel(first_col, a_ref, w_hbm, o_ref, wstage, w_bf, sem):
    _staged_column_weights(w_hbm, lambda jj: (first_col + jj,), wstage, w_bf, sem)
    o_ref[...] = jnp.dot(a_ref[...], w_bf[0], preferred_element_type=F32).astype(BF)


def _v_proj(u, w_qkv):
    R, D = u.shape
    tm = _pick(R, 1100, BF16_SUBLANES)
    tn = _pick(D, 512, LANES)
    return pl.pallas_call(
        functools.partial(_v_kernel, 2 * D // tn),
        grid=(D // tn, R // tm),
        in_specs=[pl.BlockSpec((tm, D), lambda j, i: (i, 0)), pl.BlockSpec(memory_space=pl.ANY)],
        out_specs=pl.BlockSpec((tm, tn), lambda j, i: (i, j)),
        out_shape=jax.ShapeDtypeStruct((R, D), BF),
        scratch_shapes=[pltpu.VMEM((1, D, tn), F32), pltpu.VMEM((1, D, tn), BF), pltpu.SemaphoreType.DMA((1,))],
        compiler_params=_params("arbitrary", "arbitrary"),
        name="v_proj",
    )(u, w_qkv)


def _attn_merges_ctx(n_lat, n_kv, tk):
    n_chunks = n_lat // tk
    return n_kv > n_lat and n_chunks % 2 == 0 and (n_kv - n_lat) % LANES == 0


def _attn_kernel(n_lat, tk, lambda_init, q_ref, k_ref, v_ref, lam_ref, sg_ref, o_ref,
                 sa_ref, sb_ref, sc_ref, m_ref, l_ref, acc_ref):
    hd = q_ref.shape[1] // 2
    n_kv = k_ref.shape[0]
    n_chunks = n_lat // tk
    has_ctx = n_kv > n_lat
    m_ref[...] = jnp.full_like(m_ref, -jnp.inf)
    l_ref[...] = jnp.zeros_like(l_ref)
    acc_ref[...] = jnp.zeros_like(acc_ref)

    def scores(kc, dst):
        for m in range(2):
            dst[m, :, :kc.shape[0]] = lax.dot_general(
                q_ref[:, m * hd:(m + 1) * hd], kc[:, m * hd:(m + 1) * hd],
                (((1,), (1,)), ((), ())), preferred_element_type=F32)

    def absorb(src, vc):
        scaled = []
        for m in range(2):
            s = src[m, :, :vc.shape[0]]
            m_old = m_ref[m]
            m_new = jnp.maximum(m_old, jnp.max(s, axis=-1, keepdims=True))
            alpha = jnp.exp2(m_old - m_new)
            p = jnp.exp2(s - m_new)
            l_ref[m] = alpha * l_ref[m] + jnp.sum(p, axis=-1, keepdims=True)
            m_ref[m] = m_new
            scaled.append((alpha, p.astype(BF)))
        for m in range(2):
            alpha, p = scaled[m]
            acc_ref[m] = alpha * acc_ref[m] + jnp.dot(p, vc, preferred_element_type=F32)

    def k_chunk(c):
        return k_ref[pl.ds(pl.multiple_of(c * tk, tk), tk), :]

    def v_chunk(c):
        return v_ref[pl.ds(pl.multiple_of(c * tk, tk), tk), :]

    scores(k_chunk(0), sa_ref)

    def pair(i, carry):
        c = 2 * i
        scores(k_chunk(c + 1), sb_ref)
        absorb(sa_ref, v_chunk(c))
        scores(k_chunk(c + 2), sa_ref)
        absorb(sb_ref, v_chunk(c + 1))
        return carry

    n_pairs = (n_chunks - 1) // 2
    lax.fori_loop(0, n_pairs, pair, 0)
    c = 2 * n_pairs
    if _attn_merges_ctx(n_lat, n_kv, tk):
        scores(k_ref[(c + 1) * tk:n_kv, :], sb_ref)
        absorb(sa_ref, v_chunk(c))
        absorb(sb_ref, v_ref[(c + 1) * tk:n_kv, :])
    else:
        if n_chunks - c == 2:
            scores(k_chunk(c + 1), sb_ref)
            absorb(sa_ref, v_chunk(c))
            if has_ctx:
                scores(k_ref[n_lat:n_kv, :], sc_ref)
            absorb(sb_ref, v_chunk(c + 1))
        else:
            if has_ctx:
                scores(k_ref[n_lat:n_kv, :], sc_ref)
            absorb(sa_ref, v_chunk(c))
        if has_ctx:
            absorb(sc_ref, v_ref[n_lat:n_kv, :])

    lf = lam_ref[...]
    lam = (jnp.exp(jnp.sum(lf[0:1] * lf[1:2], axis=-1, keepdims=True))
           - jnp.exp(jnp.sum(lf[2:3] * lf[3:4], axis=-1, keepdims=True)) + lambda_init)
    o = acc_ref[0] / l_ref[0] - lam * (acc_ref[1] / l_ref[1])
    o = o * lax.rsqrt(jnp.mean(o * o, axis=-1, keepdims=True) + NORM_EPS)
    o_ref[...] = ((o * sg_ref[...]) * (1.0 - lambda_init)).astype(BF)


def _diff_attention(qk, v, lambdas, subln_gain, n_lat, n_heads, lambda_init):
    R, D = v.shape
    vd = D // n_heads
    tq = _pick(n_lat, 512, BF16_SUBLANES)
    tk = _pick(n_lat, 2048, LANES)
    return pl.pallas_call(
        functools.partial(_attn_kernel, n_lat, tk, lambda_init),
        grid=(n_heads, n_lat // tq),
        in_specs=[pl.BlockSpec((tq, vd), lambda h, i: (i, h)),
                  pl.BlockSpec((R, vd), lambda h, i: (0, n_heads + h)),
                  pl.BlockSpec((R, vd), lambda h, i: (0, h)),
                  pl.BlockSpec(lambdas.shape, lambda h, i: (0, 0)),
                  pl.BlockSpec((1, vd), lambda h, i: (0, 0))],
        out_specs=pl.BlockSpec((tq, vd), lambda h, i: (i, h)),
        out_shape=jax.ShapeDtypeStruct((n_lat, D), BF),
        scratch_shapes=[pltpu.VMEM((2, tq, tk), F32),
                        pltpu.VMEM((2, tq, tk + (R - n_lat if _attn_merges_ctx(n_lat, R, tk) else 0)), F32),
                        pltpu.VMEM((2, tq, (R - n_lat) or LANES), F32),
                        pltpu.VMEM((2, tq, 1), F32), pltpu.VMEM((2, tq, 1), F32),
                        pltpu.VMEM((2, tq, vd), F32)],
        compiler_params=_params("arbitrary", "arbitrary"),
        name="diff_attention",
    )(qk, qk, v, lambdas, subln_gain.reshape(1, vd))


def _router_kernel(n_experts, x_ref, g_ref, sh_ref, sc_ref, w_ref, comb_ref, sel_ref):
    u = _modnorm_rows(x_ref[...], g_ref[...], sh_ref[...], sc_ref[...]).astype(BF)
    logits = jnp.dot(u, w_ref[...], preferred_element_type=F32)
    lane = lax.broadcasted_iota(jnp.int32, logits.shape, 1)
    neg = jnp.float32(-jnp.inf)
    l1 = jnp.where(lane < n_experts, logits, neg)
    m1 = jnp.max(l1, axis=-1, keepdims=True)
    i1 = jnp.min(jnp.where(l1 == m1, lane, LANES), axis=-1, keepdims=True)
    sel1 = lane == i1
    l2 = jnp.where(sel1, neg, l1)
    m2 = jnp.max(l2, axis=-1, keepdims=True)
    i2 = jnp.min(jnp.where(l2 == m2, lane, LANES), axis=-1, keepdims=True)
    sel2 = lane == i2
    e2 = jnp.exp(m2 - m1)
    den = 1.0 + e2
    comb_ref[...] = jnp.where(sel1, 1.0 / den, 0.0) + jnp.where(sel2, e2 / den, 0.0)
    sel_ref[...] = (sel1 | sel2).astype(jnp.int32)


def _router(x, gain, shift, scale, w_router_pad, n_experts):
    S, D = x.shape
    tm = _pick(S, 512, 8)
    vec = pl.BlockSpec((1, D), lambda i: (0, 0))
    out = pl.BlockSpec((tm, LANES), lambda i: (i, 0))
    return pl.pallas_call(
        functools.partial(_router_kernel, n_experts),
        grid=(S // tm,),
        in_specs=[pl.BlockSpec((tm, D), lambda i: (i, 0)), vec, vec, vec,
                  pl.BlockSpec((D, LANES), lambda i: (0, 0))],
        out_specs=[out, out],
        out_shape=[jax.ShapeDtypeStruct((S, LANES), F32), jax.ShapeDtypeStruct((S, LANES), jnp.int32)],
        compiler_params=_params("arbitrary"),
        name="moe_router",
    )(x, gain.reshape(1, D), shift.reshape(1, D), scale.reshape(1, D), w_router_pad)


def _row_copy(src_hbm, row, dst, r, sem):
    return pltpu.make_async_copy(src_hbm.at[pl.ds(row, 1)], dst.at[pl.ds(r, 1)], sem)


ROW_DMA_UNROLL = 8


def _gather_norm_kernel(tok_ref, nrows_ref, x_hbm, g_ref, sh_ref, sc_ref, o_ref, buf, sem):
    tg = buf.shape[1]
    t = pl.program_id(0)

    def used(tile):
        return tile * tg < nrows_ref[0]

    def issue(tile):
        slot = tile % 2

        def one(r2, c):
            for p in range(2):
                r = 2 * r2 + p
                _row_copy(x_hbm, tok_ref[tile * tg + r], buf.at[slot], r, sem.at[slot]).start(priority=p)
            return c

        lax.fori_loop(0, tg // 2, one, 0, unroll=ROW_DMA_UNROLL // 2)

    def wait(tile):
        slot = tile % 2

        def one(r, c):
            _row_copy(x_hbm, 0, buf.at[slot], r, sem.at[slot]).wait()
            return c

        lax.fori_loop(0, tg, one, 0, unroll=ROW_DMA_UNROLL)

    @pl.when((t == 0) & used(0))
    def _():
        issue(0)

    @pl.when((t + 1 < pl.num_programs(0)) & used(t + 1))
    def _():
        issue(t + 1)

    @pl.when(used(t))
    def _():
        wait(t)
        o_ref[...] = _modnorm_rows(buf[t % 2], g_ref[...], sh_ref[...], sc_ref[...]).astype(BF)

    @pl.when(jnp.logical_not(used(t)))
    def _():
        o_ref[...] = jnp.zeros_like(o_ref)


def _gather_norm(x, row_token, n_rows_used, gain, shift, scale):
    D = x.shape[1]
    Rs = row_token.shape[0]
    tg = _pick(Rs, 256, BF16_SUBLANES)
    vec = pl.BlockSpec((1, D), lambda t, tok, n: (0, 0))
    return pl.pallas_call(
        _gather_norm_kernel,
        grid_spec=pltpu.PrefetchScalarGridSpec(
            num_scalar_prefetch=2,
            grid=(Rs // tg,),
            in_specs=[pl.BlockSpec(memory_space=pl.ANY), vec, vec, vec],
            out_specs=pl.BlockSpec((tg, D), lambda t, tok, n: (t, 0)),
            scratch_shapes=[pltpu.VMEM((2, tg, D), F32), pltpu.SemaphoreType.DMA((2,))]),
        out_shape=jax.ShapeDtypeStruct((Rs, D), BF),
        compiler_params=_params("arbitrary"),
        name="moe_gather_norm",
    )(row_token, n_rows_used, x, gain.reshape(1, D), shift.reshape(1, D), scale.reshape(1, D))


SCHED_EXPERT, SCHED_RUN_START, SCHED_FIRST_RUN, SCHED_NEXT_EXPERT, SCHED_LAST_RUN = range(5)


def _staged_expert_weights(sched_ref, cnt_ref, w_hbm, col_blocks, wstage, w_bf, sem):
    j, t = pl.program_id(0), pl.program_id(1)
    nj, n_t = pl.num_programs(0), pl.num_programs(1)
    tn = wstage.shape[-1]

    def copies(e, jj):
        return [pltpu.make_async_copy(w_hbm.at[0, e, :, pl.ds(pl.multiple_of(cb * tn, tn), tn)],
                                      wstage.at[k], sem.at[k])
                for k, cb in enumerate(col_blocks(jj))]

    def sched(row):
        return sched_ref[row * n_t + t]

    @pl.when(sched(SCHED_RUN_START) == 1)
    def _():
        e = sched(SCHED_EXPERT)

        @pl.when((j == 0) & (sched(SCHED_FIRST_RUN) == 1))
        def _():
            for cp in copies(e, j):
                cp.start()

        for k, cp in enumerate(copies(e, j)):
            cp.wait()
            w_bf[k] = wstage[k].astype(BF)

        next_j = j + sched(SCHED_LAST_RUN)

        @pl.when(next_j < nj)
        def _():
            for cp in copies(sched(SCHED_NEXT_EXPERT), next_j):
                cp.start()


def _moe_gu_kernel(sched_ref, cnt_ref, a_ref, w_hbm, o_ref, wstage, w_bf, sem):
    t = pl.program_id(1)
    nj = pl.num_programs(0)

    @pl.when(t < cnt_ref[0])
    def _():
        _staged_expert_weights(sched_ref, cnt_ref, w_hbm, lambda jj: (jj, jj + nj), wstage, w_bf, sem)
        a = a_ref[...]
        g = jnp.dot(a, w_bf[0], preferred_element_type=F32)
        u = jnp.dot(a, w_bf[1], preferred_element_type=F32)
        o_ref[...] = ((g * _sigmoid(g)) * u).astype(BF)

    @pl.when(t >= cnt_ref[0])
    def _():
        o_ref[...] = jnp.zeros_like(o_ref)


def _moe_gu(a, w_gu, sched, counts, tm):
    Rs, D = a.shape
    F = w_gu.shape[3] // 2
    tn = _pick(F, 512, LANES)
    tile = lambda t, cnt: jnp.minimum(t, cnt[0] - 1)
    return pl.pallas_call(
        _moe_gu_kernel,
        grid_spec=pltpu.PrefetchScalarGridSpec(
            num_scalar_prefetch=2,
            grid=(F // tn, Rs // tm),
            in_specs=[pl.BlockSpec((tm, D), lambda j, t, sc, cnt: (tile(t, cnt), 0)),
                      pl.BlockSpec(memory_space=pl.ANY)],
            out_specs=pl.BlockSpec((tm, tn), lambda j, t, sc, cnt: (t, j)),
            scratch_shapes=[pltpu.VMEM((2, D, tn), F32), pltpu.VMEM((2, D, tn), BF),
                            pltpu.SemaphoreType.DMA((2,))]),
        out_shape=jax.ShapeDtypeStruct((Rs, F), BF),
        compiler_params=_params("arbitrary", "arbitrary"),
        name="moe_gu",
    )(sched, counts, a, w_gu)


def _moe_down_kernel(sched_ref, cnt_ref, a_ref, w_hbm, o_ref, wstage, w_bf, sem):
    t = pl.program_id(1)

    @pl.when(t < cnt_ref[0])
    def _():
        _staged_expert_weights(sched_ref, cnt_ref, w_hbm, lambda jj: (jj,), wstage, w_bf, sem)
        o_ref[...] = jnp.dot(a_ref[...], w_bf[0], preferred_element_type=F32)

    @pl.when(t >= cnt_ref[0])
    def _():
        o_ref[...] = jnp.zeros_like(o_ref)


def _moe_down(h, w_down, sched, counts, tm):
    Rs, F = h.shape
    D = w_down.shape[3]
    tn = _pick(D, 1024, LANES)
    tile = lambda t, cnt: jnp.minimum(t, cnt[0] - 1)
    return pl.pallas_call(
        _moe_down_kernel,
        grid_spec=pltpu.PrefetchScalarGridSpec(
            num_scalar_prefetch=2,
            grid=(D // tn, Rs // tm),
            in_specs=[pl.BlockSpec((tm, F), lambda j, t, sc, cnt: (tile(t, cnt), 0)),
                      pl.BlockSpec(memory_space=pl.ANY)],
            out_specs=pl.BlockSpec((tm, tn), lambda j, t, sc, cnt: (t, j)),
            scratch_shapes=[pltpu.VMEM((1, F, tn), F32), pltpu.VMEM((1, F, tn), BF),
                            pltpu.SemaphoreType.DMA((1,))]),
        out_shape=jax.ShapeDtypeStruct((Rs, D), F32),
        compiler_params=_params("arbitrary", "arbitrary"),
        name="moe_down",
    )(sched, counts, h, w_down)


def _combine_kernel(pa_ref, pb_ref, y_hbm, x_ref, g_ref, wa_ref, wb_ref, o_ref, ya, yb, sem):
    tt = ya.shape[1]
    t = pl.program_id(0)

    def issue(tile):
        slot = tile % 2

        def one(r, c):
            _row_copy(y_hbm, pa_ref[tile * tt + r], ya.at[slot], r, sem.at[0, slot]).start(priority=0)
            _row_copy(y_hbm, pb_ref[tile * tt + r], yb.at[slot], r, sem.at[1, slot]).start(priority=1)
            return c

        lax.fori_loop(0, tt, one, 0, unroll=ROW_DMA_UNROLL)

    def wait(tile):
        slot = tile % 2

        def one(r, c):
            _row_copy(y_hbm, 0, ya.at[slot], r, sem.at[0, slot]).wait()
            _row_copy(y_hbm, 0, yb.at[slot], r, sem.at[1, slot]).wait()
            return c

        lax.fori_loop(0, tt, one, 0, unroll=ROW_DMA_UNROLL)

    @pl.when(t == 0)
    def _():
        issue(0)

    @pl.when(t + 1 < pl.num_programs(0))
    def _():
        issue(t + 1)

    wait(t)
    slot = t % 2
    o_ref[...] = x_ref[...] + g_ref[...] * (wa_ref[...] * ya[slot] + wb_ref[...] * yb[slot])


def _moe_combine(x, y, pos_a, pos_b, w_a, w_b, gate):
    S, D = x.shape
    tt = _pick(S, 256, 8)
    col = pl.BlockSpec((tt, 1), lambda t, pa, pb: (t, 0))
    return pl.pallas_call(
        _combine_kernel,
        grid_spec=pltpu.PrefetchScalarGridSpec(
            num_scalar_prefetch=2,
            grid=(S // tt,),
            in_specs=[pl.BlockSpec(memory_space=pl.ANY),
                      pl.BlockSpec((tt, D), lambda t, pa, pb: (t, 0)),
                      pl.BlockSpec((1, D), lambda t, pa, pb: (0, 0)),
                      col, col],
            out_specs=pl.BlockSpec((tt, D), lambda t, pa, pb: (t, 0)),
            scratch_shapes=[pltpu.VMEM((2, tt, D), F32), pltpu.VMEM((2, tt, D), F32),
                            pltpu.SemaphoreType.DMA((2, 2))]),
        out_shape=jax.ShapeDtypeStruct((S, D), F32),
        compiler_params=_params("arbitrary"),
        name="moe_combine",
    )(pos_a, pos_b, y, x, gate.reshape(1, D), w_a, w_b)


def _routing_tables(sel, comb, tm):
    S, E = sel.shape
    Rs = TOP_K * S + E * tm
    seli = sel.astype(jnp.int32)
    cnt = jnp.sum(seli, axis=0)
    padded = (cnt + tm - 1) // tm * tm
    gend = jnp.cumsum(padded)
    gstart = gend - padded
    pos = gstart[None, :] + jnp.cumsum(seli, axis=0) - seli
    pos_a = jnp.min(jnp.where(sel, pos, Rs), axis=1).astype(jnp.int32)
    pos_b = jnp.max(jnp.where(sel, pos, -1), axis=1).astype(jnp.int32)
    w_a = jnp.sum(jnp.where(sel & (pos == pos_a[:, None]), comb, 0.0), axis=1, keepdims=True)
    w_b = jnp.sum(jnp.where(sel & (pos == pos_b[:, None]), comb, 0.0), axis=1, keepdims=True)
    tok = jnp.arange(S, dtype=jnp.int32)
    row_token = jnp.zeros((Rs,), jnp.int32).at[jnp.concatenate([pos_a, pos_b])].set(
        jnp.concatenate([tok, tok]), mode="drop")
    n_t = Rs // tm
    tix = jnp.arange(n_t, dtype=jnp.int32)
    tile_expert = jnp.minimum(
        jnp.sum((tix[:, None] * tm >= gend[None, :]).astype(jnp.int32), axis=1), E - 1).astype(jnp.int32)
    n_rows = gend[-1].astype(jnp.int32)
    n_tiles = n_rows // tm
    prev_expert = jnp.concatenate([jnp.full((1,), -1, jnp.int32), tile_expert[:-1]])
    first = (tix < n_tiles) & (tile_expert != prev_expert)
    first_run = first & (tix == 0)
    later_first = first[None, :] & (tix[None, :] > tix[:, None])
    next_first = jnp.min(jnp.where(later_first, tix[None, :], n_t), axis=1)
    last_run = next_first >= n_t
    next_expert = jnp.where(last_run, tile_expert[0], jnp.sum(
        jnp.where(tix[None, :] == next_first[:, None], tile_expert[None, :], 0), axis=1))
    sched = jnp.concatenate([tile_expert, first.astype(jnp.int32), first_run.astype(jnp.int32), next_expert,
                             last_run.astype(jnp.int32)]).astype(jnp.int32)
    return row_token, pos_a, pos_b, w_a, w_b, sched, n_tiles.reshape(1), n_rows.reshape(1)


def _rope_tables(n_lat, n_rows, head_dim):
    n_freq = head_dim // 4
    s = jnp.arange(n_lat, dtype=jnp.int32)
    inv = ROPE_BASE ** (-jnp.arange(n_freq, dtype=F32) / n_freq)
    ang = jnp.stack([s // GRID_W, s % GRID_W], axis=-1).astype(F32)[..., None] * inv
    cos, sin = jnp.cos(ang), jnp.sin(ang)
    zero = jnp.zeros_like(sin)
    cos_t = jnp.stack([cos, cos], axis=2).reshape(n_lat, head_dim)
    sa_t = jnp.stack([-sin, zero], axis=2).reshape(n_lat, head_dim)
    sb_t = jnp.stack([zero, sin], axis=2).reshape(n_lat, head_dim)
    pad = ((0, n_rows - n_lat), (0, 0))
    return jnp.pad(cos_t, pad, constant_values=1.0), jnp.pad(sa_t, pad), jnp.pad(sb_t, pad)


def kernel(x, c, ctx, c_ctx, w_mod, b_mod, norm_gain, conv_w_in, conv_w, conv_w_out, attn_w_qkv, attn_w_o,
           attn_q_gain, attn_k_gain, attn_lambdas, attn_subln_gain, ffn_w_gu, ffn_w_down, moe_router,
           moe_w_gu, moe_w_down):
    assert x.shape[0] == 1 and w_mod.shape[0] == 2
    S, D = x.shape[1], x.shape[2]
    C = ctx.shape[1]
    R = S + C
    head_dim = attn_q_gain.shape[-1]
    n_heads = D // (2 * head_dim)
    E = moe_router.shape[-1]
    F = ffn_w_down.shape[1]
    f_pad = _round_up(F, 512)

    cc = jnp.stack([c[0], c_ctx]).reshape(2, D, 1)
    mod = _mod_vectors(cc, w_mod, b_mod)[:, :2, :].reshape(2, 2, N_MOD, D)

    xa = jnp.concatenate([x[0], ctx[0]], axis=0)

    m0 = mod[0]
    u = _modnorm(xa, norm_gain[0, 0], m0[:, 0], m0[:, 1], S, R)
    b, cv = _conv_in(u, conv_w_in)
    bz = _conv_mix(b, cv, conv_w[0], S)
    xa = _proj_resid(bz, conv_w_out, xa, m0[:, 2], S)
    u = _modnorm(xa, norm_gain[0, 1], m0[:, 3], m0[:, 4], S, R)
    w_dn = jnp.pad(ffn_w_down[0], ((0, f_pad - F), (0, 0))).astype(BF)
    h = _ffn_gu(u, ffn_w_gu, f_pad)
    xa = _ffn_down(h, w_dn, xa, m0[:, 5], S)

    m1 = mod[1]
    lambda_init = 0.8 - 0.6 * math.exp(-0.3 * 1)
    u = _modnorm(xa, norm_gain[1, 0], m1[:, 0], m1[:, 1], S, R)
    cos, sa, sb = _rope_tables(S, R, head_dim)
    qk_gain = jnp.stack([attn_q_gain[0], attn_k_gain[0]]).reshape(2, 1, head_dim)
    qk = _qk_proj(u, attn_w_qkv, qk_gain, cos, sa, sb, head_dim ** -0.5 * math.log2(math.e))
    v = _v_proj(u, attn_w_qkv)
    o = _diff_attention(qk, v, attn_lambdas[0], attn_subln_gain[0], S, n_heads, lambda_init)
    x1 = _proj_resid(o, attn_w_o, xa, m1[:, 2], S)

    tm = _pick(S, 512, BF16_SUBLANES)
    w_r = jnp.pad(moe_router[0], ((0, 0), (0, LANES - E))).astype(BF)
    comb, selm = _router(x1, norm_gain[1, 1], m1[0, 3], m1[0, 4], w_r, E)
    row_token, pos_a, pos_b, w_a, w_b, sched, counts, n_rows = _routing_tables(
        selm[:, :E] > 0, comb[:, :E], tm)
    us = _gather_norm(x1, row_token, n_rows, norm_gain[1, 1], m1[0, 3], m1[0, 4])
    hs = _moe_gu(us, moe_w_gu, sched, counts, tm)
    ys = _moe_down(hs, moe_w_down, sched, counts, tm)
    out = _moe_combine(x1, ys, pos_a, pos_b, w_a, w_b, m1[0, 5])
    return out[None]
```
